```python
import jax, jax.numpy as jnp
from jax import lax
import numpy as np

D_MODEL = 1024
BATCH = 4
SEQ = 4096
DEPTH = 2

HEAD_DIM = 64
ROT_DIM = HEAD_DIM // 4
ROPE_THETA = 500000.0
A_HEADS = (D_MODEL // 2) // (2 * HEAD_DIM)
B_HEADS = (D_MODEL // 2) // HEAD_DIM
KV_LORA = D_MODEL // 4
IDX_HEADS = 4
IDX_DIM = 64
TOPK_MAX = 256
QBLOCK = 128
C_WIDTH = D_MODEL // 2
CONV_W = 3
POOL_WINDOWS = (2, 4, 8, 16)
POOL_WIDTH = D_MODEL // 2
POOL_GROUP = POOL_WIDTH // len(POOL_WINDOWS)
MEM_LEN = 256
X_HEADS = 4
X_HEAD_DIM = D_MODEL // 8
D_FF = 4 * D_MODEL
EPS = 1e-6
N_EVEN = (DEPTH + 1) // 2
N_ODD = DEPTH // 2
EVEN_SPLITS = (A_HEADS * 2 * HEAD_DIM, A_HEADS * 2 * HEAD_DIM, A_HEADS * 2 * HEAD_DIM,
               B_HEADS * HEAD_DIM, KV_LORA, IDX_HEADS * IDX_DIM, IDX_DIM, IDX_HEADS)
EVEN_IN = sum(EVEN_SPLITS)
ODD_SPLITS = (C_WIDTH, C_WIDTH, C_WIDTH, POOL_WIDTH)
ODD_IN = sum(ODD_SPLITS)

kernel_name = 'hybrid_diff_dsa_conv_pool_trunk'


def rmsnorm(x, g):
    xf = x.astype(jnp.float32)
    y = xf * lax.rsqrt(jnp.mean(xf * xf, axis=-1, keepdims=True) + EPS)
    return (y * g.astype(jnp.float32)).astype(x.dtype)


def _split(a, sizes):
    return jnp.split(a, np.cumsum(sizes)[:-1].tolist(), axis=-1)


def rope_tables(positions):
    inv = ROPE_THETA ** (-jnp.arange(0, ROT_DIM, 2, dtype=jnp.float32) / ROT_DIM)
    ang = positions.astype(jnp.float32)[..., None] * inv
    return jnp.cos(ang)[:, :, None, :], jnp.sin(ang)[:, :, None, :]


def rope(x, cos, sin):
    half = ROT_DIM // 2
    xr, xp = x[..., :ROT_DIM], x[..., ROT_DIM:]
    x1, x2 = xr[..., :half], xr[..., half:]
    rot = jnp.concatenate([x1 * cos - x2 * sin, x2 * cos + x1 * sin], axis=-1)
    return jnp.concatenate([rot.astype(x.dtype), xp], axis=-1)


def even_mixer(h, cos, sin, w_in, gq_a, gk_a, lam, lam_init, g_sub_a,
               g_kv_b, w_kv_up_b, gq_b, gk_b, g_kidx, w_out, k_sel):
    B, S, _ = h.shape
    nb = S // QBLOCK

    def to_blocks(a):
        return jnp.moveaxis(a.reshape((B, nb, QBLOCK) + a.shape[2:]), 1, 0)

    def from_blocks(o):
        return jnp.moveaxis(o, 0, 1).reshape((B, S) + o.shape[3:])

    qa, ka, va, qb, ckv, qi, ki, wi = _split(h @ w_in, EVEN_SPLITS)

    qa = rope(rmsnorm(qa.reshape(B, S, 2 * A_HEADS, HEAD_DIM), gq_a), cos, sin)
    qa = qa.reshape(B, S, A_HEADS, 2, HEAD_DIM)
    ka = rope(rmsnorm(ka.reshape(B, S, 2 * A_HEADS, HEAD_DIM), gk_a), cos, sin)
    ka = ka.reshape(B, S, A_HEADS, 2, HEAD_DIM)
    va = va.reshape(B, S, A_HEADS, 2 * HEAD_DIM)

    kb, vb = jnp.split(rmsnorm(ckv, g_kv_b) @ w_kv_up_b, 2, axis=-1)
    qb = rope(rmsnorm(qb.reshape(B, S, B_HEADS, HEAD_DIM), gq_b), cos, sin)
    kb = rope(rmsnorm(kb.reshape(B, S, B_HEADS, HEAD_DIM), gk_b), cos, sin)
    vb = vb.reshape(B, S, B_HEADS, HEAD_DIM)
    qi = rope(qi.reshape(B, S, IDX_HEADS, IDX_DIM), cos, sin)
    ki = rope(rmsnorm(ki, g_kidx)[:, :, None, :], cos, sin)[:, :, 0, :]
    wi = wi.astype(jnp.float32) * (IDX_HEADS ** -0.5 * IDX_DIM ** -0.5)

    kpos = jnp.arange(S)
    scale = HEAD_DIM ** -0.5

    def a_block(args):
        q, t0 = args
        qpos = t0 + jnp.arange(QBLOCK)
        causal = kpos[None, :] <= qpos[:, None]
        logits = jnp.einsum('bqhcd,bkhcd->bhcqk', q, ka).astype(jnp.float32) * scale
        p = jax.nn.softmax(jnp.where(causal, logits, -jnp.inf), axis=-1)
        attn = p[:, :, 0] - lam * p[:, :, 1]
        return jnp.einsum('bhqk,bkhe->bqhe', attn.astype(va.dtype), va)

    def b_block(args):
        q, qidx, widx, t0 = args
        qpos = t0 + jnp.arange(QBLOCK)
        causal = kpos[None, :] <= qpos[:, None]
        isc = jax.nn.relu(jnp.einsum('bqhd,bkd->bqhk', qidx, ki).astype(jnp.float32))
        isc = jnp.einsum('bqhk,bqh->bqk', isc, widx)
        isc = jnp.where(causal[None], isc, -jnp.inf)
        _, idx = lax.top_k(isc, k_sel)
        k_g = jax.vmap(lambda a, i: a[i])(kb, idx)
        v_g = jax.vmap(lambda a, i: a[i])(vb, idx)
        logits = jnp.einsum('bqhd,bqkhd->bhqk', q, k_g).astype(jnp.float32) * scale
        valid = (idx <= qpos[None, :, None])[:, None]
        p = jax.nn.softmax(jnp.where(valid, logits, -jnp.inf), axis=-1)
        return jnp.einsum('bhqk,bqkhd->bqhd', p.astype(vb.dtype), v_g)

    starts = jnp.arange(nb, dtype=jnp.int32) * QBLOCK
    out_a = from_blocks(lax.map(a_block, (to_blocks(qa), starts)))
    out_a = rmsnorm(out_a, g_sub_a) * (1.0 - lam_init)
    out_b = from_blocks(lax.map(b_block, (to_blocks(qb), to_blocks(qi), to_blocks(wi), starts)))
    y = jnp.concatenate([out_a.reshape(B, S, -1), out_b.reshape(B, S, -1)], axis=-1)
    return y @ w_out


def pool_mixer(z, w_pool, pool_scale):
    B, S, _ = z.shape
    zf = z.astype(jnp.float32).reshape(B, S, len(POOL_WINDOWS), POOL_GROUP)
    csp = jnp.pad(jnp.cumsum(zf, axis=1), ((0, 0), (1, 0), (0, 0), (0, 0)))
    t = jnp.arange(S)
    outs = []
    for g, w in enumerate(POOL_WINDOWS):
        lo = jnp.maximum(t + 1 - w, 0)
        win_sum = csp[:, 1:, g] - csp[:, lo, g]
        cnt = jnp.minimum(t + 1, w).astype(jnp.float32)
        outs.append(win_sum / cnt[None, :, None] - zf[:, :, g])
    pooled = jnp.stack(outs, axis=2).astype(z.dtype)
    y = jnp.einsum('bsgc,gcd->bsgd', pooled, w_pool).reshape(B, S, POOL_WIDTH)
    return y * pool_scale


def odd_mixer(h, w_in, conv_w, w_pool, pool_scale, w_out):
    gb, gc, hc, zd = _split(h @ w_in, ODD_SPLITS)
    u = gc * hc
    conv = lax.conv_general_dilated(
        u, conv_w[:, None, :].astype(u.dtype), window_strides=(1,),
        padding=[(CONV_W - 1, 0)], dimension_numbers=('NWC', 'WIO', 'NWC'),
        feature_group_count=C_WIDTH)
    yc = gb * conv
    yd = pool_mixer(zd, w_pool, pool_scale)
    return jnp.concatenate([yc, yd], axis=-1) @ w_out


def cross_attn(h, m, wq, wkv, gq, gk, wo):
    B, S, _ = h.shape
    q = rmsnorm((h @ wq).reshape(B, S, X_HEADS, X_HEAD_DIM), gq)
    k, v = jnp.split(m @ wkv, 2, axis=-1)
    k = rmsnorm(k.reshape(B, -1, X_HEADS, X_HEAD_DIM), gk)
    v = v.reshape(B, -1, X_HEADS, X_HEAD_DIM)
    logits = jnp.einsum('bqhd,bkhd->bhqk', q, k).astype(jnp.float32) * X_HEAD_DIM ** -0.5
    p = jax.nn.softmax(logits, axis=-1)
    o = jnp.einsum('bhqk,bkhd->bqhd', p.astype(v.dtype), v).reshape(B, S, -1)
    return o @ wo


def mlp(h, w_up, w_down):
    return jnp.square(jax.nn.relu(h @ w_up)) @ w_down


def setup_inputs(seed: int = 0) -> dict:
    key = jax.random.key(seed)
    ks = iter(jax.random.split(key, 48))

    def nrm(shape, scale):
        return scale * jax.random.normal(next(ks), shape, jnp.float32)

    def gain(shape):
        return 1.0 + 0.05 * jax.random.normal(next(ks), shape, jnp.float32)

    d = D_MODEL
    return {
        'x': nrm((BATCH, SEQ, d), 1.0),
        'mem': nrm((BATCH, MEM_LEN, d), 1.0),
        'positions': jnp.tile(jnp.arange(SEQ, dtype=jnp.int32)[None, :], (BATCH, 1)),
        'g_mix': gain((DEPTH, d)),
        'g_xattn': gain((DEPTH, d)),
        'g_mem': gain((DEPTH, d)),
        'g_mlp': gain((DEPTH, d)),
        'wq_x': nrm((DEPTH, d, X_HEADS * X_HEAD_DIM), d ** -0.5),
        'wkv_x': nrm((DEPTH, d, 2 * X_HEADS * X_HEAD_DIM), d ** -0.5),
        'gq_x': gain((DEPTH, X_HEAD_DIM)),
        'gk_x': gain((DEPTH, X_HEAD_DIM)),
        'wo_x': nrm((DEPTH, X_HEADS * X_HEAD_DIM, d), (X_HEADS * X_HEAD_DIM) ** -0.5),
        'w_up': nrm((DEPTH, d, D_FF), d ** -0.5),
        'w_down': nrm((DEPTH, D_FF, d), D_FF ** -0.5),
        'w_in_e': nrm((N_EVEN, d, EVEN_IN), d ** -0.5),
        'gq_a': gain((N_EVEN, HEAD_DIM)),
        'gk_a': gain((N_EVEN, HEAD_DIM)),
        'lam_q1': nrm((N_EVEN, HEAD_DIM), 0.1),
        'lam_k1': nrm((N_EVEN, HEAD_DIM), 0.1),
        'lam_q2': nrm((N_EVEN, HEAD_DIM), 0.1),
        'lam_k2': nrm((N_EVEN, HEAD_DIM), 0.1),
        'g_sub_a': gain((N_EVEN, 2 * HEAD_DIM)),
        'g_kv_b': gain((N_EVEN, KV_LORA)),
        'w_kv_up_b': nrm((N_EVEN, KV_LORA, 2 * B_HEADS * HEAD_DIM), KV_LORA ** -0.5),
        'gq_b': gain((N_EVEN, HEAD_DIM)),
        'gk_b': gain((N_EVEN, HEAD_DIM)),
        'g_kidx': gain((N_EVEN, IDX_DIM)),
        'w_out_e': nrm((N_EVEN, d, d), d ** -0.5),
        'w_in_o': nrm((N_ODD, d, ODD_IN), d ** -0.5),
        'conv_w': nrm((N_ODD, CONV_W, C_WIDTH), CONV_W ** -0.5),
        'w_pool': nrm((N_ODD, len(POOL_WINDOWS), POOL_GROUP, POOL_GROUP), POOL_GROUP ** -0.5),
        'pool_scale': gain((N_ODD, POOL_WIDTH)),
        'w_out_o': nrm((N_ODD, d, d), d ** -0.5),
    }


def reference(x, mem, positions, g_mix, g_xattn, g_mem, g_mlp, wq_x, wkv_x, gq_x, gk_x,
              wo_x, w_up, w_down, w_in_e, gq_a, gk_a, lam_q1, lam_k1, lam_q2, lam_k2,
              g_sub_a, g_kv_b, w_kv_up_b, gq_b, gk_b, g_kidx, w_out_e, w_in_o, conv_w,
              w_pool, pool_scale, w_out_o):
    S = x.shape[1]
    k_sel = min(TOPK_MAX, S // 4)
    cos, sin = rope_tables(positions)
    for i in range(DEPTH):
        j = i // 2
        h = rmsnorm(x, g_mix[i])
        if i % 2 == 0:
            lam_init = 0.8 - 0.6 * float(np.exp(-0.3 * i))
            f32 = jnp.float32
            lam = (jnp.exp(jnp.sum(lam_q1[j].astype(f32) * lam_k1[j].astype(f32)))
                   - jnp.exp(jnp.sum(lam_q2[j].astype(f32) * lam_k2[j].astype(f32)))
                   + lam_init)
            mix = even_mixer(h, cos, sin, w_in_e[j], gq_a[j], gk_a[j], lam, lam_init,
                             g_sub_a[j], g_kv_b[j], w_kv_up_b[j], gq_b[j], gk_b[j],
                             g_kidx[j], w_out_e[j], k_sel)
        else:
            mix = odd_mixer(h, w_in_o[j], conv_w[j], w_pool[j], pool_scale[j], w_out_o[j])
        x = x + mix
        x = x + cross_attn(rmsnorm(x, g_xattn[i]), rmsnorm(mem, g_mem[i]),
                           wq_x[i], wkv_x[i], gq_x[i], gk_x[i], wo_x[i])
        x = x + mlp(rmsnorm(x, g_mlp[i]), w_up[i], w_down[i])
    return x
```

```python
import functools

import numpy as np
import jax
import jax.numpy as jnp
from jax import lax
from jax.experimental import pallas as pl
from jax.experimental.pallas import tpu as pltpu

F32 = jnp.float32
BF16 = jnp.bfloat16

HEAD_DIM = 64
ROT_DIM = HEAD_DIM // 4
ROT_HALF = ROT_DIM // 2
ROPE_THETA = 500000.0
IDX_HEADS = 4
IDX_DIM = 64
TOPK_MAX = 256
CONV_W = 3
POOL_WINDOWS = (2, 4, 8, 16)
POOL_HALO = 16
X_HEADS = 4
X_HEAD_DIM = 128
EPS = 1e-6

LANES = 128
SUBLANES = 8
VMEM_LIMIT_BYTES = 56 * 1024 * 1024

KEY_NEG_INF = np.int32(-2139095041)
INT32_MIN = np.int32(-(2 ** 31))

NT_DIMS = (((1,), (1,)), ((), ()))

IDX_PRECISION = lax.Precision.HIGHEST


def _cparams(*sem):
    return pltpu.CompilerParams(dimension_semantics=sem, vmem_limit_bytes=VMEM_LIMIT_BYTES)


def _full_spec(arr):
    nd = arr.ndim
    return pl.BlockSpec(arr.shape, lambda *_: (0,) * nd)


def _rms(x, g):
    ms = jnp.mean(x * x, axis=-1, keepdims=True)
    return x * lax.rsqrt(ms + EPS) * g


def _tile_lanes(t, width):
    reps = width // t.shape[-1]
    return t if reps == 1 else jnp.concatenate([t] * reps, axis=1)


def _rope(y, c, s_lo, s_hi):
    w = y.shape[-1]
    c, s_lo, s_hi = _tile_lanes(c, w), _tile_lanes(s_lo, w), _tile_lanes(s_hi, w)
    upper = pltpu.roll(y, w - ROT_HALF, axis=1)
    lower = pltpu.roll(y, ROT_HALF, axis=1)
    return y * c + upper * s_lo + lower * s_hi


def _headnorm64(y, blockdiag, g):
    ss = jnp.dot((y * y).astype(BF16), blockdiag, preferred_element_type=F32)
    return y * lax.rsqrt(ss * (1.0 / HEAD_DIM) + EPS) * g


def _in0_body(x_ref, gmix_ref, wmain_ref, widx_ref, wkvup_ref, c_ref, slo_ref, shi_ref,
              bd_ref, gqa_ref, gka_ref, gqb_ref, gkb_ref, gkv_ref, gkidx_ref,
              qa_ref, ka_ref, va_ref, qb_ref, kb_ref, vbt_ref, qi_ref, kik_ref, wi_ref,
              *, idx_precision):
    h = _rms(x_ref[...], gmix_ref[...])
    hb = h.astype(BF16)
    y = jnp.dot(hb, wmain_ref[...], preferred_element_type=F32)
    c, s_lo, s_hi = c_ref[...], slo_ref[...], shi_ref[...]
    bd = bd_ref[...]
    scale = HEAD_DIM ** -0.5

    qa = _rope(_headnorm64(y[:, 0:512], bd, gqa_ref[...]), c, s_lo, s_hi) * scale
    qa_ref[...] = qa.astype(BF16)
    ka = _rope(_headnorm64(y[:, 512:1024], bd, gka_ref[...]), c, s_lo, s_hi)
    ka_ref[...] = ka.astype(BF16)
    va_ref[...] = y[:, 1024:1536].astype(BF16)
    qb = _rope(_headnorm64(y[:, 1536:2048], bd, gqb_ref[...]), c, s_lo, s_hi) * scale
    qb_ref[...] = qb.astype(BF16)

    ckv = _rms(y[:, 2048:2304], gkv_ref[...]).astype(BF16)
    kv = jnp.dot(ckv, wkvup_ref[...], preferred_element_type=F32)
    kb = _rope(_headnorm64(kv[:, 0:512], bd, gkb_ref[...]), c, s_lo, s_hi)
    kb_ref[...] = kb.astype(BF16)
    vbt_ref[0] = kv[:, 512:1024].T.astype(BF16)

    if idx_precision is None:
        yi = jnp.dot(hb, widx_ref[...], preferred_element_type=F32)
    else:
        yi = jnp.dot(h, widx_ref[...], preferred_element_type=F32, precision=idx_precision)
    qi_ref[...] = _rope(yi[:, 0:256], c, s_lo, s_hi).astype(qi_ref.dtype)
    blk = yi[:, 256:384]
    lane = lax.broadcasted_iota(jnp.int32, blk.shape, 1)
    ms = jnp.sum(jnp.where(lane < IDX_DIM, blk * blk, 0.0), axis=-1, keepdims=True) * (1.0 / IDX_DIM)
    kin = blk * lax.rsqrt(ms + EPS) * gkidx_ref[...]
    kir = _rope(kin, c, s_lo, s_hi)
    kik = kir + pltpu.roll(kir, IDX_DIM, axis=1)
    kik_ref[...] = kik.astype(kik_ref.dtype)
    wi_ref[...] = blk * (IDX_HEADS ** -0.5 * IDX_DIM ** -0.5)


def _in0_call(x2, gmix, wmain, widx, wkvup, c, s_lo, s_hi, bd, gqa, gka, gqb, gkb, gkv, gkidx,
              *, tm, idx_precision):
    t, d = x2.shape
    row = lambda w: pl.BlockSpec((tm, w), lambda i: (i, 0))
    fulls = [gmix, wmain, widx, wkvup]
    gains = [bd, gqa, gka, gqb, gkb, gkv, gkidx]
    in_specs = ([row(d)] + [_full_spec(a) for a in fulls] + [row(LANES)] * 3
                + [_full_spec(a) for a in gains])
    idx_dt = BF16 if idx_precision is None else F32
    sds = jax.ShapeDtypeStruct
    out_shape = ([sds((t, 512), BF16)] * 5 + [sds((t // tm, 512, tm), BF16), sds((t, 256), idx_dt),
                                              sds((t, LANES), idx_dt), sds((t, LANES), F32)])
    out_specs = ([row(512)] * 5 + [pl.BlockSpec((1, 512, tm), lambda i: (i, 0, 0)), row(256), row(LANES),
                                   row(LANES)])
    return pl.pallas_call(
        functools.partial(_in0_body, idx_precision=idx_precision),
        grid=(t // tm,),
        in_specs=in_specs,
        out_specs=out_specs,
        out_shape=out_shape,
        compiler_params=_cparams("parallel"),
        name="in0",
    )(x2, gmix, wmain, widx, wkvup, c, s_lo, s_hi, bd, gqa, gka, gqb, gkb, gkv, gkidx)


def _diff_attn_body(lam_ref, q_ref, k_ref, v_ref, gsub_ref, o_ref,
                    m0_ref, l0_ref, a0_ref, m1_ref, l1_ref, a1_ref, *, tq, tk, out_scale):
    i, j = pl.program_id(2), pl.program_id(3)

    @pl.when(j == 0)
    def _():
        for m_ref, l_ref, a_ref in ((m0_ref, l0_ref, a0_ref), (m1_ref, l1_ref, a1_ref)):
            m_ref[...] = jnp.full(m_ref.shape, -jnp.inf, F32)
            l_ref[...] = jnp.zeros(l_ref.shape, F32)
            a_ref[...] = jnp.zeros(a_ref.shape, F32)

    @pl.when(j * tk < (i + 1) * tq)
    def _():
        q = q_ref[...]
        k = k_ref[...]
        v = v_ref[...]
        lane = lax.broadcasted_iota(jnp.int32, (1, LANES), 1)
        row = i * tq + lax.broadcasted_iota(jnp.int32, (tq, tk), 0)
        col = j * tk + lax.broadcasted_iota(jnp.int32, (tq, tk), 1)
        causal = col <= row
        maps = ((lane < HEAD_DIM, m0_ref, l0_ref, a0_ref), (lane >= HEAD_DIM, m1_ref, l1_ref, a1_ref))
        for sel, m_ref, l_ref, a_ref in maps:
            qm = q * sel.astype(BF16)
            s = lax.dot_general(qm, k, NT_DIMS, preferred_element_type=F32)
            s = jnp.where(causal, s, -jnp.inf)
            m_old = m_ref[...]
            m_new = jnp.maximum(m_old, jnp.max(s, axis=-1, keepdims=True))
            alpha = jnp.exp(m_old - m_new)
            p = jnp.exp(s - m_new[:, 0:1])
            l_ref[...] = alpha * l_ref[...] + jnp.sum(p, axis=-1, keepdims=True)
            a_ref[...] = alpha * a_ref[...] + jnp.dot(p.astype(BF16), v, preferred_element_type=F32)
            m_ref[...] = m_new

    @pl.when(j == pl.num_programs(3) - 1)
    def _():
        lam = lam_ref[0]
        o = a0_ref[...] / l0_ref[...] - lam * (a1_ref[...] / l1_ref[...])
        o = _rms(o, gsub_ref[...]) * out_scale
        o_ref[...] = o.astype(o_ref.dtype)


def _diff_attn_call(lam, qa, ka, va, gsub, *, batch, seq, tq, tk, out_scale):
    t = qa.shape[0]
    heads = qa.shape[1] // LANES
    nq, nk = seq // tq, seq // tk

    def kv_map(b, h, i, j):
        last = ((i + 1) * tq - 1) // tk
        return (b * nk + jnp.minimum(j, last), h)

    q_spec = pl.BlockSpec((tq, LANES), lambda b, h, i, j: (b * nq + i, h))
    kv_spec = pl.BlockSpec((tk, LANES), kv_map)
    stat = pltpu.VMEM((tq, LANES), F32)
    return pl.pallas_call(
        functools.partial(_diff_attn_body, tq=tq, tk=tk, out_scale=out_scale),
        grid=(batch, heads, nq, nk),
        in_specs=[pl.BlockSpec(memory_space=pltpu.SMEM), q_spec, kv_spec, kv_spec, _full_spec(gsub)],
        out_specs=q_spec,
        out_shape=jax.ShapeDtypeStruct((t, heads * LANES), BF16),
        scratch_shapes=[stat] * 6,
        compiler_params=_cparams("parallel", "parallel", "parallel", "arbitrary"),
        name="diff_attn",
    )(lam, qa, ka, va, gsub)


def _key_to_float(key):
    bits = key ^ ((key >> 31) & jnp.int32(0x7FFFFFFF))
    return lax.bitcast_convert_type(bits, F32)


def _dsa_body(qi_ref, wi_ref, qb_ref, kik_ref, kb_ref, vbt_ref, tri_ref, o_ref, sc_ref, bias_ref, acc_ref,
              *, tq, tk, k_sel, idx_precision):
    i = pl.program_id(1)
    q0 = i * tq
    nkc = (q0 + tq + tk - 1) // tk
    qpos = q0 + lax.broadcasted_iota(jnp.int32, (1, tq), 1)
    lane = lax.broadcasted_iota(jnp.int32, (1, LANES), 1)
    lane_lo = lane < HEAD_DIM
    tiles = tk // LANES
    neg_inf = jnp.float32(-jnp.inf)

    wi_t = wi_ref[...].T
    qi = qi_ref[...]
    q_heads = []
    for h in range(IDX_HEADS):
        pair = qi[:, (h // 2) * LANES:(h // 2 + 1) * LANES]
        q_heads.append(pair * (lane_lo if h % 2 == 0 else ~lane_lo).astype(pair.dtype))
    w_heads = [wi_t[IDX_DIM + h:IDX_DIM + h + 1, :] for h in range(IDX_HEADS)]

    def score_chunk(c, carry):
        kk = kik_ref[pl.ds(pl.multiple_of(c * tk, tk), tk), :]
        acc = jnp.zeros((tk, tq), F32)
        for h in range(IDX_HEADS):
            s = lax.dot_general(kk, q_heads[h], NT_DIMS, preferred_element_type=F32,
                                precision=idx_precision)
            acc = acc + jnp.maximum(s, 0.0) * w_heads[h]
        kpos = c * tk + lax.broadcasted_iota(jnp.int32, (tk, 1), 0)
        sc_ref[c] = jnp.where(kpos <= qpos, acc, neg_inf)
        return carry

    lax.fori_loop(0, nkc, score_chunk, 0)

    acc_rows = 4 * SUBLANES

    def count_ge(cand):
        def body(c, acc):
            hit = jnp.where(sc_ref[c] >= cand, 1.0, 0.0)
            return acc + jnp.sum(hit.reshape(tk // acc_rows, acc_rows, tq), axis=0)
        acc = lax.fori_loop(0, nkc, body, jnp.zeros((acc_rows, tq), F32))
        return jnp.sum(acc, axis=0, keepdims=True)

    ksel = float(k_sel)

    def count_at(cand_key):
        return jnp.where(cand_key <= KEY_NEG_INF, ksel, count_ge(_key_to_float(cand_key)))

    zero_key = jnp.zeros((1, tq), jnp.int32)
    thr = jnp.where(count_at(zero_key) >= ksel, zero_key, INT32_MIN)

    def bit_step(b, thr):
        cand = thr + jnp.left_shift(jnp.int32(1), 30 - b)
        return jnp.where(count_at(cand) >= ksel, cand, thr)

    thr = lax.fori_loop(0, 31, bit_step, thr)
    thr_f = _key_to_float(thr)
    above_f = _key_to_float(thr + 1)
    need = ksel - count_ge(above_f)
    tri = tri_ref[...]

    def select_chunk(c, taken):
        for t in range(tiles):
            st = sc_ref[c, t * LANES:(t + 1) * LANES, :]
            is_above = st >= above_f
            tied = jnp.where(is_above, 0.0, jnp.where(st >= thr_f, 1.0, 0.0))
            prefix = jnp.dot(tri, tied.astype(BF16), preferred_element_type=F32)
            tie_sel = jnp.where((taken + prefix) <= need, tied, 0.0)
            sel = jnp.where(is_above, 0.0, jnp.where(tie_sel > 0.0, 0.0, neg_inf))
            kpos = c * tk + t * LANES + lax.broadcasted_iota(jnp.int32, (LANES, 1), 0)
            bias_ref[c, t * LANES:(t + 1) * LANES, :] = jnp.where(kpos <= qpos, sel, neg_inf)
            taken = taken + prefix[LANES - 1:LANES, :]
        return taken

    lax.fori_loop(0, nkc, select_chunk, jnp.zeros((1, tq), F32))

    n_heads = qb_ref.shape[1] // HEAD_DIM
    q_masked = []
    for hd in range(n_heads):
        q_pair = qb_ref[:, (hd // 2) * LANES:(hd // 2 + 1) * LANES]
        q_masked.append(q_pair * (lane_lo if hd % 2 == 0 else ~lane_lo).astype(BF16))

    def logits(c, hd):
        kc = kb_ref[pl.ds(pl.multiple_of(c * tk, tk), tk), (hd // 2) * LANES:(hd // 2 + 1) * LANES]
        return lax.dot_general(kc, q_masked[hd], NT_DIMS, preferred_element_type=F32) + bias_ref[c]

    def max_chunk(c, carry):
        return tuple(jnp.maximum(m, jnp.max(logits(c, hd).reshape(tk // SUBLANES, SUBLANES, tq), axis=0))
                     for hd, m in enumerate(carry))

    m8 = lax.fori_loop(0, nkc, max_chunk, tuple(jnp.full((SUBLANES, tq), neg_inf, F32) for _ in range(n_heads)))
    shift = [jnp.max(m, axis=0, keepdims=True) for m in m8]

    acc_ref[...] = jnp.zeros(acc_ref.shape, F32)

    def pv_chunk(c, carry):
        out = []
        for hd, l8 in enumerate(carry):
            pe = jnp.exp(logits(c, hd) - shift[hd])
            vt = vbt_ref[c, hd * HEAD_DIM:(hd + 1) * HEAD_DIM, :]
            acc_ref[hd * HEAD_DIM:(hd + 1) * HEAD_DIM, :] += jnp.dot(vt, pe.astype(BF16),
                                                                   preferred_element_type=F32)
            out.append(l8 + jnp.sum(pe.reshape(tk // SUBLANES, SUBLANES, tq), axis=0))
        return tuple(out)

    l8 = lax.fori_loop(0, nkc, pv_chunk, tuple(jnp.zeros((SUBLANES, tq), F32) for _ in range(n_heads)))
    for p in range(n_heads // 2):
        heads = range(2 * p, 2 * p + 2)
        o_t = jnp.concatenate([acc_ref[hd * HEAD_DIM:(hd + 1) * HEAD_DIM, :]
                               / jnp.sum(l8[hd], axis=0, keepdims=True) for hd in heads], axis=0)
        o_ref[:, p * LANES:(p + 1) * LANES] = o_t.T.astype(o_ref.dtype)


def _dsa_call(qi, wi, qb, kik, kb, vbt, tri, *, batch, seq, tq, tk, k_sel, idx_precision):
    t = qb.shape[0]
    nq, nk = seq // tq, seq // tk
    row = lambda w: pl.BlockSpec((tq, w), lambda b, i: (b * nq + i, 0))
    seqb = lambda w: pl.BlockSpec((seq, w), lambda b, i: (b, 0))
    vbt_spec = pl.BlockSpec((nk,) + vbt.shape[1:], lambda b, i: (b, 0, 0))
    return pl.pallas_call(
        functools.partial(_dsa_body, tq=tq, tk=tk, k_sel=k_sel, idx_precision=idx_precision),
        grid=(batch, nq),
        in_specs=[row(qi.shape[1]), row(LANES), row(qb.shape[1]), seqb(LANES), seqb(kb.shape[1]),
                  vbt_spec, _full_spec(tri)],
        out_specs=row(qb.shape[1]),
        out_shape=jax.ShapeDtypeStruct((t, qb.shape[1]), BF16),
        scratch_shapes=[pltpu.VMEM((nk, tk, tq), F32), pltpu.VMEM((nk, tk, tq), F32),
                        pltpu.VMEM((qb.shape[1], tq), F32)],
        compiler_params=_cparams("parallel", "arbitrary"),
        name="dsa",
    )(qi, wi, qb, kik, kb, vbt, tri)


def _memkv_body(mem_ref, g_ref, w_ref, gk_ref, k_ref, v_ref):
    hm = _rms(mem_ref[...], g_ref[...]).astype(BF16)
    kv = jnp.dot(hm, w_ref[...], preferred_element_type=F32)
    half = kv.shape[1] // 2
    for h in range(X_HEADS):
        kh = kv[:, h * X_HEAD_DIM:(h + 1) * X_HEAD_DIM]
        k_ref[:, h * X_HEAD_DIM:(h + 1) * X_HEAD_DIM] = _rms(kh, gk_ref[...]).astype(BF16)
    v_ref[...] = kv[:, half:].astype(BF16)


def _memkv_call(mem2, g, w, gk):
    rows, d = mem2.shape
    half = w.shape[1] // 2
    tm = min(rows, 512)
    row = lambda wd: pl.BlockSpec((tm, wd), lambda i: (i, 0))
    return pl.pallas_call(
        _memkv_body,
        grid=(rows // tm,),
        in_specs=[row(d), _full_spec(g), _full_spec(w), _full_spec(gk)],
        out_specs=[row(half), row(half)],
        out_shape=[jax.ShapeDtypeStruct((rows, half), BF16)] * 2,
        compiler_params=_cparams("parallel"),
        name="memkv",
    )(mem2, g, w, gk)


def _xattn_tail(x1, gx_ref, wq_ref, gq_ref, kx_ref, vx_ref, wo_ref):
    hx = _rms(x1, gx_ref[...]).astype(BF16)
    q = jnp.dot(hx, wq_ref[...], preferred_element_type=F32)
    outs = []
    for h in range(X_HEADS):
        cols = slice(h * X_HEAD_DIM, (h + 1) * X_HEAD_DIM)
        qh = _rms(q[:, cols], gq_ref[...]).astype(BF16)
        s = lax.dot_general(qh, kx_ref[0, :, cols], NT_DIMS, preferred_element_type=F32)
        p = jnp.exp(s - jnp.max(s, axis=-1, keepdims=True))
        o = (jnp.dot(p.astype(BF16), vx_ref[0, :, cols], preferred_element_type=F32)
             / jnp.sum(p, axis=-1, keepdims=True))
        outs.append(o.astype(BF16))
    o = jnp.concatenate(outs, axis=1)
    return x1 + jnp.dot(o, wo_ref[...], preferred_element_type=F32)


def _out0_body(x_ref, oa_ref, ob_ref, woa_ref, wob_ref, gx_ref, wq_ref, gq_ref, kx_ref, vx_ref, wo_ref,
               o_ref):
    mix = (jnp.dot(oa_ref[...], woa_ref[...], preferred_element_type=F32)
           + jnp.dot(ob_ref[...], wob_ref[...], preferred_element_type=F32))
    o_ref[...] = _xattn_tail(x_ref[...] + mix, gx_ref, wq_ref, gq_ref, kx_ref, vx_ref, wo_ref)


def _out0_call(x2, oa, ob, woa, wob, gx, wq, gq, kx, vx, wo, *, seq, tm):
    t, d = x2.shape
    row = lambda w: pl.BlockSpec((tm, w), lambda i: (i, 0))
    mem_spec = pl.BlockSpec((1,) + kx.shape[1:], lambda i: ((i * tm) // seq, 0, 0))
    fulls = [woa, wob, gx, wq, gq]
    return pl.pallas_call(
        _out0_body,
        grid=(t // tm,),
        in_specs=[row(d), row(oa.shape[1]), row(ob.shape[1])] + [_full_spec(a) for a in fulls]
                 + [mem_spec, mem_spec, _full_spec(wo)],
        out_specs=row(d),
        out_shape=jax.ShapeDtypeStruct((t, d), F32),
        compiler_params=_cparams("parallel"),
        name="out0",
    )(x2, oa, ob, woa, wob, gx, wq, gq, kx, vx, wo)


def _mlp_body(x_ref, g_ref, wup_ref, wdn_ref, o_ref, h_ref, acc_ref):
    j = pl.program_id(1)

    @pl.when(j == 0)
    def _():
        h_ref[...] = _rms(x_ref[...], g_ref[...]).astype(BF16)
        acc_ref[...] = jnp.zeros(acc_ref.shape, F32)

    u = jnp.dot(h_ref[...], wup_ref[...], preferred_element_type=F32)
    a = jnp.square(jnp.maximum(u, 0.0)).astype(BF16)
    acc_ref[...] += jnp.dot(a, wdn_ref[...], preferred_element_type=F32)

    @pl.when(j == pl.num_programs(1) - 1)
    def _():
        o_ref[...] = x_ref[...] + acc_ref[...]


def _mlp_call(x2, g, wup, wdn, *, tm, tf):
    t, d = x2.shape
    ff = wup.shape[1]
    return pl.pallas_call(
        _mlp_body,
        grid=(t // tm, ff // tf),
        in_specs=[pl.BlockSpec((tm, d), lambda i, j: (i, 0)), _full_spec(g),
                  pl.BlockSpec((d, tf), lambda i, j: (0, j)), pl.BlockSpec((tf, d), lambda i, j: (j, 0))],
        out_specs=pl.BlockSpec((tm, d), lambda i, j: (i, 0)),
        out_shape=jax.ShapeDtypeStruct((t, d), F32),
        scratch_shapes=[pltpu.VMEM((tm, d), BF16), pltpu.VMEM((tm, d), F32)],
        compiler_params=_cparams("parallel", "arbitrary"),
        name="mlp",
    )(x2, g, wup, wdn)


def _mix1_body(x_ref, xh_ref, gmix_ref, win_ref, cw_ref, wpool_ref, ps_ref, wout_ref,
               gx_ref, wq_ref, gq_ref, kx_ref, vx_ref, wo_ref, o_ref, *, tm, seq):
    i = pl.program_id(0)
    x = x_ref[...]
    xf = jnp.concatenate([xh_ref[...], x], axis=0)
    hf = _rms(xf, gmix_ref[...]).astype(BF16)
    y = jnp.dot(hf, win_ref[...], preferred_element_type=F32)
    cw = y.shape[1] // 4
    pos0 = (i * tm) % seq
    rid = lax.broadcasted_iota(jnp.int32, (tm + POOL_HALO, 1), 0)
    keep = jnp.where(rid >= POOL_HALO, 1.0, jnp.where(pos0 > 0, 1.0, 0.0))

    gb = y[POOL_HALO:, 0:cw]
    u = y[:, cw:2 * cw] * y[:, 2 * cw:3 * cw] * keep
    z = y[:, 3 * cw:4 * cw] * keep

    taps = cw_ref[...]
    conv = taps[CONV_W - 1:CONV_W, :] * u[POOL_HALO:]
    for back in range(1, CONV_W):
        conv = conv + taps[CONV_W - 1 - back:CONV_W - back, :] * pltpu.roll(u, back, axis=0)[POOL_HALO:]
    yc = gb * conv

    pos = pos0 + lax.broadcasted_iota(jnp.int32, (tm, 1), 0)
    gw = cw // len(POOL_WINDOWS)
    parts = []
    for g, w in enumerate(POOL_WINDOWS):
        zg = z[:, g * gw:(g + 1) * gw]
        s = zg
        sh = 1
        while sh < w:
            s = s + pltpu.roll(s, sh, axis=0)
            sh *= 2
        cnt = jnp.minimum(pos + 1, w).astype(F32)
        pooled = s[POOL_HALO:] / cnt - zg[POOL_HALO:]
        parts.append(jnp.dot(pooled.astype(BF16), wpool_ref[g], preferred_element_type=F32))
    yd = jnp.concatenate(parts, axis=1) * ps_ref[...]

    mix = (jnp.dot(yc.astype(BF16), wout_ref[0:cw, :], preferred_element_type=F32)
           + jnp.dot(yd.astype(BF16), wout_ref[cw:2 * cw, :], preferred_element_type=F32))
    o_ref[...] = _xattn_tail(x + mix, gx_ref, wq_ref, gq_ref, kx_ref, vx_ref, wo_ref)


def _mix1_call(x2, gmix, win, cw, wpool, ps, wout, gx, wq, gq, kx, vx, wo, *, seq, tm):
    t, d = x2.shape
    row = pl.BlockSpec((tm, d), lambda i: (i, 0))
    halo = pl.BlockSpec((POOL_HALO, d), lambda i: (jnp.maximum(i * (tm // POOL_HALO) - 1, 0), 0))
    mem_spec = pl.BlockSpec((1,) + kx.shape[1:], lambda i: ((i * tm) // seq, 0, 0))
    fulls = [gmix, win, cw, wpool, ps, wout, gx, wq, gq]
    return pl.pallas_call(
        functools.partial(_mix1_body, tm=tm, seq=seq),
        grid=(t // tm,),
        in_specs=[row, halo] + [_full_spec(a) for a in fulls] + [mem_spec, mem_spec, _full_spec(wo)],
        out_specs=row,
        out_shape=jax.ShapeDtypeStruct((t, d), F32),
        compiler_params=_cparams("parallel"),
        name="mix1",
    )(x2, x2, gmix, win, cw, wpool, ps, wout, gx, wq, gq, kx, vx, wo)


def _rope_tables(positions):
    inv = ROPE_THETA ** (-jnp.arange(0, ROT_DIM, 2, dtype=F32) / ROT_DIM)
    ang = positions.astype(F32).reshape(-1, 1) * inv
    cos, sin = jnp.cos(ang), jnp.sin(ang)
    rest = HEAD_DIM - ROT_DIM
    ones = jnp.ones((ang.shape[0], rest), F32)
    zeros = jnp.zeros((ang.shape[0], rest), F32)
    zrot = jnp.zeros_like(sin)
    c = jnp.concatenate([cos, cos, ones], axis=1)
    s_lo = jnp.concatenate([-sin, zrot, zeros], axis=1)
    s_hi = jnp.concatenate([zrot, sin, zeros], axis=1)
    dup = lambda a: jnp.concatenate([a, a], axis=1)
    return dup(c), dup(s_lo), dup(s_hi)


def _row(v, reps=1):
    return jnp.tile(v.astype(F32), reps).reshape(1, -1)


def kernel(x, mem, positions, g_mix, g_xattn, g_mem, g_mlp, wq_x, wkv_x, gq_x, gk_x, wo_x, w_up, w_down,
           w_in_e, gq_a, gk_a, lam_q1, lam_k1, lam_q2, lam_k2, g_sub_a, g_kv_b, w_kv_up_b, gq_b, gk_b,
           g_kidx, w_out_e, w_in_o, conv_w, w_pool, pool_scale, w_out_o):
    batch, seq, d = x.shape
    t = batch * seq
    k_sel = min(TOPK_MAX, seq // 4)
    tm = min(256, seq)
    tq_a = tk_a = min(512, seq)
    tq_b, tk_b = min(256, seq), min(512, seq)
    assert seq % tm == 0 and seq % tq_a == 0 and seq % tk_b == 0 and tk_b >= k_sel

    x2 = x.reshape(t, d)
    c, s_lo, s_hi = _rope_tables(positions)
    mem2 = mem.reshape(batch * mem.shape[1], d)
    depth = g_mix.shape[0]

    for layer in range(depth):
        j = layer // 2
        kx, vx = _memkv_call(mem2, _row(g_mem[layer]), wkv_x[layer].astype(BF16), _row(gk_x[layer]))
        kx = kx.reshape(batch, mem.shape[1], -1)
        vx = vx.reshape(batch, mem.shape[1], -1)
        tail = (_row(g_xattn[layer]), wq_x[layer].astype(BF16), _row(gq_x[layer]) * (X_HEAD_DIM ** -0.5),
                kx, vx, wo_x[layer].astype(BF16))
        if layer % 2 == 0:
            lam_init = 0.8 - 0.6 * float(np.exp(-0.3 * layer))
            lam = (jnp.exp(jnp.sum(lam_q1[j].astype(F32) * lam_k1[j].astype(F32)))
                   - jnp.exp(jnp.sum(lam_q2[j].astype(F32) * lam_k2[j].astype(F32))) + lam_init)
            w_in = w_in_e[j]
            n_main = 2304
            n_idx = w_in.shape[1] - n_main
            widx = jnp.pad(w_in[:, n_main:], ((0, 0), (0, 3 * LANES - n_idx)))
            if IDX_PRECISION is None:
                widx = widx.astype(BF16)
            heads8 = 512 // HEAD_DIM
            bd = jnp.kron(jnp.eye(heads8, dtype=F32), jnp.ones((HEAD_DIM, HEAD_DIM), F32)).astype(BF16)
            gkidx = jnp.pad(g_kidx[j].astype(F32), (0, LANES - IDX_DIM)).reshape(1, LANES)
            qa, ka, va, qb, kb, vbt, qi, kik, wi = _in0_call(
                x2, _row(g_mix[layer]), w_in[:, :n_main].astype(BF16), widx, w_kv_up_b[j].astype(BF16),
                c, s_lo, s_hi, bd, _row(gq_a[j], heads8), _row(gk_a[j], heads8), _row(gq_b[j], heads8),
                _row(gk_b[j], heads8), _row(g_kv_b[j]), gkidx, tm=tk_b, idx_precision=IDX_PRECISION)
            out_a = _diff_attn_call(lam.reshape(1).astype(F32), qa, ka, va, _row(g_sub_a[j]),
                                    batch=batch, seq=seq, tq=tq_a, tk=tk_a, out_scale=1.0 - lam_init)
            tri = jnp.tril(jnp.ones((LANES, LANES), F32)).astype(BF16)
            out_b = _dsa_call(qi, wi, qb, kik, kb, vbt, tri, batch=batch, seq=seq, tq=tq_b, tk=tk_b,
                              k_sel=k_sel, idx_precision=IDX_PRECISION)
            w_out = w_out_e[j].astype(BF16)
            x2 = _out0_call(x2, out_a, out_b, w_out[:512], w_out[512:], *tail, seq=seq, tm=tm)
        else:
            taps = jnp.pad(conv_w[j].astype(F32), ((0, SUBLANES - CONV_W), (0, 0)))
            x2 = _mix1_call(x2, _row(g_mix[layer]), w_in_o[j].astype(BF16), taps, w_pool[j].astype(BF16),
                            _row(pool_scale[j]), w_out_o[j].astype(BF16), *tail, seq=seq, tm=tm)
        x2 = _mlp_call(x2, _row(g_mlp[layer]), w_up[layer].astype(BF16), w_down[layer].astype(BF16),
                       tm=min(1024, t), tf=512)
    return x2.reshape(batch, seq, d)
```

```python
import functools

import numpy as np
import jax
import jax.numpy as jnp
from jax import lax
from jax.experimental import pallas as pl
from jax.experimental.pallas import tpu as pltpu

F32 = jnp.float32
BF16 = jnp.bfloat16

HEAD_DIM = 64
ROT_DIM = HEAD_DIM // 4
ROT_HALF = ROT_DIM // 2
ROPE_THETA = 500000.0
IDX_HEADS = 4
IDX_DIM = 64
TOPK_MAX = 256
CONV_W = 3
POOL_WINDOWS = (2, 4, 8, 16)
POOL_HALO = 16
X_HEADS = 4
X_HEAD_DIM = 128
EPS = 1e-6

LANES = 128
SUBLANES = 8
VMEM_LIMIT_BYTES = 56 * 1024 * 1024

KEY_NEG_INF = np.int32(-2139095041)
INT32_MIN = np.int32(-(2 ** 31))

NT_DIMS = (((1,), (1,)), ((), ()))

IDX_PRECISION = None


def _cparams(*sem):
    return pltpu.CompilerParams(dimension_semantics=sem, vmem_limit_bytes=VMEM_LIMIT_BYTES)


def _full_spec(arr):
    nd = arr.ndim
    return pl.BlockSpec(arr.shape, lambda *_: (0,) * nd)


def _rms(x, g):
    ms = jnp.mean(x * x, axis=-1, keepdims=True)
    return x * lax.rsqrt(ms + EPS) * g


def _tile_lanes(t, width):
    reps = width // t.shape[-1]
    return t if reps == 1 else jnp.concatenate([t] * reps, axis=1)


def _rope(y, c, s_lo, s_hi):
    w = y.shape[-1]
    c, s_lo, s_hi = _tile_lanes(c, w), _tile_lanes(s_lo, w), _tile_lanes(s_hi, w)
    upper = pltpu.roll(y, w - ROT_HALF, axis=1)
    lower = pltpu.roll(y, ROT_HALF, axis=1)
    return y * c + upper * s_lo + lower * s_hi


def _headnorm64(y, blockdiag, g):
    ss = jnp.dot((y * y).astype(BF16), blockdiag, preferred_element_type=F32)
    return y * lax.rsqrt(ss * (1.0 / HEAD_DIM) + EPS) * g


def _in0_body(x_ref, gmix_ref, wmain_ref, widx_ref, wkvup_ref, c_ref, slo_ref, shi_ref,
              bd_ref, gqa_ref, gka_ref, gqb_ref, gkb_ref, gkv_ref, gkidx_ref,
              qa_ref, ka_ref, va_ref, qb_ref, kb_ref, vbt_ref, qi_ref, kik_ref, wi_ref,
              *, idx_precision):
    h = _rms(x_ref[...], gmix_ref[...])
    hb = h.astype(BF16)
    y = jnp.dot(hb, wmain_ref[...], preferred_element_type=F32)
    c, s_lo, s_hi = c_ref[...], slo_ref[...], shi_ref[...]
    bd = bd_ref[...]
    scale = HEAD_DIM ** -0.5

    qa = _rope(_headnorm64(y[:, 0:512], bd, gqa_ref[...]), c, s_lo, s_hi) * scale
    qa_ref[...] = qa.astype(BF16)
    ka = _rope(_headnorm64(y[:, 512:1024], bd, gka_ref[...]), c, s_lo, s_hi)
    ka_ref[...] = ka.astype(BF16)
    va_ref[...] = y[:, 1024:1536].astype(BF16)
    qb = _rope(_headnorm64(y[:, 1536:2048], bd, gqb_ref[...]), c, s_lo, s_hi) * scale
    qb_ref[...] = qb.astype(BF16)

    ckv = _rms(y[:, 2048:2304], gkv_ref[...]).astype(BF16)
    kv = jnp.dot(ckv, wkvup_ref[...], preferred_element_type=F32)
    kb = _rope(_headnorm64(kv[:, 0:512], bd, gkb_ref[...]), c, s_lo, s_hi)
    kb_ref[...] = kb.astype(BF16)
    vbt_ref[0] = kv[:, 512:1024].T.astype(BF16)

    if idx_precision is None:
        yi = jnp.dot(hb, widx_ref[...], preferred_element_type=F32)
    else:
        yi = jnp.dot(h, widx_ref[...], preferred_element_type=F32, precision=idx_precision)
    qi_ref[...] = _rope(yi[:, 0:256], c, s_lo, s_hi).astype(qi_ref.dtype)
    blk = yi[:, 256:384]
    lane = lax.broadcasted_iota(jnp.int32, blk.shape, 1)
    ms = jnp.sum(jnp.where(lane < IDX_DIM, blk * blk, 0.0), axis=-1, keepdims=True) * (1.0 / IDX_DIM)
    kin = blk * lax.rsqrt(ms + EPS) * gkidx_ref[...]
    kir = _rope(kin, c, s_lo, s_hi)
    kik = kir + pltpu.roll(kir, IDX_DIM, axis=1)
    kik_ref[...] = kik.astype(kik_ref.dtype)
    wi_ref[...] = blk * (IDX_HEADS ** -0.5 * IDX_DIM ** -0.5)


def _in0_call(x2, gmix, wmain, widx, wkvup, c, s_lo, s_hi, bd, gqa, gka, gqb, gkb, gkv, gkidx,
              *, tm, idx_precision):
    t, d = x2.shape
    row = lambda w: pl.BlockSpec((tm, w), lambda i: (i, 0))
    fulls = [gmix, wmain, widx, wkvup]
    gains = [bd, gqa, gka, gqb, gkb, gkv, gkidx]
    in_specs = ([row(d)] + [_full_spec(a) for a in fulls] + [row(LANES)] * 3
                + [_full_spec(a) for a in gains])
    idx_dt = BF16 if idx_precision is None else F32
    sds = jax.ShapeDtypeStruct
    out_shape = ([sds((t, 512), BF16)] * 5 + [sds((t // tm, 512, tm), BF16), sds((t, 256), idx_dt),
                                              sds((t, LANES), idx_dt), sds((t, LANES), F32)])
    out_specs = ([row(512)] * 5 + [pl.BlockSpec((1, 512, tm), lambda i: (i, 0, 0)), row(256), row(LANES),
                                   row(LANES)])
    return pl.pallas_call(
        functools.partial(_in0_body, idx_precision=idx_precision),
        grid=(t // tm,),
        in_specs=in_specs,
        out_specs=out_specs,
        out_shape=out_shape,
        compiler_params=_cparams("parallel"),
        name="in0",
    )(x2, gmix, wmain, widx, wkvup, c, s_lo, s_hi, bd, gqa, gka, gqb, gkb, gkv, gkidx)


def _diff_attn_body(lam_ref, q_ref, k_ref, v_ref, gsub_ref, o_ref,
                    m0_ref, l0_ref, a0_ref, m1_ref, l1_ref, a1_ref, *, tq, tk, out_scale):
    i, j = pl.program_id(2), pl.program_id(3)

    @pl.when(j == 0)
    def _():
        for m_ref, l_ref, a_ref in ((m0_ref, l0_ref, a0_ref), (m1_ref, l1_ref, a1_ref)):
            m_ref[...] = jnp.full(m_ref.shape, -jnp.inf, F32)
            l_ref[...] = jnp.zeros(l_ref.shape, F32)
            a_ref[...] = jnp.zeros(a_ref.shape, F32)

    @pl.when(j * tk < (i + 1) * tq)
    def _():
        q = q_ref[...]
        k = k_ref[...]
        v = v_ref[...]
        lane = lax.broadcasted_iota(jnp.int32, (1, LANES), 1)
        row = i * tq + lax.broadcasted_iota(jnp.int32, (tq, tk), 0)
        col = j * tk + lax.broadcasted_iota(jnp.int32, (tq, tk), 1)
        causal = col <= row
        maps = ((lane < HEAD_DIM, m0_ref, l0_ref, a0_ref), (lane >= HEAD_DIM, m1_ref, l1_ref, a1_ref))
        for sel, m_ref, l_ref, a_ref in maps:
            qm = q * sel.astype(BF16)
            s = lax.dot_general(qm, k, NT_DIMS, preferred_element_type=F32)
            s = jnp.where(causal, s, -jnp.inf)
            m_old = m_ref[...]
            m_new = jnp.maximum(m_old, jnp.max(s, axis=-1, keepdims=True))
            alpha = jnp.exp(m_old - m_new)
            p = jnp.exp(s - m_new[:, 0:1])
            l_ref[...] = alpha * l_ref[...] + jnp.sum(p, axis=-1, keepdims=True)
            a_ref[...] = alpha * a_ref[...] + jnp.dot(p.astype(BF16), v, preferred_element_type=F32)
            m_ref[...] = m_new

    @pl.when(j == pl.num_programs(3) - 1)
    def _():
        lam = lam_ref[0]
        o = a0_ref[...] / l0_ref[...] - lam * (a1_ref[...] / l1_ref[...])
        o = _rms(o, gsub_ref[...]) * out_scale
        o_ref[...] = o.astype(o_ref.dtype)


def _diff_attn_call(lam, qa, ka, va, gsub, *, batch, seq, tq, tk, out_scale):
    t = qa.shape[0]
    heads = qa.shape[1] // LANES
    nq, nk = seq // tq, seq // tk

    def kv_map(b, h, i, j):
        last = ((i + 1) * tq - 1) // tk
        return (b * nk + jnp.minimum(j, last), h)

    q_spec = pl.BlockSpec((tq, LANES), lambda b, h, i, j: (b * nq + i, h))
    kv_spec = pl.BlockSpec((tk, LANES), kv_map)
    stat = pltpu.VMEM((tq, LANES), F32)
    return pl.pallas_call(
        functools.partial(_diff_attn_body, tq=tq, tk=tk, out_scale=out_scale),
        grid=(batch, heads, nq, nk),
        in_specs=[pl.BlockSpec(memory_space=pltpu.SMEM), q_spec, kv_spec, kv_spec, _full_spec(gsub)],
        out_specs=q_spec,
        out_shape=jax.ShapeDtypeStruct((t, heads * LANES), BF16),
        scratch_shapes=[stat] * 6,
        compiler_params=_cparams("parallel", "parallel", "parallel", "arbitrary"),
        name="diff_attn",
    )(lam, qa, ka, va, gsub)


def _key_to_float(key):
    bits = key ^ ((key >> 31) & jnp.int32(0x7FFFFFFF))
    return lax.bitcast_convert_type(bits, F32)


def _dsa_body(qi_ref, wi_ref, qb_ref, kik_ref, kb_ref, vbt_ref, tri_ref, o_ref, sc_ref, bias_ref, s_ref,
              *, tq, tk, k_sel, idx_precision):
    i = pl.program_id(1)
    q0 = i * tq
    nkc = (q0 + tq + tk - 1) // tk
    qpos = q0 + lax.broadcasted_iota(jnp.int32, (1, tq), 1)
    lane = lax.broadcasted_iota(jnp.int32, (1, LANES), 1)
    lane_lo = lane < HEAD_DIM
    tiles = tk // LANES
    neg_inf = jnp.float32(-jnp.inf)

    wi_t = wi_ref[...].T
    qi = qi_ref[...]
    q_heads = []
    for h in range(IDX_HEADS):
        pair = qi[:, (h // 2) * LANES:(h // 2 + 1) * LANES]
        q_heads.append(pair * (lane_lo if h % 2 == 0 else ~lane_lo).astype(pair.dtype))
    qi_cat = jnp.concatenate(q_heads, axis=0)
    w_heads = [wi_t[IDX_DIM + h:IDX_DIM + h + 1, :] for h in range(IDX_HEADS)]

    def score_chunk(c, carry):
        kk = kik_ref[pl.ds(pl.multiple_of(c * tk, tk), tk), :]
        s = lax.dot_general(kk, qi_cat, NT_DIMS, preferred_element_type=F32, precision=idx_precision)
        acc = jnp.zeros((tk, tq), F32)
        for h in range(IDX_HEADS):
            acc = acc + jnp.maximum(s[:, h * tq:(h + 1) * tq], 0.0) * w_heads[h]
        kpos = c * tk + lax.broadcasted_iota(jnp.int32, (tk, 1), 0)
        sc_ref[c] = jnp.where(kpos <= qpos, acc, neg_inf)
        return carry

    lax.fori_loop(0, nkc, score_chunk, 0)

    acc_rows = 4 * SUBLANES

    def count_ge(cand):
        def body(c, acc):
            hit = jnp.where(sc_ref[c] >= cand, 1.0, 0.0)
            return acc + jnp.sum(hit.reshape(tk // acc_rows, acc_rows, tq), axis=0)
        acc = lax.fori_loop(0, nkc, body, jnp.zeros((acc_rows, tq), F32))
        return jnp.sum(acc, axis=0, keepdims=True)

    ksel = float(k_sel)

    def count_at(cand_key):
        return jnp.where(cand_key <= KEY_NEG_INF, ksel, count_ge(_key_to_float(cand_key)))

    zero_key = jnp.zeros((1, tq), jnp.int32)
    thr = jnp.where(count_at(zero_key) >= ksel, zero_key, INT32_MIN)

    def bit_step(b, thr):
        cand = thr + jnp.left_shift(jnp.int32(1), 30 - b)
        return jnp.where(count_at(cand) >= ksel, cand, thr)

    thr = lax.fori_loop(0, 31, bit_step, thr)
    thr_f = _key_to_float(thr)
    above_f = _key_to_float(thr + 1)
    need = ksel - count_ge(above_f)
    tri = tri_ref[...]

    def select_chunk(c, taken):
        for t in range(tiles):
            st = sc_ref[c, t * LANES:(t + 1) * LANES, :]
            is_above = st >= above_f
            tied = jnp.where(is_above, 0.0, jnp.where(st >= thr_f, 1.0, 0.0))
            prefix = jnp.dot(tri, tied.astype(BF16), preferred_element_type=F32)
            tie_sel = jnp.where((taken + prefix) <= need, tied, 0.0)
            sel = jnp.where(is_above, 0.0, jnp.where(tie_sel > 0.0, 0.0, neg_inf))
            kpos = c * tk + t * LANES + lax.broadcasted_iota(jnp.int32, (LANES, 1), 0)
            bias_ref[c, t * LANES:(t + 1) * LANES, :] = jnp.where(kpos <= qpos, sel, neg_inf)
            taken = taken + prefix[LANES - 1:LANES, :]
        return taken

    lax.fori_loop(0, nkc, select_chunk, jnp.zeros((1, tq), F32))

    group = s_ref.shape[0]
    for p0 in range(0, qb_ref.shape[1] // LANES, group):
        pairs = range(p0, p0 + group)
        q_cat = []
        for p in pairs:
            q_pair = qb_ref[:, p * LANES:(p + 1) * LANES]
            q_cat.append(jnp.concatenate([q_pair * lane_lo.astype(BF16), q_pair * (~lane_lo).astype(BF16)],
                                         axis=0))

        def logit_chunk(c, carry, pairs=pairs, q_cat=q_cat):
            bias = bias_ref[c]
            bias2 = jnp.concatenate([bias, bias], axis=1)
            out = []
            for g, (p, m) in enumerate(zip(pairs, carry)):
                kc = kb_ref[pl.ds(pl.multiple_of(c * tk, tk), tk), p * LANES:(p + 1) * LANES]
                s = lax.dot_general(kc, q_cat[g], NT_DIMS, preferred_element_type=F32) + bias2
                s_ref[g, c] = s
                out.append(jnp.maximum(m, jnp.max(s.reshape(tk // SUBLANES, SUBLANES, 2 * tq), axis=0)))
            return tuple(out)

        m8 = lax.fori_loop(0, nkc, logit_chunk,
                           tuple(jnp.full((SUBLANES, 2 * tq), neg_inf, F32) for _ in pairs))
        shift = [jnp.max(m, axis=0, keepdims=True) for m in m8]

        def pv_chunk(c, carry, pairs=pairs, shift=shift):
            out = []
            for g, (p, (l8, acc)) in enumerate(zip(pairs, carry)):
                pe = jnp.exp(s_ref[g, c] - shift[g])
                vt = vbt_ref[c, p * LANES:(p + 1) * LANES, :]
                acc = acc + jnp.dot(vt, pe.astype(BF16), preferred_element_type=F32)
                out.append((l8 + jnp.sum(pe.reshape(tk // SUBLANES, SUBLANES, 2 * tq), axis=0), acc))
            return tuple(out)

        fin = lax.fori_loop(0, nkc, pv_chunk, tuple((jnp.zeros((SUBLANES, 2 * tq), F32),
                                                     jnp.zeros((LANES, 2 * tq), F32)) for _ in pairs))
        for p, (l8, acc) in zip(pairs, fin):
            o_all = acc / jnp.sum(l8, axis=0, keepdims=True)
            o_t = jnp.concatenate([o_all[0:HEAD_DIM, 0:tq], o_all[HEAD_DIM:LANES, tq:2 * tq]], axis=0)
            o_ref[:, p * LANES:(p + 1) * LANES] = o_t.T.astype(o_ref.dtype)


def _dsa_call(qi, wi, qb, kik, kb, vbt, tri, *, batch, seq, tq, tk, k_sel, idx_precision):
    t = qb.shape[0]
    nq, nk = seq // tq, seq // tk
    row = lambda w: pl.BlockSpec((tq, w), lambda b, i: (b * nq + i, 0))
    seqb = lambda w: pl.BlockSpec((seq, w), lambda b, i: (b, 0))
    vbt_spec = pl.BlockSpec((nk,) + vbt.shape[1:], lambda b, i: (b, 0, 0))
    return pl.pallas_call(
        functools.partial(_dsa_body, tq=tq, tk=tk, k_sel=k_sel, idx_precision=idx_precision),
        grid=(batch, nq),
        in_specs=[row(qi.shape[1]), row(LANES), row(qb.shape[1]), seqb(LANES), seqb(kb.shape[1]),
                  vbt_spec, _full_spec(tri)],
        out_specs=row(qb.shape[1]),
        out_shape=jax.ShapeDtypeStruct((t, qb.shape[1]), BF16),
        scratch_shapes=[pltpu.VMEM((nk, tk, tq), F32), pltpu.VMEM((nk, tk, tq), F32),
                        pltpu.VMEM((2, nk, tk, 2 * tq), F32)],
        compiler_params=_cparams("parallel", "arbitrary"),
        name="dsa",
    )(qi, wi, qb, kik, kb, vbt, tri)


def _memkv_body(mem_ref, g_ref, w_ref, gk_ref, k_ref, v_ref):
    hm = _rms(mem_ref[...], g_ref[...]).astype(BF16)
    kv = jnp.dot(hm, w_ref[...], preferred_element_type=F32)
    half = kv.shape[1] // 2
    for h in range(X_HEADS):
        kh = kv[:, h * X_HEAD_DIM:(h + 1) * X_HEAD_DIM]
        k_ref[:, h * X_HEAD_DIM:(h + 1) * X_HEAD_DIM] = _rms(kh, gk_ref[...]).astype(BF16)
    v_ref[...] = kv[:, half:].astype(BF16)


def _memkv_call(mem2, g, w, gk):
    rows, d = mem2.shape
    half = w.shape[1] // 2
    tm = min(rows, 512)
    row = lambda wd: pl.BlockSpec((tm, wd), lambda i: (i, 0))
    return pl.pallas_call(
        _memkv_body,
        grid=(rows // tm,),
        in_specs=[row(d), _full_spec(g), _full_spec(w), _full_spec(gk)],
        out_specs=[row(half), row(half)],
        out_shape=[jax.ShapeDtypeStruct((rows, half), BF16)] * 2,
        compiler_params=_cparams("parallel"),
        name="memkv",
    )(mem2, g, w, gk)


def _xattn_tail(x1, gx_ref, wq_ref, gq_ref, kx_ref, vx_ref, wo_ref):
    hx = _rms(x1, gx_ref[...]).astype(BF16)
    q = jnp.dot(hx, wq_ref[...], preferred_element_type=F32)
    outs = []
    for h in range(X_HEADS):
        cols = slice(h * X_HEAD_DIM, (h + 1) * X_HEAD_DIM)
        qh = _rms(q[:, cols], gq_ref[...]).astype(BF16)
        s = lax.dot_general(qh, kx_ref[0, :, cols], NT_DIMS, preferred_element_type=F32)
        p = jnp.exp(s - jnp.max(s, axis=-1, keepdims=True))
        o = (jnp.dot(p.astype(BF16), vx_ref[0, :, cols], preferred_element_type=F32)
             / jnp.sum(p, axis=-1, keepdims=True))
        outs.append(o.astype(BF16))
    o = jnp.concatenate(outs, axis=1)
    return x1 + jnp.dot(o, wo_ref[...], preferred_element_type=F32)


def _out0_body(x_ref, oa_ref, ob_ref, woa_ref, wob_ref, gx_ref, wq_ref, gq_ref, kx_ref, vx_ref, wo_ref,
               o_ref):
    mix = (jnp.dot(oa_ref[...], woa_ref[...], preferred_element_type=F32)
           + jnp.dot(ob_ref[...], wob_ref[...], preferred_element_type=F32))
    o_ref[...] = _xattn_tail(x_ref[...] + mix, gx_ref, wq_ref, gq_ref, kx_ref, vx_ref, wo_ref)


def _out0_call(x2, oa, ob, woa, wob, gx, wq, gq, kx, vx, wo, *, seq, tm):
    t, d = x2.shape
    row = lambda w: pl.BlockSpec((tm, w), lambda i: (i, 0))
    mem_spec = pl.BlockSpec((1,) + kx.shape[1:], lambda i: ((i * tm) // seq, 0, 0))
    fulls = [woa, wob, gx, wq, gq]
    return pl.pallas_call(
        _out0_body,
        grid=(t // tm,),
        in_specs=[row(d), row(oa.shape[1]), row(ob.shape[1])] + [_full_spec(a) for a in fulls]
                 + [mem_spec, mem_spec, _full_spec(wo)],
        out_specs=row(d),
        out_shape=jax.ShapeDtypeStruct((t, d), F32),
        compiler_params=_cparams("parallel"),
        name="out0",
    )(x2, oa, ob, woa, wob, gx, wq, gq, kx, vx, wo)


def _mlp_body(x_ref, g_ref, wup_ref, wdn_ref, o_ref, h_ref, acc_ref):
    j = pl.program_id(1)

    @pl.when(j == 0)
    def _():
        h_ref[...] = _rms(x_ref[...], g_ref[...]).astype(BF16)
        acc_ref[...] = jnp.zeros(acc_ref.shape, F32)

    u = jnp.dot(h_ref[...], wup_ref[...], preferred_element_type=F32)
    a = jnp.square(jnp.maximum(u, 0.0)).astype(BF16)
    acc_ref[...] += jnp.dot(a, wdn_ref[...], preferred_element_type=F32)

    @pl.when(j == pl.num_programs(1) - 1)
    def _():
        o_ref[...] = x_ref[...] + acc_ref[...]


def _mlp_call(x2, g, wup, wdn, *, tm, tf):
    t, d = x2.shape
    ff = wup.shape[1]
    return pl.pallas_call(
        _mlp_body,
        grid=(t // tm, ff // tf),
        in_specs=[pl.BlockSpec((tm, d), lambda i, j: (i, 0)), _full_spec(g),
                  pl.BlockSpec((d, tf), lambda i, j: (0, j)), pl.BlockSpec((tf, d), lambda i, j: (j, 0))],
        out_specs=pl.BlockSpec((tm, d), lambda i, j: (i, 0)),
        out_shape=jax.ShapeDtypeStruct((t, d), F32),
        scratch_shapes=[pltpu.VMEM((tm, d), BF16), pltpu.VMEM((tm, d), F32)],
        compiler_params=_cparams("parallel", "arbitrary"),
        name="mlp",
    )(x2, g, wup, wdn)


def _mix1_body(x_ref, xh_ref, gmix_ref, win_ref, cw_ref, wpool_ref, ps_ref, wout_ref,
               gx_ref, wq_ref, gq_ref, kx_ref, vx_ref, wo_ref, o_ref, *, tm, seq):
    i = pl.program_id(0)
    x = x_ref[...]
    xf = jnp.concatenate([xh_ref[...], x], axis=0)
    hf = _rms(xf, gmix_ref[...]).astype(BF16)
    y = jnp.dot(hf, win_ref[...], preferred_element_type=F32)
    cw = y.shape[1] // 4
    pos0 = (i * tm) % seq
    rid = lax.broadcasted_iota(jnp.int32, (tm + POOL_HALO, 1), 0)
    keep = jnp.where(rid >= POOL_HALO, 1.0, jnp.where(pos0 > 0, 1.0, 0.0))

    gb = y[POOL_HALO:, 0:cw]
    u = y[:, cw:2 * cw] * y[:, 2 * cw:3 * cw] * keep
    z = y[:, 3 * cw:4 * cw] * keep

    taps = cw_ref[...]
    conv = taps[CONV_W - 1:CONV_W, :] * u[POOL_HALO:]
    for back in range(1, CONV_W):
        conv = conv + taps[CONV_W - 1 - back:CONV_W - back, :] * pltpu.roll(u, back, axis=0)[POOL_HALO:]
    yc = gb * conv

    pos = pos0 + lax.broadcasted_iota(jnp.int32, (tm, 1), 0)
    gw = cw // len(POOL_WINDOWS)
    parts = []
    for g, w in enumerate(POOL_WINDOWS):
        zg = z[:, g * gw:(g + 1) * gw]
        s = zg
        sh = 1
        while sh < w:
            s = s + pltpu.roll(s, sh, axis=0)
            sh *= 2
        cnt = jnp.minimum(pos + 1, w).astype(F32)
        pooled = s[POOL_HALO:] / cnt - zg[POOL_HALO:]
        parts.append(jnp.dot(pooled.astype(BF16), wpool_ref[g], preferred_element_type=F32))
    yd = jnp.concatenate(parts, axis=1) * ps_ref[...]

    mix = (jnp.dot(yc.astype(BF16), wout_ref[0:cw, :], preferred_element_type=F32)
           + jnp.dot(yd.astype(BF16), wout_ref[cw:2 * cw, :], preferred_element_type=F32))
    o_ref[...] = _xattn_tail(x + mix, gx_ref, wq_ref, gq_ref, kx_ref, vx_ref, wo_ref)


def _mix1_call(x2, gmix, win, cw, wpool, ps, wout, gx, wq, gq, kx, vx, wo, *, seq, tm):
    t, d = x2.shape
    row = pl.BlockSpec((tm, d), lambda i: (i, 0))
    halo = pl.BlockSpec((POOL_HALO, d), lambda i: (jnp.maximum(i * (tm // POOL_HALO) - 1, 0), 0))
    mem_spec = pl.BlockSpec((1,) + kx.shape[1:], lambda i: ((i * tm) // seq, 0, 0))
    fulls = [gmix, win, cw, wpool, ps, wout, gx, wq, gq]
    return pl.pallas_call(
        functools.partial(_mix1_body, tm=tm, seq=seq),
        grid=(t // tm,),
        in_specs=[row, halo] + [_full_spec(a) for a in fulls] + [mem_spec, mem_spec, _full_spec(wo)],
        out_specs=row,
        out_shape=jax.ShapeDtypeStruct((t, d), F32),
        compiler_params=_cparams("parallel"),
        name="mix1",
    )(x2, x2, gmix, win, cw, wpool, ps, wout, gx, wq, gq, kx, vx, wo)


def _rope_tables(positions):
    inv = ROPE_THETA ** (-jnp.arange(0, ROT_DIM, 2, dtype=F32) / ROT_DIM)
    ang = positions.astype(F32).reshape(-1, 1) * inv
    cos, sin = jnp.cos(ang), jnp.sin(ang)
    rest = HEAD_DIM - ROT_DIM
    ones = jnp.ones((ang.shape[0], rest), F32)
    zeros = jnp.zeros((ang.shape[0], rest), F32)
    zrot = jnp.zeros_like(sin)
    c = jnp.concatenate([cos, cos, ones], axis=1)
    s_lo = jnp.concatenate([-sin, zrot, zeros], axis=1)
    s_hi = jnp.concatenate([zrot, sin, zeros], axis=1)
    dup = lambda a: jnp.concatenate([a, a], axis=1)
    return dup(c), dup(s_lo), dup(s_hi)


def _row(v, reps=1):
    return jnp.tile(v.astype(F32), reps).reshape(1, -1)


def kernel(x, mem, positions, g_mix, g_xattn, g_mem, g_mlp, wq_x, wkv_x, gq_x, gk_x, wo_x, w_up, w_down,
           w_in_e, gq_a, gk_a, lam_q1, lam_k1, lam_q2, lam_k2, g_sub_a, g_kv_b, w_kv_up_b, gq_b, gk_b,
           g_kidx, w_out_e, w_in_o, conv_w, w_pool, pool_scale, w_out_o):
    batch, seq, d = x.shape
    t = batch * seq
    k_sel = min(TOPK_MAX, seq // 4)
    tm = min(256, seq)
    tq_a = tk_a = min(512, seq)
    tq_b, tk_b = min(256, seq), min(512, seq)
    assert seq % tm == 0 and seq % tq_a == 0 and seq % tk_b == 0 and tk_b >= k_sel

    x2 = x.reshape(t, d)
    c, s_lo, s_hi = _rope_tables(positions)
    mem2 = mem.reshape(batch * mem.shape[1], d)
    depth = g_mix.shape[0]

    for layer in range(depth):
        j = layer // 2
        kx, vx = _memkv_call(mem2, _row(g_mem[layer]), wkv_x[layer].astype(BF16), _row(gk_x[layer]))
        kx = kx.reshape(batch, mem.shape[1], -1)
        vx = vx.reshape(batch, mem.shape[1], -1)
        tail = (_row(g_xattn[layer]), wq_x[layer].astype(BF16), _row(gq_x[layer]) * (X_HEAD_DIM ** -0.5),
                kx, vx, wo_x[layer].astype(BF16))
        if layer % 2 == 0:
            lam_init = 0.8 - 0.6 * float(np.exp(-0.3 * layer))
            lam = (jnp.exp(jnp.sum(lam_q1[j].astype(F32) * lam_k1[j].astype(F32)))
                   - jnp.exp(jnp.sum(lam_q2[j].astype(F32) * lam_k2[j].astype(F32))) + lam_init)
            w_in = w_in_e[j]
            n_main = 2304
            n_idx = w_in.shape[1] - n_main
            widx = jnp.pad(w_in[:, n_main:], ((0, 0), (0, 3 * LANES - n_idx)))
            if IDX_PRECISION is None:
                widx = widx.astype(BF16)
            heads8 = 512 // HEAD_DIM
            bd = jnp.kron(jnp.eye(heads8, dtype=F32), jnp.ones((HEAD_DIM, HEAD_DIM), F32)).astype(BF16)
            gkidx = jnp.pad(g_kidx[j].astype(F32), (0, LANES - IDX_DIM)).reshape(1, LANES)
            qa, ka, va, qb, kb, vbt, qi, kik, wi = _in0_call(
                x2, _row(g_mix[layer]), w_in[:, :n_main].astype(BF16), widx, w_kv_up_b[j].astype(BF16),
                c, s_lo, s_hi, bd, _row(gq_a[j], heads8), _row(gk_a[j], heads8), _row(gq_b[j], heads8),
                _row(gk_b[j], heads8), _row(g_kv_b[j]), gkidx, tm=tk_b, idx_precision=IDX_PRECISION)
            out_a = _diff_attn_call(lam.reshape(1).astype(F32), qa, ka, va, _row(g_sub_a[j]),
                                    batch=batch, seq=seq, tq=tq_a, tk=tk_a, out_scale=1.0 - lam_init)
            tri = jnp.tril(jnp.ones((LANES, LANES), F32)).astype(BF16)
            out_b = _dsa_call(qi, wi, qb, kik, kb, vbt, tri, batch=batch, seq=seq, tq=tq_b, tk=tk_b,
                              k_sel=k_sel, idx_precision=IDX_PRECISION)
            w_out = w_out_e[j].astype(BF16)
            x2 = _out0_call(x2, out_a, out_b, w_out[:512], w_out[512:], *tail, seq=seq, tm=tm)
        else:
            taps = jnp.pad(conv_w[j].astype(F32), ((0, SUBLANES - CONV_W), (0, 0)))
            x2 = _mix1_call(x2, _row(g_mix[layer]), w_in_o[j].astype(BF16), taps, w_pool[j].astype(BF16),
                            _row(pool_scale[j]), w_out_o[j].astype(BF16), *tail, seq=seq, tm=tm)
        x2 = _mlp_call(x2, _row(g_mlp[layer]), w_up[layer].astype(BF16), w_down[layer].astype(BF16),
                       tm=min(1024, t), tf=512)
    return x2.reshape(batch, seq, d)
```

```python
import functools

import numpy as np
import jax
import jax.numpy as jnp
from jax import lax
from jax.experimental import pallas as pl
from jax.experimental.pallas import tpu as pltpu

F32 = jnp.float32
BF16 = jnp.bfloat16

HEAD_DIM = 64
ROT_DIM = HEAD_DIM // 4
ROT_HALF = ROT_DIM // 2
ROPE_THETA = 500000.0
IDX_HEADS = 4
IDX_DIM = 64
TOPK_MAX = 256
CONV_W = 3
POOL_WINDOWS = (2, 4, 8, 16)
POOL_HALO = 16
X_HEADS = 4
X_HEAD_DIM = 128
EPS = 1e-6

LANES = 128
SUBLANES = 8
VMEM_LIMIT_BYTES = 56 * 1024 * 1024

KEY_NEG_INF = np.int32(-2139095041)
INT32_MIN = np.int32(-(2 ** 31))

NT_DIMS = (((1,), (1,)), ((), ()))

IDX_PRECISION = None


def _cparams(*sem):
    return pltpu.CompilerParams(dimension_semantics=sem, vmem_limit_bytes=VMEM_LIMIT_BYTES)


def _full_spec(arr):
    nd = arr.ndim
    return pl.BlockSpec(arr.shape, lambda *_: (0,) * nd)


def _rms(x, g):
    ms = jnp.mean(x * x, axis=-1, keepdims=True)
    return x * lax.rsqrt(ms + EPS) * g


def _tile_lanes(t, width):
    reps = width // t.shape[-1]
    return t if reps == 1 else jnp.concatenate([t] * reps, axis=1)


def _rope(y, c, s_lo, s_hi):
    w = y.shape[-1]
    c, s_lo, s_hi = _tile_lanes(c, w), _tile_lanes(s_lo, w), _tile_lanes(s_hi, w)
    upper = pltpu.roll(y, w - ROT_HALF, axis=1)
    lower = pltpu.roll(y, ROT_HALF, axis=1)
    return y * c + upper * s_lo + lower * s_hi


def _headnorm64(y, blockdiag, g):
    ss = jnp.dot((y * y).astype(BF16), blockdiag, preferred_element_type=F32)
    return y * lax.rsqrt(ss * (1.0 / HEAD_DIM) + EPS) * g


def _in0_body(x_ref, gmix_ref, wmain_ref, widx_ref, wkvup_ref, c_ref, slo_ref, shi_ref,
              bd_ref, gqa_ref, gka_ref, gqb_ref, gkb_ref, gkv_ref, gkidx_ref,
              qa_ref, ka_ref, vat_ref, qb_ref, kb_ref, vbt_ref, qi_ref, kik_ref, wi_ref,
              *, idx_precision):
    h = _rms(x_ref[...], gmix_ref[...])
    hb = h.astype(BF16)
    y = jnp.dot(hb, wmain_ref[...], preferred_element_type=F32)
    c, s_lo, s_hi = c_ref[...], slo_ref[...], shi_ref[...]
    bd = bd_ref[...]
    scale = HEAD_DIM ** -0.5

    qa = _rope(_headnorm64(y[:, 0:512], bd, gqa_ref[...]), c, s_lo, s_hi) * scale
    qa_ref[...] = qa.astype(BF16)
    ka = _rope(_headnorm64(y[:, 512:1024], bd, gka_ref[...]), c, s_lo, s_hi)
    ka_ref[...] = ka.astype(BF16)
    vat_ref[0] = y[:, 1024:1536].T.astype(BF16)
    qb = _rope(_headnorm64(y[:, 1536:2048], bd, gqb_ref[...]), c, s_lo, s_hi) * scale
    qb_ref[...] = qb.astype(BF16)

    ckv = _rms(y[:, 2048:2304], gkv_ref[...]).astype(BF16)
    kv = jnp.dot(ckv, wkvup_ref[...], preferred_element_type=F32)
    kb = _rope(_headnorm64(kv[:, 0:512], bd, gkb_ref[...]), c, s_lo, s_hi)
    kb_ref[...] = kb.astype(BF16)
    vbt_ref[0] = kv[:, 512:1024].T.astype(BF16)

    if idx_precision is None:
        yi = jnp.dot(hb, widx_ref[...], preferred_element_type=F32)
    else:
        yi = jnp.dot(h, widx_ref[...], preferred_element_type=F32, precision=idx_precision)
    qi_ref[...] = _rope(yi[:, 0:256], c, s_lo, s_hi).astype(qi_ref.dtype)
    blk = yi[:, 256:384]
    lane = lax.broadcasted_iota(jnp.int32, blk.shape, 1)
    ms = jnp.sum(jnp.where(lane < IDX_DIM, blk * blk, 0.0), axis=-1, keepdims=True) * (1.0 / IDX_DIM)
    kin = blk * lax.rsqrt(ms + EPS) * gkidx_ref[...]
    kir = _rope(kin, c, s_lo, s_hi)
    kik = kir + pltpu.roll(kir, IDX_DIM, axis=1)
    kik_ref[...] = kik.astype(kik_ref.dtype)
    wi_ref[...] = blk * (IDX_HEADS ** -0.5 * IDX_DIM ** -0.5)


def _in0_call(x2, gmix, wmain, widx, wkvup, c, s_lo, s_hi, bd, gqa, gka, gqb, gkb, gkv, gkidx,
              *, tm, idx_precision):
    t, d = x2.shape
    row = lambda w: pl.BlockSpec((tm, w), lambda i: (i, 0))
    fulls = [gmix, wmain, widx, wkvup]
    gains = [bd, gqa, gka, gqb, gkb, gkv, gkidx]
    in_specs = ([row(d)] + [_full_spec(a) for a in fulls] + [row(LANES)] * 3
                + [_full_spec(a) for a in gains])
    idx_dt = BF16 if idx_precision is None else F32
    sds = jax.ShapeDtypeStruct
    vt_shape, vt_spec = sds((t // tm, 512, tm), BF16), pl.BlockSpec((1, 512, tm), lambda i: (i, 0, 0))
    out_shape = ([sds((t, 512), BF16)] * 2 + [vt_shape] + [sds((t, 512), BF16)] * 2
                 + [vt_shape, sds((t, 256), idx_dt), sds((t, LANES), idx_dt), sds((t, LANES), F32)])
    out_specs = [row(512)] * 2 + [vt_spec] + [row(512)] * 2 + [vt_spec, row(256), row(LANES), row(LANES)]
    return pl.pallas_call(
        functools.partial(_in0_body, idx_precision=idx_precision),
        grid=(t // tm,),
        in_specs=in_specs,
        out_specs=out_specs,
        out_shape=out_shape,
        compiler_params=_cparams("parallel"),
        name="in0",
    )(x2, gmix, wmain, widx, wkvup, c, s_lo, s_hi, bd, gqa, gka, gqb, gkb, gkv, gkidx)


def _diff_attn_body(lam_ref, q_ref, k_ref, vt_ref, gsub_ref, o_ref, s_ref, *, tq, tk, out_scale):
    i = pl.program_id(2)
    q0 = i * tq
    n_full = (q0 + 1) // tk
    nkc = (q0 + tq + tk - 1) // tk
    lane = lax.broadcasted_iota(jnp.int32, (1, LANES), 1)
    lane_lo = lane < HEAD_DIM
    qpos = q0 + lax.broadcasted_iota(jnp.int32, (1, tq), 1)
    qpos2 = jnp.concatenate([qpos, qpos], axis=1)
    neg_inf = jnp.float32(-jnp.inf)
    heads = range(q_ref.shape[1] // LANES)
    q_cat = []
    for g in heads:
        q_h = q_ref[:, g * LANES:(g + 1) * LANES]
        q_cat.append(jnp.concatenate([q_h * lane_lo.astype(BF16), q_h * (~lane_lo).astype(BF16)], axis=0))

    def logit_chunk(masked):
        def body(c, carry):
            rows = pl.ds(pl.multiple_of(c * tk, tk), tk)
            out = []
            for g, m in zip(heads, carry):
                s = lax.dot_general(k_ref[rows, g * LANES:(g + 1) * LANES], q_cat[g], NT_DIMS,
                                    preferred_element_type=F32)
                if masked:
                    kpos = c * tk + lax.broadcasted_iota(jnp.int32, (tk, 1), 0)
                    s = jnp.where(kpos <= qpos2, s, neg_inf)
                s_ref[g, c] = s
                out.append(jnp.maximum(m, jnp.max(s.reshape(tk // SUBLANES, SUBLANES, 2 * tq), axis=0)))
            return tuple(out)
        return body

    m8 = tuple(jnp.full((SUBLANES, 2 * tq), neg_inf, F32) for _ in heads)
    m8 = lax.fori_loop(0, n_full, logit_chunk(False), m8)
    m8 = lax.fori_loop(n_full, nkc, logit_chunk(True), m8)
    shift = [jnp.max(m, axis=0, keepdims=True) for m in m8]

    def pv_chunk(c, carry):
        out = []
        for g, (l8, acc) in zip(heads, carry):
            pe = jnp.exp(s_ref[g, c] - shift[g])
            acc = acc + jnp.dot(vt_ref[c, g * LANES:(g + 1) * LANES, :], pe.astype(BF16),
                                preferred_element_type=F32)
            out.append((l8 + jnp.sum(pe.reshape(tk // SUBLANES, SUBLANES, 2 * tq), axis=0), acc))
        return tuple(out)

    fin = lax.fori_loop(0, nkc, pv_chunk, tuple((jnp.zeros((SUBLANES, 2 * tq), F32),
                                                 jnp.zeros((LANES, 2 * tq), F32)) for _ in heads))
    lam = lam_ref[0]
    for g, (l8, acc) in zip(heads, fin):
        o_all = acc / jnp.sum(l8, axis=0, keepdims=True)
        o = o_all[:, 0:tq] - lam * o_all[:, tq:2 * tq]
        o = o * lax.rsqrt(jnp.mean(o * o, axis=0, keepdims=True) + EPS)
        o_ref[:, g * LANES:(g + 1) * LANES] = (o.T * gsub_ref[...] * out_scale).astype(o_ref.dtype)


def _diff_attn_call(lam, qa, ka, vat, gsub, *, batch, seq, tq, tk, group, out_scale):
    t = qa.shape[0]
    w = group * LANES
    nq, nk = seq // tq, seq // tk
    q_spec = pl.BlockSpec((tq, w), lambda b, h, i: (b * nq + i, h))
    return pl.pallas_call(
        functools.partial(_diff_attn_body, tq=tq, tk=tk, out_scale=out_scale),
        grid=(batch, qa.shape[1] // w, nq),
        in_specs=[pl.BlockSpec(memory_space=pltpu.SMEM), q_spec,
                  pl.BlockSpec((seq, w), lambda b, h, i: (b, h)),
                  pl.BlockSpec((nk, w, tk), lambda b, h, i: (b, h, 0)), _full_spec(gsub)],
        out_specs=q_spec,
        out_shape=jax.ShapeDtypeStruct((t, qa.shape[1]), BF16),
        scratch_shapes=[pltpu.VMEM((group, nk, tk, 2 * tq), F32)],
        compiler_params=_cparams("parallel", "parallel", "arbitrary"),
        name="diff_attn",
    )(lam, qa, ka, vat, gsub)


def _key_to_float(key):
    bits = key ^ ((key >> 31) & jnp.int32(0x7FFFFFFF))
    return lax.bitcast_convert_type(bits, F32)


def _dsa_body(qi_ref, wi_ref, qb_ref, kik_ref, kb_ref, vbt_ref, tri_ref, o_ref, sc_ref, bias_ref, s_ref,
              *, tq, tk, k_sel, idx_precision):
    i = pl.program_id(1)
    q0 = i * tq
    nkc = (q0 + tq + tk - 1) // tk
    qpos = q0 + lax.broadcasted_iota(jnp.int32, (1, tq), 1)
    lane = lax.broadcasted_iota(jnp.int32, (1, LANES), 1)
    lane_lo = lane < HEAD_DIM
    tiles = tk // LANES
    neg_inf = jnp.float32(-jnp.inf)

    wi_t = wi_ref[...].T
    qi = qi_ref[...]
    q_heads = []
    for h in range(IDX_HEADS):
        pair = qi[:, (h // 2) * LANES:(h // 2 + 1) * LANES]
        q_heads.append(pair * (lane_lo if h % 2 == 0 else ~lane_lo).astype(pair.dtype))
    qi_cat = jnp.concatenate(q_heads, axis=0)
    w_heads = [wi_t[IDX_DIM + h:IDX_DIM + h + 1, :] for h in range(IDX_HEADS)]

    def score_chunk(c, carry):
        kk = kik_ref[pl.ds(pl.multiple_of(c * tk, tk), tk), :]
        s = lax.dot_general(kk, qi_cat, NT_DIMS, preferred_element_type=F32, precision=idx_precision)
        acc = jnp.zeros((tk, tq), F32)
        for h in range(IDX_HEADS):
            acc = acc + jnp.maximum(s[:, h * tq:(h + 1) * tq], 0.0) * w_heads[h]
        kpos = c * tk + lax.broadcasted_iota(jnp.int32, (tk, 1), 0)
        sc_ref[c] = jnp.where(kpos <= qpos, acc, neg_inf)
        return carry

    lax.fori_loop(0, nkc, score_chunk, 0)

    acc_rows = 4 * SUBLANES

    def count_ge(cand):
        def body(c, acc):
            hit = jnp.where(sc_ref[c] >= cand, 1.0, 0.0)
            return acc + jnp.sum(hit.reshape(tk // acc_rows, acc_rows, tq), axis=0)
        acc = lax.fori_loop(0, nkc, body, jnp.zeros((acc_rows, tq), F32))
        return jnp.sum(acc, axis=0, keepdims=True)

    ksel = float(k_sel)

    def count_at(cand_key):
        return jnp.where(cand_key <= KEY_NEG_INF, ksel, count_ge(_key_to_float(cand_key)))

    zero_key = jnp.zeros((1, tq), jnp.int32)
    thr = jnp.where(count_at(zero_key) >= ksel, zero_key, INT32_MIN)

    def bit_step(b, thr):
        cand = thr + jnp.left_shift(jnp.int32(1), 30 - b)
        return jnp.where(count_at(cand) >= ksel, cand, thr)

    thr = lax.fori_loop(0, 31, bit_step, thr)
    thr_f = _key_to_float(thr)
    above_f = _key_to_float(thr + 1)
    need = ksel - count_ge(above_f)
    tri = tri_ref[...]

    def select_chunk(c, taken):
        for t in range(tiles):
            st = sc_ref[c, t * LANES:(t + 1) * LANES, :]
            is_above = st >= above_f
            tied = jnp.where(is_above, 0.0, jnp.where(st >= thr_f, 1.0, 0.0))
            prefix = jnp.dot(tri, tied.astype(BF16), preferred_element_type=F32)
            tie_sel = jnp.where((taken + prefix) <= need, tied, 0.0)
            sel = jnp.where(is_above, 0.0, jnp.where(tie_sel > 0.0, 0.0, neg_inf))
            kpos = c * tk + t * LANES + lax.broadcasted_iota(jnp.int32, (LANES, 1), 0)
            bias_ref[c, t * LANES:(t + 1) * LANES, :] = jnp.where(kpos <= qpos, sel, neg_inf)
            taken = taken + prefix[LANES - 1:LANES, :]
        return taken

    lax.fori_loop(0, nkc, select_chunk, jnp.zeros((1, tq), F32))

    group = s_ref.shape[0]
    for p0 in range(0, qb_ref.shape[1] // LANES, group):
        pairs = range(p0, p0 + group)
        q_cat = []
        for p in pairs:
            q_pair = qb_ref[:, p * LANES:(p + 1) * LANES]
            q_cat.append(jnp.concatenate([q_pair * lane_lo.astype(BF16), q_pair * (~lane_lo).astype(BF16)],
                                         axis=0))

        def logit_chunk(c, carry, pairs=pairs, q_cat=q_cat):
            bias = bias_ref[c]
            bias2 = jnp.concatenate([bias, bias], axis=1)
            out = []
            for g, (p, m) in enumerate(zip(pairs, carry)):
                kc = kb_ref[pl.ds(pl.multiple_of(c * tk, tk), tk), p * LANES:(p + 1) * LANES]
                s = lax.dot_general(kc, q_cat[g], NT_DIMS, preferred_element_type=F32) + bias2
                s_ref[g, c] = s
                out.append(jnp.maximum(m, jnp.max(s.reshape(tk // SUBLANES, SUBLANES, 2 * tq), axis=0)))
            return tuple(out)

        m8 = lax.fori_loop(0, nkc, logit_chunk,
                           tuple(jnp.full((SUBLANES, 2 * tq), neg_inf, F32) for _ in pairs))
        shift = [jnp.max(m, axis=0, keepdims=True) for m in m8]

        def pv_chunk(c, carry, pairs=pairs, shift=shift):
            out = []
            for g, (p, (l8, acc)) in enumerate(zip(pairs, carry)):
                pe = jnp.exp(s_ref[g, c] - shift[g])
                vt = vbt_ref[c, p * LANES:(p + 1) * LANES, :]
                acc = acc + jnp.dot(vt, pe.astype(BF16), preferred_element_type=F32)
                out.append((l8 + jnp.sum(pe.reshape(tk // SUBLANES, SUBLANES, 2 * tq), axis=0), acc))
            return tuple(out)

        fin = lax.fori_loop(0, nkc, pv_chunk, tuple((jnp.zeros((SUBLANES, 2 * tq), F32),
                                                     jnp.zeros((LANES, 2 * tq), F32)) for _ in pairs))
        for p, (l8, acc) in zip(pairs, fin):
            o_all = acc / jnp.sum(l8, axis=0, keepdims=True)
            o_t = jnp.concatenate([o_all[0:HEAD_DIM, 0:tq], o_all[HEAD_DIM:LANES, tq:2 * tq]], axis=0)
            o_ref[:, p * LANES:(p + 1) * LANES] = o_t.T.astype(o_ref.dtype)


def _dsa_call(qi, wi, qb, kik, kb, vbt, tri, *, batch, seq, tq, tk, k_sel, idx_precision):
    t = qb.shape[0]
    nq, nk = seq // tq, seq // tk
    row = lambda w: pl.BlockSpec((tq, w), lambda b, i: (b * nq + i, 0))
    seqb = lambda w: pl.BlockSpec((seq, w), lambda b, i: (b, 0))
    vbt_spec = pl.BlockSpec((nk,) + vbt.shape[1:], lambda b, i: (b, 0, 0))
    return pl.pallas_call(
        functools.partial(_dsa_body, tq=tq, tk=tk, k_sel=k_sel, idx_precision=idx_precision),
        grid=(batch, nq),
        in_specs=[row(qi.shape[1]), row(LANES), row(qb.shape[1]), seqb(LANES), seqb(kb.shape[1]),
                  vbt_spec, _full_spec(tri)],
        out_specs=row(qb.shape[1]),
        out_shape=jax.ShapeDtypeStruct((t, qb.shape[1]), BF16),
        scratch_shapes=[pltpu.VMEM((nk, tk, tq), F32), pltpu.VMEM((nk, tk, tq), F32),
                        pltpu.VMEM((2, nk, tk, 2 * tq), F32)],
        compiler_params=_cparams("parallel", "arbitrary"),
        name="dsa",
    )(qi, wi, qb, kik, kb, vbt, tri)


def _memkv_body(mem_ref, g_ref, w_ref, gk_ref, k_ref, v_ref):
    hm = _rms(mem_ref[...], g_ref[...]).astype(BF16)
    kv = jnp.dot(hm, w_ref[...], preferred_element_type=F32)
    half = kv.shape[1] // 2
    for h in range(X_HEADS):
        kh = kv[:, h * X_HEAD_DIM:(h + 1) * X_HEAD_DIM]
        k_ref[:, h * X_HEAD_DIM:(h + 1) * X_HEAD_DIM] = _rms(kh, gk_ref[...]).astype(BF16)
    v_ref[...] = kv[:, half:].astype(BF16)


def _memkv_call(mem2, g, w, gk):
    rows, d = mem2.shape
    half = w.shape[1] // 2
    tm = min(rows, 512)
    row = lambda wd: pl.BlockSpec((tm, wd), lambda i: (i, 0))
    return pl.pallas_call(
        _memkv_body,
        grid=(rows // tm,),
        in_specs=[row(d), _full_spec(g), _full_spec(w), _full_spec(gk)],
        out_specs=[row(half), row(half)],
        out_shape=[jax.ShapeDtypeStruct((rows, half), BF16)] * 2,
        compiler_params=_cparams("parallel"),
        name="memkv",
    )(mem2, g, w, gk)


def _xattn_tail(x1, gx_ref, wq_ref, gq_ref, kx_ref, vx_ref, wo_ref):
    hx = _rms(x1, gx_ref[...]).astype(BF16)
    q = jnp.dot(hx, wq_ref[...], preferred_element_type=F32)
    outs = []
    for h in range(X_HEADS):
        cols = slice(h * X_HEAD_DIM, (h + 1) * X_HEAD_DIM)
        qh = _rms(q[:, cols], gq_ref[...]).astype(BF16)
        s = lax.dot_general(qh, kx_ref[0, :, cols], NT_DIMS, preferred_element_type=F32)
        p = jnp.exp(s - jnp.max(s, axis=-1, keepdims=True))
        o = (jnp.dot(p.astype(BF16), vx_ref[0, :, cols], preferred_element_type=F32)
             / jnp.sum(p, axis=-1, keepdims=True))
        outs.append(o.astype(BF16))
    o = jnp.concatenate(outs, axis=1)
    return x1 + jnp.dot(o, wo_ref[...], preferred_element_type=F32)


def _out0_body(x_ref, oa_ref, ob_ref, woa_ref, wob_ref, gx_ref, wq_ref, gq_ref, kx_ref, vx_ref, wo_ref,
               o_ref):
    mix = (jnp.dot(oa_ref[...], woa_ref[...], preferred_element_type=F32)
           + jnp.dot(ob_ref[...], wob_ref[...], preferred_element_type=F32))
    o_ref[...] = _xattn_tail(x_ref[...] + mix, gx_ref, wq_ref, gq_ref, kx_ref, vx_ref, wo_ref)


def _out0_call(x2, oa, ob, woa, wob, gx, wq, gq, kx, vx, wo, *, seq, tm):
    t, d = x2.shape
    row = lambda w: pl.BlockSpec((tm, w), lambda i: (i, 0))
    mem_spec = pl.BlockSpec((1,) + kx.shape[1:], lambda i: ((i * tm) // seq, 0, 0))
    fulls = [woa, wob, gx, wq, gq]
    return pl.pallas_call(
        _out0_body,
        grid=(t // tm,),
        in_specs=[row(d), row(oa.shape[1]), row(ob.shape[1])] + [_full_spec(a) for a in fulls]
                 + [mem_spec, mem_spec, _full_spec(wo)],
        out_specs=row(d),
        out_shape=jax.ShapeDtypeStruct((t, d), F32),
        compiler_params=_cparams("parallel"),
        name="out0",
    )(x2, oa, ob, woa, wob, gx, wq, gq, kx, vx, wo)


def _mlp_body(x_ref, g_ref, wup_ref, wdn_ref, o_ref, h_ref, acc_ref):
    j = pl.program_id(1)

    @pl.when(j == 0)
    def _():
        h_ref[...] = _rms(x_ref[...], g_ref[...]).astype(BF16)
        acc_ref[...] = jnp.zeros(acc_ref.shape, F32)

    u = jnp.dot(h_ref[...], wup_ref[...], preferred_element_type=F32)
    a = jnp.square(jnp.maximum(u, 0.0)).astype(BF16)
    acc_ref[...] += jnp.dot(a, wdn_ref[...], preferred_element_type=F32)

    @pl.when(j == pl.num_programs(1) - 1)
    def _():
        o_ref[...] = x_ref[...] + acc_ref[...]


def _mlp_call(x2, g, wup, wdn, *, tm, tf):
    t, d = x2.shape
    ff = wup.shape[1]
    return pl.pallas_call(
        _mlp_body,
        grid=(t // tm, ff // tf),
        in_specs=[pl.BlockSpec((tm, d), lambda i, j: (i, 0)), _full_spec(g),
                  pl.BlockSpec((d, tf), lambda i, j: (0, j)), pl.BlockSpec((tf, d), lambda i, j: (j, 0))],
        out_specs=pl.BlockSpec((tm, d), lambda i, j: (i, 0)),
        out_shape=jax.ShapeDtypeStruct((t, d), F32),
        scratch_shapes=[pltpu.VMEM((tm, d), BF16), pltpu.VMEM((tm, d), F32)],
        compiler_params=_cparams("parallel", "arbitrary"),
        name="mlp",
    )(x2, g, wup, wdn)


def _mix1_body(x_ref, xh_ref, gmix_ref, win_ref, cw_ref, wpool_ref, ps_ref, wout_ref,
               gx_ref, wq_ref, gq_ref, kx_ref, vx_ref, wo_ref, o_ref, *, tm, seq):
    i = pl.program_id(0)
    x = x_ref[...]
    xf = jnp.concatenate([xh_ref[...], x], axis=0)
    hf = _rms(xf, gmix_ref[...]).astype(BF16)
    y = jnp.dot(hf, win_ref[...], preferred_element_type=F32)
    cw = y.shape[1] // 4
    pos0 = (i * tm) % seq
    rid = lax.broadcasted_iota(jnp.int32, (tm + POOL_HALO, 1), 0)
    keep = jnp.where(rid >= POOL_HALO, 1.0, jnp.where(pos0 > 0, 1.0, 0.0))

    gb = y[POOL_HALO:, 0:cw]
    u = y[:, cw:2 * cw] * y[:, 2 * cw:3 * cw] * keep
    z = y[:, 3 * cw:4 * cw] * keep

    taps = cw_ref[...]
    conv = taps[CONV_W - 1:CONV_W, :] * u[POOL_HALO:]
    for back in range(1, CONV_W):
        conv = conv + taps[CONV_W - 1 - back:CONV_W - back, :] * pltpu.roll(u, back, axis=0)[POOL_HALO:]
    yc = gb * conv

    pos = pos0 + lax.broadcasted_iota(jnp.int32, (tm, 1), 0)
    gw = cw // len(POOL_WINDOWS)
    parts = []
    for g, w in enumerate(POOL_WINDOWS):
        zg = z[:, g * gw:(g + 1) * gw]
        s = zg
        sh = 1
        while sh < w:
            s = s + pltpu.roll(s, sh, axis=0)
            sh *= 2
        cnt = jnp.minimum(pos + 1, w).astype(F32)
        pooled = s[POOL_HALO:] / cnt - zg[POOL_HALO:]
        parts.append(jnp.dot(pooled.astype(BF16), wpool_ref[g], preferred_element_type=F32))
    yd = jnp.concatenate(parts, axis=1) * ps_ref[...]

    mix = (jnp.dot(yc.astype(BF16), wout_ref[0:cw, :], preferred_element_type=F32)
           + jnp.dot(yd.astype(BF16), wout_ref[cw:2 * cw, :], preferred_element_type=F32))
    o_ref[...] = _xattn_tail(x + mix, gx_ref, wq_ref, gq_ref, kx_ref, vx_ref, wo_ref)


def _mix1_call(x2, gmix, win, cw, wpool, ps, wout, gx, wq, gq, kx, vx, wo, *, seq, tm):
    t, d = x2.shape
    row = pl.BlockSpec((tm, d), lambda i: (i, 0))
    halo = pl.BlockSpec((POOL_HALO, d), lambda i: (jnp.maximum(i * (tm // POOL_HALO) - 1, 0), 0))
    mem_spec = pl.BlockSpec((1,) + kx.shape[1:], lambda i: ((i * tm) // seq, 0, 0))
    fulls = [gmix, win, cw, wpool, ps, wout, gx, wq, gq]
    return pl.pallas_call(
        functools.partial(_mix1_body, tm=tm, seq=seq),
        grid=(t // tm,),
        in_specs=[row, halo] + [_full_spec(a) for a in fulls] + [mem_spec, mem_spec, _full_spec(wo)],
        out_specs=row,
        out_shape=jax.ShapeDtypeStruct((t, d), F32),
        compiler_params=_cparams("parallel"),
        name="mix1",
    )(x2, x2, gmix, win, cw, wpool, ps, wout, gx, wq, gq, kx, vx, wo)


def _rope_tables(positions):
    inv = ROPE_THETA ** (-jnp.arange(0, ROT_DIM, 2, dtype=F32) / ROT_DIM)
    ang = positions.astype(F32).reshape(-1, 1) * inv
    cos, sin = jnp.cos(ang), jnp.sin(ang)
    rest = HEAD_DIM - ROT_DIM
    ones = jnp.ones((ang.shape[0], rest), F32)
    zeros = jnp.zeros((ang.shape[0], rest), F32)
    zrot = jnp.zeros_like(sin)
    c = jnp.concatenate([cos, cos, ones], axis=1)
    s_lo = jnp.concatenate([-sin, zrot, zeros], axis=1)
    s_hi = jnp.concatenate([zrot, sin, zeros], axis=1)
    dup = lambda a: jnp.concatenate([a, a], axis=1)
    return dup(c), dup(s_lo), dup(s_hi)


def _row(v, reps=1):
    return jnp.tile(v.astype(F32), reps).reshape(1, -1)


def kernel(x, mem, positions, g_mix, g_xattn, g_mem, g_mlp, wq_x, wkv_x, gq_x, gk_x, wo_x, w_up, w_down,
           w_in_e, gq_a, gk_a, lam_q1, lam_k1, lam_q2, lam_k2, g_sub_a, g_kv_b, w_kv_up_b, gq_b, gk_b,
           g_kidx, w_out_e, w_in_o, conv_w, w_pool, pool_scale, w_out_o):
    batch, seq, d = x.shape
    t = batch * seq
    k_sel = min(TOPK_MAX, seq // 4)
    tm = min(256, seq)
    tq_b, tk_b = min(256, seq), min(512, seq)
    assert seq % tm == 0 and seq % tq_b == 0 and seq % tk_b == 0 and tk_b >= k_sel

    x2 = x.reshape(t, d)
    c, s_lo, s_hi = _rope_tables(positions)
    mem2 = mem.reshape(batch * mem.shape[1], d)
    depth = g_mix.shape[0]

    for layer in range(depth):
        j = layer // 2
        kx, vx = _memkv_call(mem2, _row(g_mem[layer]), wkv_x[layer].astype(BF16), _row(gk_x[layer]))
        kx = kx.reshape(batch, mem.shape[1], -1)
        vx = vx.reshape(batch, mem.shape[1], -1)
        tail = (_row(g_xattn[layer]), wq_x[layer].astype(BF16), _row(gq_x[layer]) * (X_HEAD_DIM ** -0.5),
                kx, vx, wo_x[layer].astype(BF16))
        if layer % 2 == 0:
            lam_init = 0.8 - 0.6 * float(np.exp(-0.3 * layer))
            lam = (jnp.exp(jnp.sum(lam_q1[j].astype(F32) * lam_k1[j].astype(F32)))
                   - jnp.exp(jnp.sum(lam_q2[j].astype(F32) * lam_k2[j].astype(F32))) + lam_init)
            w_in = w_in_e[j]
            n_main = 2304
            n_idx = w_in.shape[1] - n_main
            widx = jnp.pad(w_in[:, n_main:], ((0, 0), (0, 3 * LANES - n_idx)))
            if IDX_PRECISION is None:
                widx = widx.astype(BF16)
            heads8 = 512 // HEAD_DIM
            bd = jnp.kron(jnp.eye(heads8, dtype=F32), jnp.ones((HEAD_DIM, HEAD_DIM), F32)).astype(BF16)
            gkidx = jnp.pad(g_kidx[j].astype(F32), (0, LANES - IDX_DIM)).reshape(1, LANES)
            qa, ka, vat, qb, kb, vbt, qi, kik, wi = _in0_call(
                x2, _row(g_mix[layer]), w_in[:, :n_main].astype(BF16), widx, w_kv_up_b[j].astype(BF16),
                c, s_lo, s_hi, bd, _row(gq_a[j], heads8), _row(gk_a[j], heads8), _row(gq_b[j], heads8),
                _row(gk_b[j], heads8), _row(g_kv_b[j]), gkidx, tm=tk_b, idx_precision=IDX_PRECISION)
            out_a = _diff_attn_call(lam.reshape(1).astype(F32), qa, ka, vat, _row(g_sub_a[j]),
                                    batch=batch, seq=seq, tq=tq_b, tk=tk_b, group=2, out_scale=1.0 - lam_init)
            tri = jnp.tril(jnp.ones((LANES, LANES), F32)).astype(BF16)
            out_b = _dsa_call(qi, wi, qb, kik, kb, vbt, tri, batch=batch, seq=seq, tq=tq_b, tk=tk_b,
                              k_sel=k_sel, idx_precision=IDX_PRECISION)
            w_out = w_out_e[j].astype(BF16)
            x2 = _out0_call(x2, out_a, out_b, w_out[:512], w_out[512:], *tail, seq=seq, tm=tm)
        else:
            taps = jnp.pad(conv_w[j].astype(F32), ((0, SUBLANES - CONV_W), (0, 0)))
            x2 = _mix1_call(x2, _row(g_mix[layer]), w_in_o[j].astype(BF16), taps, w_pool[j].astype(BF16),
                            _row(pool_scale[j]), w_out_o[j].astype(BF16), *tail, seq=seq, tm=tm)
        x2 = _mlp_call(x2, _row(g_mlp[layer]), w_up[layer].astype(BF16), w_down[layer].astype(BF16),
                       tm=min(1024, t), tf=512)
    return x2.reshape(batch, seq, d)
```

```python
import functools

import numpy as np
import jax
import jax.numpy as jnp
from jax import lax
from jax.experimental import pallas as pl
from jax.experimental.pallas import tpu as pltpu

F32 = jnp.float32
BF16 = jnp.bfloat16

HEAD_DIM = 64
ROT_DIM = HEAD_DIM // 4
ROT_HALF = ROT_DIM // 2
ROPE_THETA = 500000.0
IDX_HEADS = 4
IDX_DIM = 64
TOPK_MAX = 256
CONV_W = 3
POOL_WINDOWS = (2, 4, 8, 16)
POOL_HALO = 16
X_HEADS = 4
X_HEAD_DIM = 128
EPS = 1e-6
LOG2_E = 1.4426950408889634

LANES = 128
SUBLANES = 8
VMEM_LIMIT_BYTES = 56 * 1024 * 1024

KEY_NEG_INF = np.int32(-2139095041)
INT32_MIN = np.int32(-(2 ** 31))

NT_DIMS = (((1,), (1,)), ((), ()))

IDX_PRECISION = None


def _cparams(*sem):
    return pltpu.CompilerParams(dimension_semantics=sem, vmem_limit_bytes=VMEM_LIMIT_BYTES)


def _full_spec(arr):
    nd = arr.ndim
    return pl.BlockSpec(arr.shape, lambda *_: (0,) * nd)


def _rms(x, g):
    ms = jnp.mean(x * x, axis=-1, keepdims=True)
    return x * lax.rsqrt(ms + EPS) * g


def _tile_lanes(t, width):
    reps = width // t.shape[-1]
    return t if reps == 1 else jnp.concatenate([t] * reps, axis=1)


def _rope_tables(pos_row, inv_col):
    ang = inv_col * pos_row
    cos, sin = jnp.cos(ang), jnp.sin(ang)
    rest = (HEAD_DIM - ROT_DIM, ang.shape[1])
    head_c = jnp.concatenate([cos, cos, jnp.ones(rest, F32)], axis=0)
    head_s = jnp.concatenate([-sin, sin, jnp.zeros(rest, F32)], axis=0)
    return (jnp.concatenate([head_c, head_c], axis=0).T, jnp.concatenate([head_s, head_s], axis=0).T)


def _rope(y, c, s):
    w = y.shape[-1]
    c, s = _tile_lanes(c, w), _tile_lanes(s, w)
    lane = lax.broadcasted_iota(jnp.int32, (1, w), 1)
    upper = pltpu.roll(y, w - ROT_HALF, axis=1)
    lower = pltpu.roll(y, ROT_HALF, axis=1)
    partner = jnp.where((lane & (HEAD_DIM - 1)) < ROT_HALF, upper, lower)
    return y * c + partner * s


def _headnorm64(y, blockdiag, g):
    ss = jnp.dot((y * y).astype(BF16), blockdiag, preferred_element_type=F32)
    return y * lax.rsqrt(ss * (1.0 / HEAD_DIM) + EPS) * g


def _in0_body(x_ref, gmix_ref, wmain_ref, widx_ref, wkvup_ref, pos_ref, inv_ref,
              bd_ref, gqa_ref, gka_ref, gqb_ref, gkb_ref, gkv_ref, gkidx_ref,
              qa_ref, ka_ref, vat_ref, qb_ref, kb_ref, vbt_ref, qi_ref, kik_ref, wi_ref,
              *, idx_precision):
    h = _rms(x_ref[...], gmix_ref[...])
    hb = h.astype(BF16)
    y = jnp.dot(hb, wmain_ref[...], preferred_element_type=F32)
    rot = _rope_tables(pos_ref[0].astype(F32), inv_ref[...])
    bd = bd_ref[...]
    scale = HEAD_DIM ** -0.5 * LOG2_E

    qa = _rope(_headnorm64(y[:, 0:512], bd, gqa_ref[...]), *rot) * scale
    qa_ref[...] = qa.astype(BF16)
    ka = _rope(_headnorm64(y[:, 512:1024], bd, gka_ref[...]), *rot)
    ka_ref[...] = ka.astype(BF16)
    vat_ref[0] = y[:, 1024:1536].T.astype(BF16)
    qb = _rope(_headnorm64(y[:, 1536:2048], bd, gqb_ref[...]), *rot) * scale
    qb_ref[...] = qb.astype(BF16)

    ckv = _rms(y[:, 2048:2304], gkv_ref[...]).astype(BF16)
    kv = jnp.dot(ckv, wkvup_ref[...], preferred_element_type=F32)
    kb = _rope(_headnorm64(kv[:, 0:512], bd, gkb_ref[...]), *rot)
    kb_ref[...] = kb.astype(BF16)
    vbt_ref[0] = kv[:, 512:1024].T.astype(BF16)

    if idx_precision is None:
        yi = jnp.dot(hb, widx_ref[...], preferred_element_type=F32)
    else:
        yi = jnp.dot(h, widx_ref[...], preferred_element_type=F32, precision=idx_precision)
    qi_ref[...] = _rope(yi[:, 0:256], *rot).astype(qi_ref.dtype)
    blk = yi[:, 256:384]
    lane = lax.broadcasted_iota(jnp.int32, blk.shape, 1)
    ms = jnp.sum(jnp.where(lane < IDX_DIM, blk * blk, 0.0), axis=-1, keepdims=True) * (1.0 / IDX_DIM)
    kin = blk * lax.rsqrt(ms + EPS) * gkidx_ref[...]
    kir = _rope(kin, *rot)
    kik = kir + pltpu.roll(kir, IDX_DIM, axis=1)
    kik_ref[...] = kik.astype(kik_ref.dtype)
    wi_ref[...] = blk * (IDX_HEADS ** -0.5 * IDX_DIM ** -0.5)


def _in0_call(x2, gmix, wmain, widx, wkvup, pos, inv, bd, gqa, gka, gqb, gkb, gkv, gkidx,
              *, tm, idx_precision):
    t, d = x2.shape
    row = lambda w: pl.BlockSpec((tm, w), lambda i: (i, 0))
    fulls = [gmix, wmain, widx, wkvup]
    gains = [inv, bd, gqa, gka, gqb, gkb, gkv, gkidx]
    in_specs = ([row(d)] + [_full_spec(a) for a in fulls] + [pl.BlockSpec((1, 1, tm), lambda i: (i, 0, 0))]
                + [_full_spec(a) for a in gains])
    idx_dt = BF16 if idx_precision is None else F32
    sds = jax.ShapeDtypeStruct
    vt_shape, vt_spec = sds((t // tm, 512, tm), BF16), pl.BlockSpec((1, 512, tm), lambda i: (i, 0, 0))
    out_shape = ([sds((t, 512), BF16)] * 2 + [vt_shape] + [sds((t, 512), BF16)] * 2
                 + [vt_shape, sds((t, 256), idx_dt), sds((t, LANES), idx_dt), sds((t, LANES), F32)])
    out_specs = [row(512)] * 2 + [vt_spec] + [row(512)] * 2 + [vt_spec, row(256), row(LANES), row(LANES)]
    return pl.pallas_call(
        functools.partial(_in0_body, idx_precision=idx_precision),
        grid=(t // tm,),
        in_specs=in_specs,
        out_specs=out_specs,
        out_shape=out_shape,
        compiler_params=_cparams("parallel"),
        name="in0",
    )(x2, gmix, wmain, widx, wkvup, pos, inv, bd, gqa, gka, gqb, gkb, gkv, gkidx)


def _diff_attn_body(lam_ref, q_ref, k_ref, vt_ref, gsub_ref, o_ref, s_ref, *, tq, tk, out_scale):
    i = pl.program_id(2)
    q0 = i * tq
    n_full = (q0 + 1) // tk
    nkc = (q0 + tq + tk - 1) // tk
    lane = lax.broadcasted_iota(jnp.int32, (1, LANES), 1)
    lane_lo = lane < HEAD_DIM
    qpos = q0 + lax.broadcasted_iota(jnp.int32, (1, tq), 1)
    qpos2 = jnp.concatenate([qpos, qpos], axis=1)
    neg_inf = jnp.float32(-jnp.inf)
    heads = range(q_ref.shape[1] // LANES)
    q_cat = []
    for g in heads:
        q_h = q_ref[:, g * LANES:(g + 1) * LANES]
        q_cat.append(jnp.concatenate([q_h * lane_lo.astype(BF16), q_h * (~lane_lo).astype(BF16)], axis=0))

    def logit_chunk(masked):
        def body(c, carry):
            rows = pl.ds(pl.multiple_of(c * tk, tk), tk)
            out = []
            for g, m in zip(heads, carry):
                s = lax.dot_general(k_ref[rows, g * LANES:(g + 1) * LANES], q_cat[g], NT_DIMS,
                                    preferred_element_type=F32)
                if masked:
                    kpos = c * tk + lax.broadcasted_iota(jnp.int32, (tk, 1), 0)
                    s = jnp.where(kpos <= qpos2, s, neg_inf)
                s_ref[g, c] = s
                out.append(jnp.maximum(m, jnp.max(s.reshape(tk // SUBLANES, SUBLANES, 2 * tq), axis=0)))
            return tuple(out)
        return body

    m8 = tuple(jnp.full((SUBLANES, 2 * tq), neg_inf, F32) for _ in heads)
    m8 = lax.fori_loop(0, n_full, logit_chunk(False), m8)
    m8 = lax.fori_loop(n_full, nkc, logit_chunk(True), m8)
    shift = [jnp.max(m, axis=0, keepdims=True) for m in m8]

    def pv_chunk(c, carry):
        out = []
        for g, (l8, acc) in zip(heads, carry):
            pe = jnp.exp2(s_ref[g, c] - shift[g])
            acc = acc + jnp.dot(vt_ref[c, g * LANES:(g + 1) * LANES, :], pe.astype(BF16),
                                preferred_element_type=F32)
            out.append((l8 + jnp.sum(pe.reshape(tk // SUBLANES, SUBLANES, 2 * tq), axis=0), acc))
        return tuple(out)

    fin = lax.fori_loop(0, nkc, pv_chunk, tuple((jnp.zeros((SUBLANES, 2 * tq), F32),
                                                 jnp.zeros((LANES, 2 * tq), F32)) for _ in heads))
    lam = lam_ref[0]
    for g, (l8, acc) in zip(heads, fin):
        o_all = acc / jnp.sum(l8, axis=0, keepdims=True)
        o = o_all[:, 0:tq] - lam * o_all[:, tq:2 * tq]
        o = o * lax.rsqrt(jnp.mean(o * o, axis=0, keepdims=True) + EPS)
        o_ref[:, g * LANES:(g + 1) * LANES] = (o.T * gsub_ref[...] * out_scale).astype(o_ref.dtype)


def _diff_attn_call(lam, qa, ka, vat, gsub, *, batch, seq, tq, tk, group, out_scale):
    t = qa.shape[0]
    w = group * LANES
    nq, nk = seq // tq, seq // tk
    q_spec = pl.BlockSpec((tq, w), lambda b, h, i: (b * nq + i, h))
    return pl.pallas_call(
        functools.partial(_diff_attn_body, tq=tq, tk=tk, out_scale=out_scale),
        grid=(batch, qa.shape[1] // w, nq),
        in_specs=[pl.BlockSpec(memory_space=pltpu.SMEM), q_spec,
                  pl.BlockSpec((seq, w), lambda b, h, i: (b, h)),
                  pl.BlockSpec((nk, w, tk), lambda b, h, i: (b, h, 0)), _full_spec(gsub)],
        out_specs=q_spec,
        out_shape=jax.ShapeDtypeStruct((t, qa.shape[1]), BF16),
        scratch_shapes=[pltpu.VMEM((group, nk, tk, 2 * tq), F32)],
        compiler_params=_cparams("parallel", "parallel", "arbitrary"),
        name="diff_attn",
    )(lam, qa, ka, vat, gsub)


def _key_to_float(key):
    bits = key ^ ((key >> 31) & jnp.int32(0x7FFFFFFF))
    return lax.bitcast_convert_type(bits, F32)


def _dsa_body(qi_ref, wi_ref, qb_ref, kik_ref, kb_ref, vbt_ref, tri_ref, o_ref, sc_ref, bias_ref, s_ref,
              *, tq, tk, k_sel, idx_precision):
    i = pl.program_id(1)
    q0 = i * tq
    nkc = (q0 + tq + tk - 1) // tk
    qpos = q0 + lax.broadcasted_iota(jnp.int32, (1, tq), 1)
    lane = lax.broadcasted_iota(jnp.int32, (1, LANES), 1)
    lane_lo = lane < HEAD_DIM
    tiles = tk // LANES
    neg_inf = jnp.float32(-jnp.inf)

    wi_t = wi_ref[...].T
    qi = qi_ref[...]
    q_heads = []
    for h in range(IDX_HEADS):
        pair = qi[:, (h // 2) * LANES:(h // 2 + 1) * LANES]
        q_heads.append(pair * (lane_lo if h % 2 == 0 else ~lane_lo).astype(pair.dtype))
    qi_cat = jnp.concatenate(q_heads, axis=0)
    w_heads = [wi_t[IDX_DIM + h:IDX_DIM + h + 1, :] for h in range(IDX_HEADS)]

    def score_chunk(c, carry):
        kk = kik_ref[pl.ds(pl.multiple_of(c * tk, tk), tk), :]
        s = lax.dot_general(kk, qi_cat, NT_DIMS, preferred_element_type=F32, precision=idx_precision)
        acc = jnp.zeros((tk, tq), F32)
        for h in range(IDX_HEADS):
            acc = acc + jnp.maximum(s[:, h * tq:(h + 1) * tq], 0.0) * w_heads[h]
        kpos = c * tk + lax.broadcasted_iota(jnp.int32, (tk, 1), 0)
        sc_ref[c] = jnp.where(kpos <= qpos, acc, neg_inf)
        return carry

    lax.fori_loop(0, nkc, score_chunk, 0)

    acc_rows = 4 * SUBLANES

    def count_ge(cand):
        def body(c, acc):
            hit = jnp.where(sc_ref[c] >= cand, 1.0, 0.0)
            return acc + jnp.sum(hit.reshape(tk // acc_rows, acc_rows, tq), axis=0)
        acc = lax.fori_loop(0, nkc, body, jnp.zeros((acc_rows, tq), F32))
        return jnp.sum(acc, axis=0, keepdims=True)

    ksel = float(k_sel)

    def count_at(cand_key):
        return jnp.where(cand_key <= KEY_NEG_INF, ksel, count_ge(_key_to_float(cand_key)))

    zero_key = jnp.zeros((1, tq), jnp.int32)
    thr = jnp.where(count_at(zero_key) >= ksel, zero_key, INT32_MIN)

    def bit_step(b, thr):
        cand = thr + jnp.left_shift(jnp.int32(1), 30 - b)
        return jnp.where(count_at(cand) >= ksel, cand, thr)

    thr = lax.fori_loop(0, 31, bit_step, thr)
    thr_f = _key_to_float(thr)
    above_f = _key_to_float(thr + 1)
    need = ksel - count_ge(above_f)
    tri = tri_ref[...]

    def select_chunk(c, taken):
        for t in range(tiles):
            st = sc_ref[c, t * LANES:(t + 1) * LANES, :]
            is_above = st >= above_f
            tied = jnp.where(is_above, 0.0, jnp.where(st >= thr_f, 1.0, 0.0))
            prefix = jnp.dot(tri, tied.astype(BF16), preferred_element_type=F32)
            tie_sel = jnp.where((taken + prefix) <= need, tied, 0.0)
            sel = jnp.where(is_above, 0.0, jnp.where(tie_sel > 0.0, 0.0, neg_inf))
            kpos = c * tk + t * LANES + lax.broadcasted_iota(jnp.int32, (LANES, 1), 0)
            bias_ref[c, t * LANES:(t + 1) * LANES, :] = jnp.where(kpos <= qpos, sel, neg_inf)
            taken = taken + prefix[LANES - 1:LANES, :]
        return taken

    lax.fori_loop(0, nkc, select_chunk, jnp.zeros((1, tq), F32))

    group = s_ref.shape[0]
    for p0 in range(0, qb_ref.shape[1] // LANES, group):
        pairs = range(p0, p0 + group)
        q_cat = []
        for p in pairs:
            q_pair = qb_ref[:, p * LANES:(p + 1) * LANES]
            q_cat.append(jnp.concatenate([q_pair * lane_lo.astype(BF16), q_pair * (~lane_lo).astype(BF16)],
                                         axis=0))

        def logit_chunk(c, carry, pairs=pairs, q_cat=q_cat):
            bias = bias_ref[c]
            bias2 = jnp.concatenate([bias, bias], axis=1)
            out = []
            for g, (p, m) in enumerate(zip(pairs, carry)):
                kc = kb_ref[pl.ds(pl.multiple_of(c * tk, tk), tk), p * LANES:(p + 1) * LANES]
                s = lax.dot_general(kc, q_cat[g], NT_DIMS, preferred_element_type=F32) + bias2
                s_ref[g, c] = s
                out.append(jnp.maximum(m, jnp.max(s.reshape(tk // SUBLANES, SUBLANES, 2 * tq), axis=0)))
            return tuple(out)

        m8 = lax.fori_loop(0, nkc, logit_chunk,
                           tuple(jnp.full((SUBLANES, 2 * tq), neg_inf, F32) for _ in pairs))
        shift = [jnp.max(m, axis=0, keepdims=True) for m in m8]

        def pv_chunk(c, carry, pairs=pairs, shift=shift):
            out = []
            for g, (p, (l8, acc)) in enumerate(zip(pairs, carry)):
                pe = jnp.exp2(s_ref[g, c] - shift[g])
                vt = vbt_ref[c, p * LANES:(p + 1) * LANES, :]
                acc = acc + jnp.dot(vt, pe.astype(BF16), preferred_element_type=F32)
                out.append((l8 + jnp.sum(pe.reshape(tk // SUBLANES, SUBLANES, 2 * tq), axis=0), acc))
            return tuple(out)

        fin = lax.fori_loop(0, nkc, pv_chunk, tuple((jnp.zeros((SUBLANES, 2 * tq), F32),
                                                     jnp.zeros((LANES, 2 * tq), F32)) for _ in pairs))
        for p, (l8, acc) in zip(pairs, fin):
            o_all = acc / jnp.sum(l8, axis=0, keepdims=True)
            o_t = jnp.concatenate([o_all[0:HEAD_DIM, 0:tq], o_all[HEAD_DIM:LANES, tq:2 * tq]], axis=0)
            o_ref[:, p * LANES:(p + 1) * LANES] = o_t.T.astype(o_ref.dtype)


def _dsa_call(qi, wi, qb, kik, kb, vbt, tri, *, batch, seq, tq, tk, k_sel, idx_precision):
    t = qb.shape[0]
    nq, nk = seq // tq, seq // tk
    row = lambda w: pl.BlockSpec((tq, w), lambda b, i: (b * nq + i, 0))
    seqb = lambda w: pl.BlockSpec((seq, w), lambda b, i: (b, 0))
    vbt_spec = pl.BlockSpec((nk,) + vbt.shape[1:], lambda b, i: (b, 0, 0))
    return pl.pallas_call(
        functools.partial(_dsa_body, tq=tq, tk=tk, k_sel=k_sel, idx_precision=idx_precision),
        grid=(batch, nq),
        in_specs=[row(qi.shape[1]), row(LANES), row(qb.shape[1]), seqb(LANES), seqb(kb.shape[1]),
                  vbt_spec, _full_spec(tri)],
        out_specs=row(qb.shape[1]),
        out_shape=jax.ShapeDtypeStruct((t, qb.shape[1]), BF16),
        scratch_shapes=[pltpu.VMEM((nk, tk, tq), F32), pltpu.VMEM((nk, tk, tq), F32),
                        pltpu.VMEM((2, nk, tk, 2 * tq), F32)],
        compiler_params=_cparams("parallel", "arbitrary"),
        name="dsa",
    )(qi, wi, qb, kik, kb, vbt, tri)


def _memkv_body(mem_ref, g_ref, w_ref, gk_ref, k_ref, v_ref):
    hm = _rms(mem_ref[...], g_ref[...]).astype(BF16)
    kv = jnp.dot(hm, w_ref[...], preferred_element_type=F32)
    half = kv.shape[1] // 2
    for h in range(X_HEADS):
        kh = kv[:, h * X_HEAD_DIM:(h + 1) * X_HEAD_DIM]
        k_ref[:, h * X_HEAD_DIM:(h + 1) * X_HEAD_DIM] = _rms(kh, gk_ref[...]).astype(BF16)
    v_ref[...] = kv[:, half:].astype(BF16)


def _memkv_call(mem2, g, w, gk):
    rows, d = mem2.shape
    half = w.shape[1] // 2
    tm = min(rows, 512)
    row = lambda wd: pl.BlockSpec((tm, wd), lambda i: (i, 0))
    return pl.pallas_call(
        _memkv_body,
        grid=(rows // tm,),
        in_specs=[row(d), _full_spec(g), _full_spec(w), _full_spec(gk)],
        out_specs=[row(half), row(half)],
        out_shape=[jax.ShapeDtypeStruct((rows, half), BF16)] * 2,
        compiler_params=_cparams("parallel"),
        name="memkv",
    )(mem2, g, w, gk)


def _xattn_tail(x1, gx_ref, wq_ref, gq_ref, kx_ref, vx_ref, wo_ref):
    hx = _rms(x1, gx_ref[...]).astype(BF16)
    q = jnp.dot(hx, wq_ref[...], preferred_element_type=F32)
    outs = []
    for h in range(X_HEADS):
        cols = slice(h * X_HEAD_DIM, (h + 1) * X_HEAD_DIM)
        qh = _rms(q[:, cols], gq_ref[...]).astype(BF16)
        s = lax.dot_general(qh, kx_ref[0, :, cols], NT_DIMS, preferred_element_type=F32)
        p = jnp.exp(s - jnp.max(s, axis=-1, keepdims=True))
        o = (jnp.dot(p.astype(BF16), vx_ref[0, :, cols], preferred_element_type=F32)
             / jnp.sum(p, axis=-1, keepdims=True))
        outs.append(o.astype(BF16))
    o = jnp.concatenate(outs, axis=1)
    return x1 + jnp.dot(o, wo_ref[...], preferred_element_type=F32)


def _out0_body(x_ref, oa_ref, ob_ref, woa_ref, wob_ref, gx_ref, wq_ref, gq_ref, kx_ref, vx_ref, wo_ref,
               o_ref):
    mix = (jnp.dot(oa_ref[...], woa_ref[...], preferred_element_type=F32)
           + jnp.dot(ob_ref[...], wob_ref[...], preferred_element_type=F32))
    o_ref[...] = _xattn_tail(x_ref[...] + mix, gx_ref, wq_ref, gq_ref, kx_ref, vx_ref, wo_ref)


def _out0_call(x2, oa, ob, woa, wob, gx, wq, gq, kx, vx, wo, *, seq, tm):
    t, d = x2.shape
    row = lambda w: pl.BlockSpec((tm, w), lambda i: (i, 0))
    mem_spec = pl.BlockSpec((1,) + kx.shape[1:], lambda i: ((i * tm) // seq, 0, 0))
    fulls = [woa, wob, gx, wq, gq]
    return pl.pallas_call(
        _out0_body,
        grid=(t // tm,),
        in_specs=[row(d), row(oa.shape[1]), row(ob.shape[1])] + [_full_spec(a) for a in fulls]
                 + [mem_spec, mem_spec, _full_spec(wo)],
        out_specs=row(d),
        out_shape=jax.ShapeDtypeStruct((t, d), F32),
        compiler_params=_cparams("parallel"),
        name="out0",
    )(x2, oa, ob, woa, wob, gx, wq, gq, kx, vx, wo)


def _mlp_body(x_ref, g_ref, wup_ref, wdn_ref, o_ref, h_ref, acc_ref):
    j = pl.program_id(1)

    @pl.when(j == 0)
    def _():
        h_ref[...] = _rms(x_ref[...], g_ref[...]).astype(BF16)
        acc_ref[...] = jnp.zeros(acc_ref.shape, F32)

    u = jnp.dot(h_ref[...], wup_ref[...], preferred_element_type=F32)
    a = jnp.square(jnp.maximum(u, 0.0)).astype(BF16)
    acc_ref[...] += jnp.dot(a, wdn_ref[...], preferred_element_type=F32)

    @pl.when(j == pl.num_programs(1) - 1)
    def _():
        o_ref[...] = x_ref[...] + acc_ref[...]


def _mlp_call(x2, g, wup, wdn, layer, *, tm, tf):
    t, d = x2.shape
    ff = wup.shape[2]
    return pl.pallas_call(
        _mlp_body,
        grid=(t // tm, ff // tf),
        in_specs=[pl.BlockSpec((tm, d), lambda i, j: (i, 0)), _full_spec(g),
                  pl.BlockSpec((None, d, tf), lambda i, j: (layer, 0, j)),
                  pl.BlockSpec((None, tf, d), lambda i, j: (layer, j, 0))],
        out_specs=pl.BlockSpec((tm, d), lambda i, j: (i, 0)),
        out_shape=jax.ShapeDtypeStruct((t, d), F32),
        scratch_shapes=[pltpu.VMEM((tm, d), BF16), pltpu.VMEM((tm, d), F32)],
        compiler_params=_cparams("parallel", "arbitrary"),
        name="mlp",
    )(x2, g, wup, wdn)


def _mix1_body(x_ref, xh_ref, gmix_ref, win_ref, cw_ref, wpool_ref, ps_ref, wout_ref,
               gx_ref, wq_ref, gq_ref, kx_ref, vx_ref, wo_ref, o_ref, *, tm, seq):
    i = pl.program_id(0)
    x = x_ref[...]
    xf = jnp.concatenate([xh_ref[...], x], axis=0)
    hf = _rms(xf, gmix_ref[...]).astype(BF16)
    y = jnp.dot(hf, win_ref[...], preferred_element_type=F32)
    cw = y.shape[1] // 4
    pos0 = (i * tm) % seq
    rid = lax.broadcasted_iota(jnp.int32, (tm + POOL_HALO, 1), 0)
    keep = jnp.where(rid >= POOL_HALO, 1.0, jnp.where(pos0 > 0, 1.0, 0.0))

    gb = y[POOL_HALO:, 0:cw]
    u = y[:, cw:2 * cw] * y[:, 2 * cw:3 * cw] * keep
    z = y[:, 3 * cw:4 * cw] * keep

    taps = cw_ref[...]
    conv = taps[CONV_W - 1:CONV_W, :] * u[POOL_HALO:]
    for back in range(1, CONV_W):
        conv = conv + taps[CONV_W - 1 - back:CONV_W - back, :] * pltpu.roll(u, back, axis=0)[POOL_HALO:]
    yc = gb * conv

    pos = pos0 + lax.broadcasted_iota(jnp.int32, (tm, 1), 0)
    gw = cw // len(POOL_WINDOWS)
    parts = []
    for g, w in enumerate(POOL_WINDOWS):
        zg = z[:, g * gw:(g + 1) * gw]
        s = zg
        sh = 1
        while sh < w:
            s = s + pltpu.roll(s, sh, axis=0)
            sh *= 2
        cnt = jnp.minimum(pos + 1, w).astype(F32)
        pooled = s[POOL_HALO:] / cnt - zg[POOL_HALO:]
        parts.append(jnp.dot(pooled.astype(BF16), wpool_ref[g], preferred_element_type=F32))
    yd = jnp.concatenate(parts, axis=1) * ps_ref[...]

    mix = (jnp.dot(yc.astype(BF16), wout_ref[0:cw, :], preferred_element_type=F32)
           + jnp.dot(yd.astype(BF16), wout_ref[cw:2 * cw, :], preferred_element_type=F32))
    o_ref[...] = _xattn_tail(x + mix, gx_ref, wq_ref, gq_ref, kx_ref, vx_ref, wo_ref)


def _mix1_call(x2, gmix, win, cw, wpool, ps, wout, gx, wq, gq, kx, vx, wo, *, seq, tm):
    t, d = x2.shape
    row = pl.BlockSpec((tm, d), lambda i: (i, 0))
    halo = pl.BlockSpec((POOL_HALO, d), lambda i: (jnp.maximum(i * (tm // POOL_HALO) - 1, 0), 0))
    mem_spec = pl.BlockSpec((1,) + kx.shape[1:], lambda i: ((i * tm) // seq, 0, 0))
    fulls = [gmix, win, cw, wpool, ps, wout, gx, wq, gq]
    return pl.pallas_call(
        functools.partial(_mix1_body, tm=tm, seq=seq),
        grid=(t // tm,),
        in_specs=[row, halo] + [_full_spec(a) for a in fulls] + [mem_spec, mem_spec, _full_spec(wo)],
        out_specs=row,
        out_shape=jax.ShapeDtypeStruct((t, d), F32),
        compiler_params=_cparams("parallel"),
        name="mix1",
    )(x2, x2, gmix, win, cw, wpool, ps, wout, gx, wq, gq, kx, vx, wo)


def _row(v, reps=1):
    return jnp.tile(v.astype(F32), reps).reshape(1, -1)


def kernel(x, mem, positions, g_mix, g_xattn, g_mem, g_mlp, wq_x, wkv_x, gq_x, gk_x, wo_x, w_up, w_down,
           w_in_e, gq_a, gk_a, lam_q1, lam_k1, lam_q2, lam_k2, g_sub_a, g_kv_b, w_kv_up_b, gq_b, gk_b,
           g_kidx, w_out_e, w_in_o, conv_w, w_pool, pool_scale, w_out_o):
    batch, seq, d = x.shape
    t = batch * seq
    k_sel = min(TOPK_MAX, seq // 4)
    tm = min(256, seq)
    tq_b, tk_b = min(256, seq), min(512, seq)
    assert seq % tm == 0 and seq % tq_b == 0 and seq % tk_b == 0 and tk_b >= k_sel

    x2 = x.reshape(t, d)
    pos = positions.reshape(t // tk_b, 1, tk_b)
    inv_freq = (ROPE_THETA ** (-jnp.arange(0, ROT_DIM, 2, dtype=F32) / ROT_DIM)).reshape(ROT_HALF, 1)
    mem2 = mem.reshape(batch * mem.shape[1], d)
    depth = g_mix.shape[0]
    w_up_b, w_down_b = w_up.astype(BF16), w_down.astype(BF16)

    for layer in range(depth):
        j = layer // 2
        kx, vx = _memkv_call(mem2, _row(g_mem[layer]), wkv_x[layer].astype(BF16), _row(gk_x[layer]))
        kx = kx.reshape(batch, mem.shape[1], -1)
        vx = vx.reshape(batch, mem.shape[1], -1)
        tail = (_row(g_xattn[layer]), wq_x[layer].astype(BF16), _row(gq_x[layer]) * (X_HEAD_DIM ** -0.5),
                kx, vx, wo_x[layer].astype(BF16))
        if layer % 2 == 0:
            lam_init = 0.8 - 0.6 * float(np.exp(-0.3 * layer))
            lam = (jnp.exp(jnp.sum(lam_q1[j].astype(F32) * lam_k1[j].astype(F32)))
                   - jnp.exp(jnp.sum(lam_q2[j].astype(F32) * lam_k2[j].astype(F32))) + lam_init)
            w_in = w_in_e[j]
            n_main = 2304
            n_idx = w_in.shape[1] - n_main
            widx = jnp.pad(w_in[:, n_main:], ((0, 0), (0, 3 * LANES - n_idx)))
            if IDX_PRECISION is None:
                widx = widx.astype(BF16)
            heads8 = 512 // HEAD_DIM
            bd = jnp.kron(jnp.eye(heads8, dtype=F32), jnp.ones((HEAD_DIM, HEAD_DIM), F32)).astype(BF16)
            gkidx = jnp.pad(g_kidx[j].astype(F32), (0, LANES - IDX_DIM)).reshape(1, LANES)
            qa, ka, vat, qb, kb, vbt, qi, kik, wi = _in0_call(
                x2, _row(g_mix[layer]), w_in[:, :n_main].astype(BF16), widx, w_kv_up_b[j].astype(BF16),
                pos, inv_freq, bd, _row(gq_a[j], heads8), _row(gk_a[j], heads8), _row(gq_b[j], heads8),
                _row(gk_b[j], heads8), _row(g_kv_b[j]), gkidx, tm=tk_b, idx_precision=IDX_PRECISION)
            out_a = _diff_attn_call(lam.reshape(1).astype(F32), qa, ka, vat, _row(g_sub_a[j]),
                                    batch=batch, seq=seq, tq=tq_b, tk=tk_b, group=2, out_scale=1.0 - lam_init)
            tri = jnp.tril(jnp.ones((LANES, LANES), F32)).astype(BF16)
            out_b = _dsa_call(qi, wi, qb, kik, kb, vbt, tri, batch=batch, seq=seq, tq=tq_b, tk=tk_b,
                              k_sel=k_sel, idx_precision=IDX_PRECISION)
            w_out = w_out_e[j].astype(BF16)
            x2 = _out0_call(x2, out_a, out_b, w_out[:512], w_out[512:], *tail, seq=seq, tm=tm)
        else:
            taps = jnp.pad(conv_w[j].astype(F32), ((0, SUBLANES - CONV_W), (0, 0)))
            x2 = _mix1_call(x2, _row(g_mix[layer]), w_in_o[j].astype(BF16), taps, w_pool[j].astype(BF16),
                            _row(pool_scale[j]), w_out_o[j].astype(BF16), *tail, seq=seq, tm=tm)
        x2 = _mlp_call(x2, _row(g_mlp[layer]), w_up_b, w_down_b, layer, tm=min(1024, t), tf=512)
    return x2.reshape(batch, seq, d)
```

```python
import functools

import numpy as np
import jax
import jax.numpy as jnp
from jax import lax
from jax.experimental import pallas as pl
from jax.experimental.pallas import tpu as pltpu

F32 = jnp.float32
BF16 = jnp.bfloat16

HEAD_DIM = 64
ROT_DIM = HEAD_DIM // 4
ROT_HALF = ROT_DIM // 2
ROPE_THETA = 500000.0
IDX_HEADS = 4
IDX_DIM = 64
TOPK_MAX = 256
CONV_W = 3
POOL_WINDOWS = (2, 4, 8, 16)
POOL_HALO = 16
X_HEADS = 4
X_HEAD_DIM = 128
EPS = 1e-6
LOG2_E = 1.4426950408889634

LANES = 128
SUBLANES = 8
VMEM_LIMIT_BYTES = 56 * 1024 * 1024

KEY_NEG_INF = np.int32(-2139095041)
INT32_MIN = np.int32(-(2 ** 31))

NT_DIMS = (((1,), (1,)), ((), ()))

IDX_PRECISION = None


def _cparams(*sem):
    return pltpu.CompilerParams(dimension_semantics=sem, vmem_limit_bytes=VMEM_LIMIT_BYTES)


def _full_spec(arr):
    nd = arr.ndim
    return pl.BlockSpec(arr.shape, lambda *_: (0,) * nd)


def _rms(x, g):
    ms = jnp.mean(x * x, axis=-1, keepdims=True)
    return x * lax.rsqrt(ms + EPS) * g


def _tile_lanes(t, width):
    reps = width // t.shape[-1]
    return t if reps == 1 else jnp.concatenate([t] * reps, axis=1)


def _rope_tables(pos_row, inv_col):
    ang = inv_col * pos_row
    cos, sin = jnp.cos(ang), jnp.sin(ang)
    rest = (HEAD_DIM - ROT_DIM, ang.shape[1])
    head_c = jnp.concatenate([cos, cos, jnp.ones(rest, F32)], axis=0)
    head_s = jnp.concatenate([-sin, sin, jnp.zeros(rest, F32)], axis=0)
    return (jnp.concatenate([head_c, head_c], axis=0).T, jnp.concatenate([head_s, head_s], axis=0).T)


def _rope(y, c, s):
    w = y.shape[-1]
    c, s = _tile_lanes(c, w), _tile_lanes(s, w)
    lane = lax.broadcasted_iota(jnp.int32, (1, w), 1)
    upper = pltpu.roll(y, w - ROT_HALF, axis=1)
    lower = pltpu.roll(y, ROT_HALF, axis=1)
    partner = jnp.where((lane & (HEAD_DIM - 1)) < ROT_HALF, upper, lower)
    return y * c + partner * s


def _headnorm64(y, blockdiag, g):
    ss = jnp.dot((y * y).astype(BF16), blockdiag, preferred_element_type=F32)
    return y * lax.rsqrt(ss * (1.0 / HEAD_DIM) + EPS) * g


def _in0_body(x_ref, gmix_ref, wmain_ref, widx_ref, wkvup_ref, pos_ref, inv_ref,
              bd_ref, gqa_ref, gka_ref, gqb_ref, gkb_ref, gkv_ref, gkidx_ref,
              qa_ref, ka_ref, vat_ref, qb_ref, kb_ref, vbt_ref, qi_ref, kik_ref, wi_ref,
              *, idx_precision):
    h = _rms(x_ref[...], gmix_ref[...])
    hb = h.astype(BF16)
    proj = lambda lo, hi: jnp.dot(hb, wmain_ref[:, lo:hi], preferred_element_type=F32)
    rot = _rope_tables(pos_ref[0].astype(F32), inv_ref[...])
    bd = bd_ref[...]
    scale = HEAD_DIM ** -0.5 * LOG2_E

    qa = _rope(_headnorm64(proj(0, 512), bd, gqa_ref[...]), *rot) * scale
    qa_ref[...] = qa.astype(BF16)
    ka = _rope(_headnorm64(proj(512, 1024), bd, gka_ref[...]), *rot)
    ka_ref[...] = ka.astype(BF16)
    tk = vat_ref.shape[2]
    va = proj(1024, 1536)
    for r in range(vat_ref.shape[0]):
        vat_ref[r] = va[r * tk:(r + 1) * tk, :].T.astype(BF16)
    qb = _rope(_headnorm64(proj(1536, 2048), bd, gqb_ref[...]), *rot) * scale
    qb_ref[...] = qb.astype(BF16)

    ckv = _rms(proj(2048, 2304), gkv_ref[...]).astype(BF16)
    kv = jnp.dot(ckv, wkvup_ref[...], preferred_element_type=F32)
    kb = _rope(_headnorm64(kv[:, 0:512], bd, gkb_ref[...]), *rot)
    kb_ref[...] = kb.astype(BF16)
    for r in range(vbt_ref.shape[0]):
        vbt_ref[r] = kv[r * tk:(r + 1) * tk, 512:1024].T.astype(BF16)

    if idx_precision is None:
        yi = jnp.dot(hb, widx_ref[...], preferred_element_type=F32)
    else:
        yi = jnp.dot(h, widx_ref[...], preferred_element_type=F32, precision=idx_precision)
    qi_ref[...] = _rope(yi[:, 0:256], *rot).astype(qi_ref.dtype)
    blk = yi[:, 256:384]
    lane = lax.broadcasted_iota(jnp.int32, blk.shape, 1)
    ms = jnp.sum(jnp.where(lane < IDX_DIM, blk * blk, 0.0), axis=-1, keepdims=True) * (1.0 / IDX_DIM)
    kin = blk * lax.rsqrt(ms + EPS) * gkidx_ref[...]
    kir = _rope(kin, *rot)
    kik = kir + pltpu.roll(kir, IDX_DIM, axis=1)
    kik_ref[...] = kik.astype(kik_ref.dtype)
    wi_ref[...] = blk * (IDX_HEADS ** -0.5 * IDX_DIM ** -0.5)


def _in0_call(x2, gmix, wmain, widx, wkvup, pos, inv, bd, gqa, gka, gqb, gkb, gkv, gkidx,
              *, tm, tk, idx_precision):
    t, d = x2.shape
    row = lambda w: pl.BlockSpec((tm, w), lambda i: (i, 0))
    fulls = [gmix, wmain, widx, wkvup]
    gains = [inv, bd, gqa, gka, gqb, gkb, gkv, gkidx]
    in_specs = ([row(d)] + [_full_spec(a) for a in fulls] + [pl.BlockSpec((1, 1, tm), lambda i: (i, 0, 0))]
                + [_full_spec(a) for a in gains])
    idx_dt = BF16 if idx_precision is None else F32
    sds = jax.ShapeDtypeStruct
    vt_shape, vt_spec = sds((t // tk, 512, tk), BF16), pl.BlockSpec((tm // tk, 512, tk), lambda i: (i, 0, 0))
    out_shape = ([sds((t, 512), BF16)] * 2 + [vt_shape] + [sds((t, 512), BF16)] * 2
                 + [vt_shape, sds((t, 256), idx_dt), sds((t, LANES), idx_dt), sds((t, LANES), F32)])
    out_specs = [row(512)] * 2 + [vt_spec] + [row(512)] * 2 + [vt_spec, row(256), row(LANES), row(LANES)]
    return pl.pallas_call(
        functools.partial(_in0_body, idx_precision=idx_precision),
        grid=(t // tm,),
        in_specs=in_specs,
        out_specs=out_specs,
        out_shape=out_shape,
        compiler_params=_cparams("parallel"),
        name="in0",
    )(x2, gmix, wmain, widx, wkvup, pos, inv, bd, gqa, gka, gqb, gkb, gkv, gkidx)


def _diff_attn_body(lam_ref, q_ref, k_ref, vt_ref, gsub_ref, o_ref, s_ref, *, tq, tk, out_scale):
    i = pl.program_id(2)
    q0 = i * tq
    n_full = (q0 + 1) // tk
    nkc = (q0 + tq + tk - 1) // tk
    lane = lax.broadcasted_iota(jnp.int32, (1, LANES), 1)
    lane_lo = lane < HEAD_DIM
    qpos = q0 + lax.broadcasted_iota(jnp.int32, (1, tq), 1)
    qpos2 = jnp.concatenate([qpos, qpos], axis=1)
    neg_inf = jnp.float32(-jnp.inf)
    heads = range(q_ref.shape[1] // LANES)
    q_cat = []
    for g in heads:
        q_h = q_ref[:, g * LANES:(g + 1) * LANES]
        q_cat.append(jnp.concatenate([q_h * lane_lo.astype(BF16), q_h * (~lane_lo).astype(BF16)], axis=0))

    def logit_chunk(masked):
        def body(c, carry):
            rows = pl.ds(pl.multiple_of(c * tk, tk), tk)
            out = []
            for g, m in zip(heads, carry):
                s = lax.dot_general(k_ref[rows, g * LANES:(g + 1) * LANES], q_cat[g], NT_DIMS,
                                    preferred_element_type=F32)
                if masked:
                    kpos = c * tk + lax.broadcasted_iota(jnp.int32, (tk, 1), 0)
                    s = jnp.where(kpos <= qpos2, s, neg_inf)
                s_ref[g, c] = s
                out.append(jnp.maximum(m, jnp.max(s.reshape(tk // SUBLANES, SUBLANES, 2 * tq), axis=0)))
            return tuple(out)
        return body

    m8 = tuple(jnp.full((SUBLANES, 2 * tq), neg_inf, F32) for _ in heads)
    m8 = lax.fori_loop(0, n_full, logit_chunk(False), m8)
    m8 = lax.fori_loop(n_full, nkc, logit_chunk(True), m8)
    shift = [jnp.max(m, axis=0, keepdims=True) for m in m8]

    def pv_chunk(c, carry):
        out = []
        for g, (l8, acc) in zip(heads, carry):
            pe = jnp.exp2(s_ref[g, c] - shift[g])
            acc = acc + jnp.dot(vt_ref[c, g * LANES:(g + 1) * LANES, :], pe.astype(BF16),
                                preferred_element_type=F32)
            out.append((l8 + jnp.sum(pe.reshape(tk // SUBLANES, SUBLANES, 2 * tq), axis=0), acc))
        return tuple(out)

    fin = lax.fori_loop(0, nkc, pv_chunk, tuple((jnp.zeros((SUBLANES, 2 * tq), F32),
                                                 jnp.zeros((LANES, 2 * tq), F32)) for _ in heads))
    lam = lam_ref[0]
    for g, (l8, acc) in zip(heads, fin):
        o_all = acc / jnp.sum(l8, axis=0, keepdims=True)
        o = o_all[:, 0:tq] - lam * o_all[:, tq:2 * tq]
        o = o * lax.rsqrt(jnp.mean(o * o, axis=0, keepdims=True) + EPS)
        o_ref[:, g * LANES:(g + 1) * LANES] = (o.T * gsub_ref[...] * out_scale).astype(o_ref.dtype)


def _diff_attn_call(lam, qa, ka, vat, gsub, *, batch, seq, tq, tk, group, out_scale):
    t = qa.shape[0]
    w = group * LANES
    nq, nk = seq // tq, seq // tk
    q_spec = pl.BlockSpec((tq, w), lambda b, h, i: (b * nq + i, h))
    return pl.pallas_call(
        functools.partial(_diff_attn_body, tq=tq, tk=tk, out_scale=out_scale),
        grid=(batch, qa.shape[1] // w, nq),
        in_specs=[pl.BlockSpec(memory_space=pltpu.SMEM), q_spec,
                  pl.BlockSpec((seq, w), lambda b, h, i: (b, h)),
                  pl.BlockSpec((nk, w, tk), lambda b, h, i: (b, h, 0)), _full_spec(gsub)],
        out_specs=q_spec,
        out_shape=jax.ShapeDtypeStruct((t, qa.shape[1]), BF16),
        scratch_shapes=[pltpu.VMEM((group, nk, tk, 2 * tq), F32)],
        compiler_params=_cparams("parallel", "parallel", "arbitrary"),
        name="diff_attn",
    )(lam, qa, ka, vat, gsub)


def _key_to_float(key):
    bits = key ^ ((key >> 31) & jnp.int32(0x7FFFFFFF))
    return lax.bitcast_convert_type(bits, F32)


def _dsa_body(qi_ref, wi_ref, qb_ref, kik_ref, kb_ref, vbt_ref, tri_ref, o_ref, sc_ref, bias_ref, s_ref,
              *, tq, tk, k_sel, idx_precision):
    i = pl.program_id(1)
    q0 = i * tq
    nkc = (q0 + tq + tk - 1) // tk
    qpos = q0 + lax.broadcasted_iota(jnp.int32, (1, tq), 1)
    lane = lax.broadcasted_iota(jnp.int32, (1, LANES), 1)
    lane_lo = lane < HEAD_DIM
    tiles = tk // LANES
    neg_inf = jnp.float32(-jnp.inf)

    wi_t = wi_ref[...].T
    qi = qi_ref[...]
    q_heads = []
    for h in range(IDX_HEADS):
        pair = qi[:, (h // 2) * LANES:(h // 2 + 1) * LANES]
        q_heads.append(pair * (lane_lo if h % 2 == 0 else ~lane_lo).astype(pair.dtype))
    qi_cat = jnp.concatenate(q_heads, axis=0)
    w_heads = [wi_t[IDX_DIM + h:IDX_DIM + h + 1, :] for h in range(IDX_HEADS)]

    def score_chunk(c, carry):
        kk = kik_ref[pl.ds(pl.multiple_of(c * tk, tk), tk), :]
        s = lax.dot_general(kk, qi_cat, NT_DIMS, preferred_element_type=F32, precision=idx_precision)
        acc = jnp.zeros((tk, tq), F32)
        for h in range(IDX_HEADS):
            acc = acc + jnp.maximum(s[:, h * tq:(h + 1) * tq], 0.0) * w_heads[h]
        kpos = c * tk + lax.broadcasted_iota(jnp.int32, (tk, 1), 0)
        sc_ref[c] = jnp.where(kpos <= qpos, acc, neg_inf)
        return carry

    lax.fori_loop(0, nkc, score_chunk, 0)

    acc_rows = 4 * SUBLANES

    def count_ge(cand):
        def body(c, acc):
            hit = jnp.where(sc_ref[c] >= cand, 1.0, 0.0)
            return acc + jnp.sum(hit.reshape(tk // acc_rows, acc_rows, tq), axis=0)
        acc = lax.fori_loop(0, nkc, body, jnp.zeros((acc_rows, tq), F32))
        return jnp.sum(acc, axis=0, keepdims=True)

    ksel = float(k_sel)

    def count_at(cand_key):
        return jnp.where(cand_key <= KEY_NEG_INF, ksel, count_ge(_key_to_float(cand_key)))

    zero_key = jnp.zeros((1, tq), jnp.int32)
    thr = jnp.where(count_at(zero_key) >= ksel, zero_key, INT32_MIN)

    def bit_step(b, thr):
        cand = thr + jnp.left_shift(jnp.int32(1), 30 - b)
        return jnp.where(count_at(cand) >= ksel, cand, thr)

    thr = lax.fori_loop(0, 31, bit_step, thr)
    thr_f = _key_to_float(thr)
    above_f = _key_to_float(thr + 1)
    need = ksel - count_ge(above_f)
    tri = tri_ref[...]

    def select_chunk(c, taken):
        for t in range(tiles):
            st = sc_ref[c, t * LANES:(t + 1) * LANES, :]
            is_above = st >= above_f
            tied = jnp.where(is_above, 0.0, jnp.where(st >= thr_f, 1.0, 0.0))
            prefix = jnp.dot(tri, tied.astype(BF16), preferred_element_type=F32)
            tie_sel = jnp.where((taken + prefix) <= need, tied, 0.0)
            sel = jnp.where(is_above, 0.0, jnp.where(tie_sel > 0.0, 0.0, neg_inf))
            kpos = c * tk + t * LANES + lax.broadcasted_iota(jnp.int32, (LANES, 1), 0)
            bias_ref[c, t * LANES:(t + 1) * LANES, :] = jnp.where(kpos <= qpos, sel, neg_inf)
            taken = taken + prefix[LANES - 1:LANES, :]
        return taken

    lax.fori_loop(0, nkc, select_chunk, jnp.zeros((1, tq), F32))

    group = s_ref.shape[0]
    for p0 in range(0, qb_ref.shape[1] // LANES, group):
        pairs = range(p0, p0 + group)
        q_cat = []
        for p in pairs:
            q_pair = qb_ref[:, p * LANES:(p + 1) * LANES]
            q_cat.append(jnp.concatenate([q_pair * lane_lo.astype(BF16), q_pair * (~lane_lo).astype(BF16)],
                                         axis=0))

        def logit_chunk(c, carry, pairs=pairs, q_cat=q_cat):
            bias = bias_ref[c]
            bias2 = jnp.concatenate([bias, bias], axis=1)
            out = []
            for g, (p, m) in enumerate(zip(pairs, carry)):
                kc = kb_ref[pl.ds(pl.multiple_of(c * tk, tk), tk), p * LANES:(p + 1) * LANES]
                s = lax.dot_general(kc, q_cat[g], NT_DIMS, preferred_element_type=F32) + bias2
                s_ref[g, c] = s
                out.append(jnp.maximum(m, jnp.max(s.reshape(tk // SUBLANES, SUBLANES, 2 * tq), axis=0)))
            return tuple(out)

        m8 = lax.fori_loop(0, nkc, logit_chunk,
                           tuple(jnp.full((SUBLANES, 2 * tq), neg_inf, F32) for _ in pairs))
        shift = [jnp.max(m, axis=0, keepdims=True) for m in m8]

        def pv_chunk(c, carry, pairs=pairs, shift=shift):
            out = []
            for g, (p, (l8, acc)) in enumerate(zip(pairs, carry)):
                pe = jnp.exp2(s_ref[g, c] - shift[g])
                vt = vbt_ref[c, p * LANES:(p + 1) * LANES, :]
                acc = acc + jnp.dot(vt, pe.astype(BF16), preferred_element_type=F32)
                out.append((l8 + jnp.sum(pe.reshape(tk // SUBLANES, SUBLANES, 2 * tq), axis=0), acc))
            return tuple(out)

        fin = lax.fori_loop(0, nkc, pv_chunk, tuple((jnp.zeros((SUBLANES, 2 * tq), F32),
                                                     jnp.zeros((LANES, 2 * tq), F32)) for _ in pairs))
        for p, (l8, acc) in zip(pairs, fin):
            o_all = acc / jnp.sum(l8, axis=0, keepdims=True)
            o_t = jnp.concatenate([o_all[0:HEAD_DIM, 0:tq], o_all[HEAD_DIM:LANES, tq:2 * tq]], axis=0)
            o_ref[:, p * LANES:(p + 1) * LANES] = o_t.T.astype(o_ref.dtype)


def _dsa_call(qi, wi, qb, kik, kb, vbt, tri, *, batch, seq, tq, tk, k_sel, idx_precision):
    t = qb.shape[0]
    nq, nk = seq // tq, seq // tk
    row = lambda w: pl.BlockSpec((tq, w), lambda b, i: (b * nq + i, 0))
    seqb = lambda w: pl.BlockSpec((seq, w), lambda b, i: (b, 0))
    vbt_spec = pl.BlockSpec((nk,) + vbt.shape[1:], lambda b, i: (b, 0, 0))
    return pl.pallas_call(
        functools.partial(_dsa_body, tq=tq, tk=tk, k_sel=k_sel, idx_precision=idx_precision),
        grid=(batch, nq),
        in_specs=[row(qi.shape[1]), row(LANES), row(qb.shape[1]), seqb(LANES), seqb(kb.shape[1]),
                  vbt_spec, _full_spec(tri)],
        out_specs=row(qb.shape[1]),
        out_shape=jax.ShapeDtypeStruct((t, qb.shape[1]), BF16),
        scratch_shapes=[pltpu.VMEM((nk, tk, tq), F32), pltpu.VMEM((nk, tk, tq), F32),
                        pltpu.VMEM((2, nk, tk, 2 * tq), F32)],
        compiler_params=_cparams("parallel", "arbitrary"),
        name="dsa",
    )(qi, wi, qb, kik, kb, vbt, tri)


def _memkv_body(mem_ref, g_ref, w_ref, gk_ref, k_ref, v_ref):
    hm = _rms(mem_ref[...], g_ref[...]).astype(BF16)
    kv = jnp.dot(hm, w_ref[...], preferred_element_type=F32)
    half = kv.shape[1] // 2
    for h in range(X_HEADS):
        kh = kv[:, h * X_HEAD_DIM:(h + 1) * X_HEAD_DIM]
        k_ref[:, h * X_HEAD_DIM:(h + 1) * X_HEAD_DIM] = _rms(kh, gk_ref[...]).astype(BF16)
    v_ref[...] = kv[:, half:].astype(BF16)


def _memkv_call(mem2, g, w, gk):
    rows, d = mem2.shape
    half = w.shape[1] // 2
    tm = min(rows, 512)
    row = lambda wd: pl.BlockSpec((tm, wd), lambda i: (i, 0))
    return pl.pallas_call(
        _memkv_body,
        grid=(rows // tm,),
        in_specs=[row(d), _full_spec(g), _full_spec(w), _full_spec(gk)],
        out_specs=[row(half), row(half)],
        out_shape=[jax.ShapeDtypeStruct((rows, half), BF16)] * 2,
        compiler_params=_cparams("parallel"),
        name="memkv",
    )(mem2, g, w, gk)


def _xattn_tail(x1, gx_ref, wq_ref, gq_ref, kx_ref, vx_ref, wo_ref):
    hx = _rms(x1, gx_ref[...]).astype(BF16)
    q = jnp.dot(hx, wq_ref[...], preferred_element_type=F32)
    outs = []
    for h in range(X_HEADS):
        cols = slice(h * X_HEAD_DIM, (h + 1) * X_HEAD_DIM)
        qh = _rms(q[:, cols], gq_ref[...]).astype(BF16)
        s = lax.dot_general(qh, kx_ref[0, :, cols], NT_DIMS, preferred_element_type=F32)
        p = jnp.exp(s - jnp.max(s, axis=-1, keepdims=True))
        o = (jnp.dot(p.astype(BF16), vx_ref[0, :, cols], preferred_element_type=F32)
             / jnp.sum(p, axis=-1, keepdims=True))
        outs.append(o.astype(BF16))
    o = jnp.concatenate(outs, axis=1)
    return x1 + jnp.dot(o, wo_ref[...], preferred_element_type=F32)


def _out0_body(x_ref, oa_ref, ob_ref, woa_ref, wob_ref, gx_ref, wq_ref, gq_ref, kx_ref, vx_ref, wo_ref,
               o_ref):
    mix = (jnp.dot(oa_ref[...], woa_ref[...], preferred_element_type=F32)
           + jnp.dot(ob_ref[...], wob_ref[...], preferred_element_type=F32))
    o_ref[...] = _xattn_tail(x_ref[...] + mix, gx_ref, wq_ref, gq_ref, kx_ref, vx_ref, wo_ref)


def _out0_call(x2, oa, ob, woa, wob, gx, wq, gq, kx, vx, wo, *, seq, tm):
    t, d = x2.shape
    row = lambda w: pl.BlockSpec((tm, w), lambda i: (i, 0))
    mem_spec = pl.BlockSpec((1,) + kx.shape[1:], lambda i: ((i * tm) // seq, 0, 0))
    fulls = [woa, wob, gx, wq, gq]
    return pl.pallas_call(
        _out0_body,
        grid=(t // tm,),
        in_specs=[row(d), row(oa.shape[1]), row(ob.shape[1])] + [_full_spec(a) for a in fulls]
                 + [mem_spec, mem_spec, _full_spec(wo)],
        out_specs=row(d),
        out_shape=jax.ShapeDtypeStruct((t, d), F32),
        compiler_params=_cparams("parallel"),
        name="out0",
    )(x2, oa, ob, woa, wob, gx, wq, gq, kx, vx, wo)


def _mlp_body(x_ref, g_ref, wup_ref, wdn_ref, o_ref, h_ref, acc_ref):
    j = pl.program_id(1)

    @pl.when(j == 0)
    def _():
        h_ref[...] = _rms(x_ref[...], g_ref[...]).astype(BF16)
        acc_ref[...] = jnp.zeros(acc_ref.shape, F32)

    u = jnp.dot(h_ref[...], wup_ref[...], preferred_element_type=F32)
    a = jnp.square(jnp.maximum(u, 0.0)).astype(BF16)
    acc_ref[...] += jnp.dot(a, wdn_ref[...], preferred_element_type=F32)

    @pl.when(j == pl.num_programs(1) - 1)
    def _():
        o_ref[...] = x_ref[...] + acc_ref[...]


def _mlp_call(x2, g, wup, wdn, layer, *, tm, tf):
    t, d = x2.shape
    ff = wup.shape[2]
    return pl.pallas_call(
        _mlp_body,
        grid=(t // tm, ff // tf),
        in_specs=[pl.BlockSpec((tm, d), lambda i, j: (i, 0)), _full_spec(g),
                  pl.BlockSpec((None, d, tf), lambda i, j: (layer, 0, j)),
                  pl.BlockSpec((None, tf, d), lambda i, j: (layer, j, 0))],
        out_specs=pl.BlockSpec((tm, d), lambda i, j: (i, 0)),
        out_shape=jax.ShapeDtypeStruct((t, d), F32),
        scratch_shapes=[pltpu.VMEM((tm, d), BF16), pltpu.VMEM((tm, d), F32)],
        compiler_params=_cparams("parallel", "arbitrary"),
        name="mlp",
    )(x2, g, wup, wdn)


def _mix1_body(x_ref, xh_ref, gmix_ref, win_ref, cw_ref, wpool_ref, ps_ref, wout_ref,
               gx_ref, wq_ref, gq_ref, kx_ref, vx_ref, wo_ref, o_ref, *, tm, seq):
    i = pl.program_id(0)
    x = x_ref[...]
    xf = jnp.concatenate([xh_ref[...], x], axis=0)
    hf = _rms(xf, gmix_ref[...]).astype(BF16)
    y = jnp.dot(hf, win_ref[...], preferred_element_type=F32)
    cw = y.shape[1] // 4
    pos0 = (i * tm) % seq
    rid = lax.broadcasted_iota(jnp.int32, (tm + POOL_HALO, 1), 0)
    keep = jnp.where(rid >= POOL_HALO, 1.0, jnp.where(pos0 > 0, 1.0, 0.0))

    gb = y[POOL_HALO:, 0:cw]
    u = y[:, cw:2 * cw] * y[:, 2 * cw:3 * cw] * keep
    z = y[:, 3 * cw:4 * cw] * keep

    taps = cw_ref[...]
    conv = taps[CONV_W - 1:CONV_W, :] * u[POOL_HALO:]
    for back in range(1, CONV_W):
        conv = conv + taps[CONV_W - 1 - back:CONV_W - back, :] * pltpu.roll(u, back, axis=0)[POOL_HALO:]
    yc = gb * conv

    pos = pos0 + lax.broadcasted_iota(jnp.int32, (tm, 1), 0)
    gw = cw // len(POOL_WINDOWS)
    parts = []
    for g, w in enumerate(POOL_WINDOWS):
        zg = z[:, g * gw:(g + 1) * gw]
        s = zg
        sh = 1
        while sh < w:
            s = s + pltpu.roll(s, sh, axis=0)
            sh *= 2
        cnt = jnp.minimum(pos + 1, w).astype(F32)
        pooled = s[POOL_HALO:] / cnt - zg[POOL_HALO:]
        parts.append(jnp.dot(pooled.astype(BF16), wpool_ref[g], preferred_element_type=F32))
    yd = jnp.concatenate(parts, axis=1) * ps_ref[...]

    mix = (jnp.dot(yc.astype(BF16), wout_ref[0:cw, :], preferred_element_type=F32)
           + jnp.dot(yd.astype(BF16), wout_ref[cw:2 * cw, :], preferred_element_type=F32))
    o_ref[...] = _xattn_tail(x + mix, gx_ref, wq_ref, gq_ref, kx_ref, vx_ref, wo_ref)


def _mix1_call(x2, gmix, win, cw, wpool, ps, wout, gx, wq, gq, kx, vx, wo, *, seq, tm):
    t, d = x2.shape
    row = pl.BlockSpec((tm, d), lambda i: (i, 0))
    halo = pl.BlockSpec((POOL_HALO, d), lambda i: (jnp.maximum(i * (tm // POOL_HALO) - 1, 0), 0))
    mem_spec = pl.BlockSpec((1,) + kx.shape[1:], lambda i: ((i * tm) // seq, 0, 0))
    fulls = [gmix, win, cw, wpool, ps, wout, gx, wq, gq]
    return pl.pallas_call(
        functools.partial(_mix1_body, tm=tm, seq=seq),
        grid=(t // tm,),
        in_specs=[row, halo] + [_full_spec(a) for a in fulls] + [mem_spec, mem_spec, _full_spec(wo)],
        out_specs=row,
        out_shape=jax.ShapeDtypeStruct((t, d), F32),
        compiler_params=_cparams("parallel"),
        name="mix1",
    )(x2, x2, gmix, win, cw, wpool, ps, wout, gx, wq, gq, kx, vx, wo)


def _row(v, reps=1):
    return jnp.tile(v.astype(F32), reps).reshape(1, -1)


def kernel(x, mem, positions, g_mix, g_xattn, g_mem, g_mlp, wq_x, wkv_x, gq_x, gk_x, wo_x, w_up, w_down,
           w_in_e, gq_a, gk_a, lam_q1, lam_k1, lam_q2, lam_k2, g_sub_a, g_kv_b, w_kv_up_b, gq_b, gk_b,
           g_kidx, w_out_e, w_in_o, conv_w, w_pool, pool_scale, w_out_o):
    batch, seq, d = x.shape
    t = batch * seq
    k_sel = min(TOPK_MAX, seq // 4)
    tm = min(1024, seq)
    tq_b, tk_b = min(256, seq), min(512, seq)
    assert seq % tm == 0 and seq % tq_b == 0 and seq % tk_b == 0 and tk_b >= k_sel

    x2 = x.reshape(t, d)
    pos = positions.reshape(t // tm, 1, tm)
    inv_freq = (ROPE_THETA ** (-jnp.arange(0, ROT_DIM, 2, dtype=F32) / ROT_DIM)).reshape(ROT_HALF, 1)
    mem2 = mem.reshape(batch * mem.shape[1], d)
    depth = g_mix.shape[0]
    w_up_b, w_down_b = w_up.astype(BF16), w_down.astype(BF16)

    for layer in range(depth):
        j = layer // 2
        kx, vx = _memkv_call(mem2, _row(g_mem[layer]), wkv_x[layer].astype(BF16), _row(gk_x[layer]))
        kx = kx.reshape(batch, mem.shape[1], -1)
        vx = vx.reshape(batch, mem.shape[1], -1)
        tail = (_row(g_xattn[layer]), wq_x[layer].astype(BF16), _row(gq_x[layer]) * (X_HEAD_DIM ** -0.5),
                kx, vx, wo_x[layer].astype(BF16))
        if layer % 2 == 0:
            lam_init = 0.8 - 0.6 * float(np.exp(-0.3 * layer))
            lam = (jnp.exp(jnp.sum(lam_q1[j].astype(F32) * lam_k1[j].astype(F32)))
                   - jnp.exp(jnp.sum(lam_q2[j].astype(F32) * lam_k2[j].astype(F32))) + lam_init)
            w_in = w_in_e[j]
            n_main = 2304
            n_idx = w_in.shape[1] - n_main
            widx = jnp.pad(w_in[:, n_main:], ((0, 0), (0, 3 * LANES - n_idx)))
            if IDX_PRECISION is None:
                widx = widx.astype(BF16)
            heads8 = 512 // HEAD_DIM
            bd = jnp.kron(jnp.eye(heads8, dtype=F32), jnp.ones((HEAD_DIM, HEAD_DIM), F32)).astype(BF16)
            gkidx = jnp.pad(g_kidx[j].astype(F32), (0, LANES - IDX_DIM)).reshape(1, LANES)
            qa, ka, vat, qb, kb, vbt, qi, kik, wi = _in0_call(
                x2, _row(g_mix[layer]), w_in[:, :n_main].astype(BF16), widx, w_kv_up_b[j].astype(BF16),
                pos, inv_freq, bd, _row(gq_a[j], heads8), _row(gk_a[j], heads8), _row(gq_b[j], heads8),
                _row(gk_b[j], heads8), _row(g_kv_b[j]), gkidx, tm=tm, tk=tk_b, idx_precision=IDX_PRECISION)
            out_a = _diff_attn_call(lam.reshape(1).astype(F32), qa, ka, vat, _row(g_sub_a[j]),
                                    batch=batch, seq=seq, tq=tq_b, tk=tk_b, group=2, out_scale=1.0 - lam_init)
            tri = jnp.tril(jnp.ones((LANES, LANES), F32)).astype(BF16)
            out_b = _dsa_call(qi, wi, qb, kik, kb, vbt, tri, batch=batch, seq=seq, tq=tq_b, tk=tk_b,
                              k_sel=k_sel, idx_precision=IDX_PRECISION)
            w_out = w_out_e[j].astype(BF16)
            x2 = _out0_call(x2, out_a, out_b, w_out[:512], w_out[512:], *tail, seq=seq, tm=tm)
        else:
            taps = jnp.pad(conv_w[j].astype(F32), ((0, SUBLANES - CONV_W), (0, 0)))
            x2 = _mix1_call(x2, _row(g_mix[layer]), w_in_o[j].astype(BF16), taps, w_pool[j].astype(BF16),
                            _row(pool_scale[j]), w_out_o[j].astype(BF16), *tail, seq=seq, tm=tm)
        x2 = _mlp_call(x2, _row(g_mlp[layer]), w_up_b, w_down_b, layer, tm=min(1024, t), tf=512)
    return x2.reshape(batch, seq, d)
```

```python
import functools

import numpy as np
import jax
import jax.numpy as jnp
from jax import lax
from jax.experimental import pallas as pl
from jax.experimental.pallas import tpu as pltpu

F32 = jnp.float32
BF16 = jnp.bfloat16

HEAD_DIM = 64
ROT_DIM = HEAD_DIM // 4
ROT_HALF = ROT_DIM // 2
ROPE_THETA = 500000.0
IDX_HEADS = 4
IDX_DIM = 64
TOPK_MAX = 256
CONV_W = 3
POOL_WINDOWS = (2, 4, 8, 16)
POOL_HALO = 16
X_HEADS = 4
X_HEAD_DIM = 128
EPS = 1e-6
LOG2_E = 1.4426950408889634

LANES = 128
SUBLANES = 8
VMEM_LIMIT_BYTES = 56 * 1024 * 1024

KEY_NEG_INF = np.int32(-2139095041)
INT32_MIN = np.int32(-(2 ** 31))

NT_DIMS = (((1,), (1,)), ((), ()))

IDX_PRECISION = None


def _cparams(*sem):
    return pltpu.CompilerParams(dimension_semantics=sem, vmem_limit_bytes=VMEM_LIMIT_BYTES)


def _full_spec(arr):
    nd = arr.ndim
    return pl.BlockSpec(arr.shape, lambda *_: (0,) * nd)


def _rms(x, g):
    ms = jnp.mean(x * x, axis=-1, keepdims=True)
    return x * lax.rsqrt(ms + EPS) * g


def _tile_lanes(t, width):
    reps = width // t.shape[-1]
    return t if reps == 1 else jnp.concatenate([t] * reps, axis=1)


def _rope_tables(pos_row, inv_col):
    ang = inv_col * pos_row
    cos, sin = jnp.cos(ang), jnp.sin(ang)
    rest = (HEAD_DIM - ROT_DIM, ang.shape[1])
    head_c = jnp.concatenate([cos, cos, jnp.ones(rest, F32)], axis=0)
    head_s = jnp.concatenate([-sin, sin, jnp.zeros(rest, F32)], axis=0)
    return (jnp.concatenate([head_c, head_c], axis=0).T, jnp.concatenate([head_s, head_s], axis=0).T)


def _rope(y, c, s):
    w = y.shape[-1]
    c, s = _tile_lanes(c, w), _tile_lanes(s, w)
    lane = lax.broadcasted_iota(jnp.int32, (1, w), 1)
    upper = pltpu.roll(y, w - ROT_HALF, axis=1)
    lower = pltpu.roll(y, ROT_HALF, axis=1)
    partner = jnp.where((lane & (HEAD_DIM - 1)) < ROT_HALF, upper, lower)
    return y * c + partner * s


def _headnorm64(y, blockdiag, g):
    ss = jnp.dot((y * y).astype(BF16), blockdiag, preferred_element_type=F32)
    return y * lax.rsqrt(ss * (1.0 / HEAD_DIM) + EPS) * g


def _in0_body(x_ref, gmix_ref, wmain_ref, widx_ref, wkvup_ref, pos_ref, inv_ref,
              bd_ref, gqa_ref, gka_ref, gqb_ref, gkb_ref, gkv_ref, gkidx_ref,
              qa_ref, ka_ref, vat_ref, qb_ref, kb_ref, vbt_ref, qi_ref, kik_ref, wi_ref,
              *, idx_precision):
    h = _rms(x_ref[...], gmix_ref[...])
    hb = h.astype(BF16)
    proj = lambda lo, hi: jnp.dot(hb, wmain_ref[:, lo:hi], preferred_element_type=F32)
    rot = _rope_tables(pos_ref[0].astype(F32), inv_ref[...])
    bd = bd_ref[...]
    scale = HEAD_DIM ** -0.5 * LOG2_E

    qa = _rope(_headnorm64(proj(0, 512), bd, gqa_ref[...]), *rot) * scale
    qa_ref[...] = qa.astype(BF16)
    ka = _rope(_headnorm64(proj(512, 1024), bd, gka_ref[...]), *rot)
    ka_ref[...] = ka.astype(BF16)
    tk = vat_ref.shape[2]
    va = proj(1024, 1536)
    for r in range(vat_ref.shape[0]):
        vat_ref[r] = va[r * tk:(r + 1) * tk, :].T.astype(BF16)
    qb = _rope(_headnorm64(proj(1536, 2048), bd, gqb_ref[...]), *rot) * scale
    qb_ref[...] = qb.astype(BF16)

    ckv = _rms(proj(2048, 2304), gkv_ref[...]).astype(BF16)
    kv = jnp.dot(ckv, wkvup_ref[...], preferred_element_type=F32)
    kb = _rope(_headnorm64(kv[:, 0:512], bd, gkb_ref[...]), *rot)
    kb_ref[...] = kb.astype(BF16)
    for r in range(vbt_ref.shape[0]):
        vbt_ref[r] = kv[r * tk:(r + 1) * tk, 512:1024].T.astype(BF16)

    if idx_precision is None:
        yi = jnp.dot(hb, widx_ref[...], preferred_element_type=F32)
    else:
        yi = jnp.dot(h, widx_ref[...], preferred_element_type=F32, precision=idx_precision)
    qi_ref[...] = _rope(yi[:, 0:256], *rot).astype(qi_ref.dtype)
    blk = yi[:, 256:384]
    lane = lax.broadcasted_iota(jnp.int32, blk.shape, 1)
    ms = jnp.sum(jnp.where(lane < IDX_DIM, blk * blk, 0.0), axis=-1, keepdims=True) * (1.0 / IDX_DIM)
    kin = blk * lax.rsqrt(ms + EPS) * gkidx_ref[...]
    kir = _rope(kin, *rot)
    kik = kir + pltpu.roll(kir, IDX_DIM, axis=1)
    kik_ref[...] = kik.astype(kik_ref.dtype)
    wi_ref[...] = blk * (IDX_HEADS ** -0.5 * IDX_DIM ** -0.5)


def _in0_call(x2, gmix, wmain, widx, wkvup, pos, inv, bd, gqa, gka, gqb, gkb, gkv, gkidx,
              *, tm, tk, idx_precision):
    t, d = x2.shape
    row = lambda w: pl.BlockSpec((tm, w), lambda i: (i, 0))
    fulls = [gmix, wmain, widx, wkvup]
    gains = [inv, bd, gqa, gka, gqb, gkb, gkv, gkidx]
    in_specs = ([row(d)] + [_full_spec(a) for a in fulls] + [pl.BlockSpec((1, 1, tm), lambda i: (i, 0, 0))]
                + [_full_spec(a) for a in gains])
    idx_dt = BF16 if idx_precision is None else F32
    sds = jax.ShapeDtypeStruct
    vt_shape, vt_spec = sds((t // tk, 512, tk), BF16), pl.BlockSpec((tm // tk, 512, tk), lambda i: (i, 0, 0))
    out_shape = ([sds((t, 512), BF16)] * 2 + [vt_shape] + [sds((t, 512), BF16)] * 2
                 + [vt_shape, sds((t, 256), idx_dt), sds((t, LANES), idx_dt), sds((t, LANES), F32)])
    out_specs = [row(512)] * 2 + [vt_spec] + [row(512)] * 2 + [vt_spec, row(256), row(LANES), row(LANES)]
    return pl.pallas_call(
        functools.partial(_in0_body, idx_precision=idx_precision),
        grid=(t // tm,),
        in_specs=in_specs,
        out_specs=out_specs,
        out_shape=out_shape,
        compiler_params=_cparams("parallel"),
        name="in0",
    )(x2, gmix, wmain, widx, wkvup, pos, inv, bd, gqa, gka, gqb, gkb, gkv, gkidx)


def _diff_attn_body(lam_ref, q_ref, k_ref, vt_ref, gsub_ref, o_ref, s_ref, *, tq, tk, out_scale):
    i = pl.program_id(2)
    q0 = i * tq
    n_full = (q0 + 1) // tk
    nkc = (q0 + tq + tk - 1) // tk
    lane = lax.broadcasted_iota(jnp.int32, (1, LANES), 1)
    lane_lo = lane < HEAD_DIM
    qpos = q0 + lax.broadcasted_iota(jnp.int32, (1, tq), 1)
    qpos2 = jnp.concatenate([qpos, qpos], axis=1)
    neg_inf = jnp.float32(-jnp.inf)
    heads = range(q_ref.shape[1] // LANES)
    q_cat = []
    for g in heads:
        q_h = q_ref[:, g * LANES:(g + 1) * LANES]
        q_cat.append(jnp.concatenate([q_h * lane_lo.astype(BF16), q_h * (~lane_lo).astype(BF16)], axis=0))

    def logit_chunk(masked):
        def body(c, carry):
            rows = pl.ds(pl.multiple_of(c * tk, tk), tk)
            out = []
            for g, m in zip(heads, carry):
                s = lax.dot_general(k_ref[rows, g * LANES:(g + 1) * LANES], q_cat[g], NT_DIMS,
                                    preferred_element_type=F32)
                if masked:
                    kpos = c * tk + lax.broadcasted_iota(jnp.int32, (tk, 1), 0)
                    s = jnp.where(kpos <= qpos2, s, neg_inf)
                s_ref[g, c] = s
                out.append(jnp.maximum(m, jnp.max(s.reshape(tk // SUBLANES, SUBLANES, 2 * tq), axis=0)))
            return tuple(out)
        return body

    m8 = tuple(jnp.full((SUBLANES, 2 * tq), neg_inf, F32) for _ in heads)
    m8 = lax.fori_loop(0, n_full, logit_chunk(False), m8)
    m8 = lax.fori_loop(n_full, nkc, logit_chunk(True), m8)
    shift = [jnp.max(m, axis=0, keepdims=True) for m in m8]

    def pv_chunk(c, carry):
        out = []
        for g, (l8, acc) in zip(heads, carry):
            pe = jnp.exp2(s_ref[g, c] - shift[g])
            acc = acc + jnp.dot(vt_ref[c, g * LANES:(g + 1) * LANES, :], pe.astype(BF16),
                                preferred_element_type=F32)
            out.append((l8 + jnp.sum(pe.reshape(tk // SUBLANES, SUBLANES, 2 * tq), axis=0), acc))
        return tuple(out)

    fin = lax.fori_loop(0, nkc, pv_chunk, tuple((jnp.zeros((SUBLANES, 2 * tq), F32),
                                                 jnp.zeros((LANES, 2 * tq), F32)) for _ in heads))
    lam = lam_ref[0]
    for g, (l8, acc) in zip(heads, fin):
        o_all = acc / jnp.sum(l8, axis=0, keepdims=True)
        o = o_all[:, 0:tq] - lam * o_all[:, tq:2 * tq]
        o = o * lax.rsqrt(jnp.mean(o * o, axis=0, keepdims=True) + EPS)
        o_ref[:, g * LANES:(g + 1) * LANES] = (o.T * gsub_ref[...] * out_scale).astype(o_ref.dtype)


def _diff_attn_call(lam, qa, ka, vat, gsub, *, batch, seq, tq, tk, group, out_scale):
    t = qa.shape[0]
    w = group * LANES
    nq, nk = seq // tq, seq // tk
    q_spec = pl.BlockSpec((tq, w), lambda b, h, i: (b * nq + i, h))
    return pl.pallas_call(
        functools.partial(_diff_attn_body, tq=tq, tk=tk, out_scale=out_scale),
        grid=(batch, qa.shape[1] // w, nq),
        in_specs=[pl.BlockSpec(memory_space=pltpu.SMEM), q_spec,
                  pl.BlockSpec((seq, w), lambda b, h, i: (b, h)),
                  pl.BlockSpec((nk, w, tk), lambda b, h, i: (b, h, 0)), _full_spec(gsub)],
        out_specs=q_spec,
        out_shape=jax.ShapeDtypeStruct((t, qa.shape[1]), BF16),
        scratch_shapes=[pltpu.VMEM((group, nk, tk, 2 * tq), F32)],
        compiler_params=_cparams("parallel", "parallel", "arbitrary"),
        name="diff_attn",
    )(lam, qa, ka, vat, gsub)


def _key_to_float(key):
    bits = key ^ ((key >> 31) & jnp.int32(0x7FFFFFFF))
    return lax.bitcast_convert_type(bits, F32)


def _dsa_body(qi_ref, wi_ref, qb_ref, kik_ref, kb_ref, vbt_ref, tri_ref, o_ref, sc_ref, bias_ref, s_ref,
              *, tq, tk, k_sel, idx_precision):
    i = pl.program_id(1)
    q0 = i * tq
    nkc = (q0 + tq + tk - 1) // tk
    qpos = q0 + lax.broadcasted_iota(jnp.int32, (1, tq), 1)
    lane = lax.broadcasted_iota(jnp.int32, (1, LANES), 1)
    lane_lo = lane < HEAD_DIM
    tiles = tk // LANES
    neg_inf = jnp.float32(-jnp.inf)

    wi_t = wi_ref[...].T
    qi = qi_ref[...]
    q_heads = []
    for h in range(IDX_HEADS):
        pair = qi[:, (h // 2) * LANES:(h // 2 + 1) * LANES]
        q_heads.append(pair * (lane_lo if h % 2 == 0 else ~lane_lo).astype(pair.dtype))
    qi_cat = jnp.concatenate(q_heads, axis=0)
    w_heads = [wi_t[IDX_DIM + h:IDX_DIM + h + 1, :] for h in range(IDX_HEADS)]

    def score_chunk(c, carry):
        kk = kik_ref[pl.ds(pl.multiple_of(c * tk, tk), tk), :]
        s = lax.dot_general(kk, qi_cat, NT_DIMS, preferred_element_type=F32, precision=idx_precision)
        acc = jnp.zeros((tk, tq), F32)
        for h in range(IDX_HEADS):
            acc = acc + jnp.maximum(s[:, h * tq:(h + 1) * tq], 0.0) * w_heads[h]
        kpos = c * tk + lax.broadcasted_iota(jnp.int32, (tk, 1), 0)
        sc_ref[c] = jnp.where(kpos <= qpos, acc, neg_inf)
        return carry

    lax.fori_loop(0, nkc, score_chunk, 0)

    acc_rows = 4 * SUBLANES

    def count_ge(cand):
        def body(c, acc):
            for r in range(tk // acc_rows):
                acc = acc + jnp.where(sc_ref[c, r * acc_rows:(r + 1) * acc_rows, :] >= cand, 1.0, 0.0)
            return acc
        acc = lax.fori_loop(0, nkc, body, jnp.zeros((acc_rows, tq), F32))
        return jnp.sum(acc, axis=0, keepdims=True)

    ksel = float(k_sel)

    def count_at(cand_key):
        return jnp.where(cand_key <= KEY_NEG_INF, ksel, count_ge(_key_to_float(cand_key)))

    zero_key = jnp.zeros((1, tq), jnp.int32)
    thr = jnp.where(count_at(zero_key) >= ksel, zero_key, INT32_MIN)

    def bit_step(b, thr):
        cand = thr + jnp.left_shift(jnp.int32(1), 30 - b)
        return jnp.where(count_at(cand) >= ksel, cand, thr)

    thr = lax.fori_loop(0, 31, bit_step, thr)
    thr_f = _key_to_float(thr)
    above_f = _key_to_float(thr + 1)
    need = ksel - count_ge(above_f)
    tri = tri_ref[...]

    def select_chunk(c, taken):
        for t in range(tiles):
            st = sc_ref[c, t * LANES:(t + 1) * LANES, :]
            is_above = st >= above_f
            tied = jnp.where(is_above, 0.0, jnp.where(st >= thr_f, 1.0, 0.0))
            prefix = jnp.dot(tri, tied.astype(BF16), preferred_element_type=F32)
            tie_sel = jnp.where((taken + prefix) <= need, tied, 0.0)
            sel = jnp.where(is_above, 0.0, jnp.where(tie_sel > 0.0, 0.0, neg_inf))
            kpos = c * tk + t * LANES + lax.broadcasted_iota(jnp.int32, (LANES, 1), 0)
            bias_ref[c, t * LANES:(t + 1) * LANES, :] = jnp.where(kpos <= qpos, sel, neg_inf)
            taken = taken + prefix[LANES - 1:LANES, :]
        return taken

    lax.fori_loop(0, nkc, select_chunk, jnp.zeros((1, tq), F32))

    group = s_ref.shape[0]
    for p0 in range(0, qb_ref.shape[1] // LANES, group):
        pairs = range(p0, p0 + group)
        q_cat = []
        for p in pairs:
            q_pair = qb_ref[:, p * LANES:(p + 1) * LANES]
            q_cat.append(jnp.concatenate([q_pair * lane_lo.astype(BF16), q_pair * (~lane_lo).astype(BF16)],
                                         axis=0))

        def logit_chunk(c, carry, pairs=pairs, q_cat=q_cat):
            bias = bias_ref[c]
            bias2 = jnp.concatenate([bias, bias], axis=1)
            out = []
            for g, (p, m) in enumerate(zip(pairs, carry)):
                kc = kb_ref[pl.ds(pl.multiple_of(c * tk, tk), tk), p * LANES:(p + 1) * LANES]
                s = lax.dot_general(kc, q_cat[g], NT_DIMS, preferred_element_type=F32) + bias2
                s_ref[g, c] = s
                out.append(jnp.maximum(m, jnp.max(s.reshape(tk // SUBLANES, SUBLANES, 2 * tq), axis=0)))
            return tuple(out)

        m8 = lax.fori_loop(0, nkc, logit_chunk,
                           tuple(jnp.full((SUBLANES, 2 * tq), neg_inf, F32) for _ in pairs))
        shift = [jnp.max(m, axis=0, keepdims=True) for m in m8]

        def pv_chunk(c, carry, pairs=pairs, shift=shift):
            out = []
            for g, (p, (l8, acc)) in enumerate(zip(pairs, carry)):
                pe = jnp.exp2(s_ref[g, c] - shift[g])
                vt = vbt_ref[c, p * LANES:(p + 1) * LANES, :]
                acc = acc + jnp.dot(vt, pe.astype(BF16), preferred_element_type=F32)
                out.append((l8 + jnp.sum(pe.reshape(tk // SUBLANES, SUBLANES, 2 * tq), axis=0), acc))
            return tuple(out)

        fin = lax.fori_loop(0, nkc, pv_chunk, tuple((jnp.zeros((SUBLANES, 2 * tq), F32),
                                                     jnp.zeros((LANES, 2 * tq), F32)) for _ in pairs))
        for p, (l8, acc) in zip(pairs, fin):
            o_all = acc / jnp.sum(l8, axis=0, keepdims=True)
            o_t = jnp.concatenate([o_all[0:HEAD_DIM, 0:tq], o_all[HEAD_DIM:LANES, tq:2 * tq]], axis=0)
            o_ref[:, p * LANES:(p + 1) * LANES] = o_t.T.astype(o_ref.dtype)


def _dsa_call(qi, wi, qb, kik, kb, vbt, tri, *, batch, seq, tq, tk, k_sel, idx_precision):
    t = qb.shape[0]
    nq, nk = seq // tq, seq // tk
    row = lambda w: pl.BlockSpec((tq, w), lambda b, i: (b * nq + i, 0))
    seqb = lambda w: pl.BlockSpec((seq, w), lambda b, i: (b, 0))
    vbt_spec = pl.BlockSpec((nk,) + vbt.shape[1:], lambda b, i: (b, 0, 0))
    return pl.pallas_call(
        functools.partial(_dsa_body, tq=tq, tk=tk, k_sel=k_sel, idx_precision=idx_precision),
        grid=(batch, nq),
        in_specs=[row(qi.shape[1]), row(LANES), row(qb.shape[1]), seqb(LANES), seqb(kb.shape[1]),
                  vbt_spec, _full_spec(tri)],
        out_specs=row(qb.shape[1]),
        out_shape=jax.ShapeDtypeStruct((t, qb.shape[1]), BF16),
        scratch_shapes=[pltpu.VMEM((nk, tk, tq), F32), pltpu.VMEM((nk, tk, tq), F32),
                        pltpu.VMEM((2, nk, tk, 2 * tq), F32)],
        compiler_params=_cparams("parallel", "arbitrary"),
        name="dsa",
    )(qi, wi, qb, kik, kb, vbt, tri)


def _memkv_body(mem_ref, g_ref, w_ref, gk_ref, k_ref, v_ref):
    hm = _rms(mem_ref[...], g_ref[...]).astype(BF16)
    kv = jnp.dot(hm, w_ref[...], preferred_element_type=F32)
    half = kv.shape[1] // 2
    for h in range(X_HEADS):
        kh = kv[:, h * X_HEAD_DIM:(h + 1) * X_HEAD_DIM]
        k_ref[:, h * X_HEAD_DIM:(h + 1) * X_HEAD_DIM] = _rms(kh, gk_ref[...]).astype(BF16)
    v_ref[...] = kv[:, half:].astype(BF16)


def _memkv_call(mem2, g, w, gk):
    rows, d = mem2.shape
    half = w.shape[1] // 2
    tm = min(rows, 512)
    row = lambda wd: pl.BlockSpec((tm, wd), lambda i: (i, 0))
    return pl.pallas_call(
        _memkv_body,
        grid=(rows // tm,),
        in_specs=[row(d), _full_spec(g), _full_spec(w), _full_spec(gk)],
        out_specs=[row(half), row(half)],
        out_shape=[jax.ShapeDtypeStruct((rows, half), BF16)] * 2,
        compiler_params=_cparams("parallel"),
        name="memkv",
    )(mem2, g, w, gk)


def _xattn_tail(x1, gx_ref, wq_ref, gq_ref, kx_ref, vx_ref, wo_ref):
    hx = _rms(x1, gx_ref[...]).astype(BF16)
    q = jnp.dot(hx, wq_ref[...], preferred_element_type=F32)
    outs = []
    for h in range(X_HEADS):
        cols = slice(h * X_HEAD_DIM, (h + 1) * X_HEAD_DIM)
        qh = _rms(q[:, cols], gq_ref[...]).astype(BF16)
        s = lax.dot_general(qh, kx_ref[0, :, cols], NT_DIMS, preferred_element_type=F32)
        p = jnp.exp(s - jnp.max(s, axis=-1, keepdims=True))
        o = (jnp.dot(p.astype(BF16), vx_ref[0, :, cols], preferred_element_type=F32)
             / jnp.sum(p, axis=-1, keepdims=True))
        outs.append(o.astype(BF16))
    o = jnp.concatenate(outs, axis=1)
    return x1 + jnp.dot(o, wo_ref[...], preferred_element_type=F32)


def _out0_body(x_ref, oa_ref, ob_ref, woa_ref, wob_ref, gx_ref, wq_ref, gq_ref, kx_ref, vx_ref, wo_ref,
               o_ref):
    mix = (jnp.dot(oa_ref[...], woa_ref[...], preferred_element_type=F32)
           + jnp.dot(ob_ref[...], wob_ref[...], preferred_element_type=F32))
    o_ref[...] = _xattn_tail(x_ref[...] + mix, gx_ref, wq_ref, gq_ref, kx_ref, vx_ref, wo_ref)


def _out0_call(x2, oa, ob, woa, wob, gx, wq, gq, kx, vx, wo, *, seq, tm):
    t, d = x2.shape
    row = lambda w: pl.BlockSpec((tm, w), lambda i: (i, 0))
    mem_spec = pl.BlockSpec((1,) + kx.shape[1:], lambda i: ((i * tm) // seq, 0, 0))
    fulls = [woa, wob, gx, wq, gq]
    return pl.pallas_call(
        _out0_body,
        grid=(t // tm,),
        in_specs=[row(d), row(oa.shape[1]), row(ob.shape[1])] + [_full_spec(a) for a in fulls]
                 + [mem_spec, mem_spec, _full_spec(wo)],
        out_specs=row(d),
        out_shape=jax.ShapeDtypeStruct((t, d), F32),
        compiler_params=_cparams("parallel"),
        name="out0",
    )(x2, oa, ob, woa, wob, gx, wq, gq, kx, vx, wo)


def _mlp_body(x_ref, g_ref, wup_ref, wdn_ref, o_ref, h_ref, acc_ref):
    j = pl.program_id(1)

    @pl.when(j == 0)
    def _():
        h_ref[...] = _rms(x_ref[...], g_ref[...]).astype(BF16)
        acc_ref[...] = jnp.zeros(acc_ref.shape, F32)

    u = jnp.dot(h_ref[...], wup_ref[...], preferred_element_type=F32)
    a = jnp.square(jnp.maximum(u, 0.0)).astype(BF16)
    acc_ref[...] += jnp.dot(a, wdn_ref[...], preferred_element_type=F32)

    @pl.when(j == pl.num_programs(1) - 1)
    def _():
        o_ref[...] = x_ref[...] + acc_ref[...]


def _mlp_call(x2, g, wup, wdn, layer, *, tm, tf):
    t, d = x2.shape
    ff = wup.shape[2]
    return pl.pallas_call(
        _mlp_body,
        grid=(t // tm, ff // tf),
        in_specs=[pl.BlockSpec((tm, d), lambda i, j: (i, 0)), _full_spec(g),
                  pl.BlockSpec((None, d, tf), lambda i, j: (layer, 0, j)),
                  pl.BlockSpec((None, tf, d), lambda i, j: (layer, j, 0))],
        out_specs=pl.BlockSpec((tm, d), lambda i, j: (i, 0)),
        out_shape=jax.ShapeDtypeStruct((t, d), F32),
        scratch_shapes=[pltpu.VMEM((tm, d), BF16), pltpu.VMEM((tm, d), F32)],
        compiler_params=_cparams("parallel", "arbitrary"),
        name="mlp",
    )(x2, g, wup, wdn)


def _mix1_body(x_ref, xh_ref, gmix_ref, win_ref, cw_ref, wpool_ref, ps_ref, wout_ref,
               gx_ref, wq_ref, gq_ref, kx_ref, vx_ref, wo_ref, o_ref, *, tm, seq):
    i = pl.program_id(0)
    x = x_ref[...]
    xf = jnp.concatenate([xh_ref[...], x], axis=0)
    hf = _rms(xf, gmix_ref[...]).astype(BF16)
    y = jnp.dot(hf, win_ref[...], preferred_element_type=F32)
    cw = y.shape[1] // 4
    pos0 = (i * tm) % seq
    rid = lax.broadcasted_iota(jnp.int32, (tm + POOL_HALO, 1), 0)
    keep = jnp.where(rid >= POOL_HALO, 1.0, jnp.where(pos0 > 0, 1.0, 0.0))

    gb = y[POOL_HALO:, 0:cw]
    u = y[:, cw:2 * cw] * y[:, 2 * cw:3 * cw] * keep
    z = y[:, 3 * cw:4 * cw] * keep

    taps = cw_ref[...]
    conv = taps[CONV_W - 1:CONV_W, :] * u[POOL_HALO:]
    for back in range(1, CONV_W):
        conv = conv + taps[CONV_W - 1 - back:CONV_W - back, :] * pltpu.roll(u, back, axis=0)[POOL_HALO:]
    yc = gb * conv

    pos = pos0 + lax.broadcasted_iota(jnp.int32, (tm, 1), 0)
    gw = cw // len(POOL_WINDOWS)
    parts = []
    for g, w in enumerate(POOL_WINDOWS):
        zg = z[:, g * gw:(g + 1) * gw]
        s = zg
        sh = 1
        while sh < w:
            s = s + pltpu.roll(s, sh, axis=0)
            sh *= 2
        cnt = jnp.minimum(pos + 1, w).astype(F32)
        pooled = s[POOL_HALO:] / cnt - zg[POOL_HALO:]
        parts.append(jnp.dot(pooled.astype(BF16), wpool_ref[g], preferred_element_type=F32))
    yd = jnp.concatenate(parts, axis=1) * ps_ref[...]

    mix = (jnp.dot(yc.astype(BF16), wout_ref[0:cw, :], preferred_element_type=F32)
           + jnp.dot(yd.astype(BF16), wout_ref[cw:2 * cw, :], preferred_element_type=F32))
    o_ref[...] = _xattn_tail(x + mix, gx_ref, wq_ref, gq_ref, kx_ref, vx_ref, wo_ref)


def _mix1_call(x2, gmix, win, cw, wpool, ps, wout, gx, wq, gq, kx, vx, wo, *, seq, tm):
    t, d = x2.shape
    row = pl.BlockSpec((tm, d), lambda i: (i, 0))
    halo = pl.BlockSpec((POOL_HALO, d), lambda i: (jnp.maximum(i * (tm // POOL_HALO) - 1, 0), 0))
    mem_spec = pl.BlockSpec((1,) + kx.shape[1:], lambda i: ((i * tm) // seq, 0, 0))
    fulls = [gmix, win, cw, wpool, ps, wout, gx, wq, gq]
    return pl.pallas_call(
        functools.partial(_mix1_body, tm=tm, seq=seq),
        grid=(t // tm,),
        in_specs=[row, halo] + [_full_spec(a) for a in fulls] + [mem_spec, mem_spec, _full_spec(wo)],
        out_specs=row,
        out_shape=jax.ShapeDtypeStruct((t, d), F32),
        compiler_params=_cparams("parallel"),
        name="mix1",
    )(x2, x2, gmix, win, cw, wpool, ps, wout, gx, wq, gq, kx, vx, wo)


def _row(v, reps=1):
    return jnp.tile(v.astype(F32), reps).reshape(1, -1)


def kernel(x, mem, positions, g_mix, g_xattn, g_mem, g_mlp, wq_x, wkv_x, gq_x, gk_x, wo_x, w_up, w_down,
           w_in_e, gq_a, gk_a, lam_q1, lam_k1, lam_q2, lam_k2, g_sub_a, g_kv_b, w_kv_up_b, gq_b, gk_b,
           g_kidx, w_out_e, w_in_o, conv_w, w_pool, pool_scale, w_out_o):
    batch, seq, d = x.shape
    t = batch * seq
    k_sel = min(TOPK_MAX, seq // 4)
    tm = min(1024, seq)
    tq_b, tk_b = min(256, seq), min(512, seq)
    assert seq % tm == 0 and seq % tq_b == 0 and seq % tk_b == 0 and tk_b >= k_sel

    x2 = x.reshape(t, d)
    pos = positions.reshape(t // tm, 1, tm)
    inv_freq = (ROPE_THETA ** (-jnp.arange(0, ROT_DIM, 2, dtype=F32) / ROT_DIM)).reshape(ROT_HALF, 1)
    mem2 = mem.reshape(batch * mem.shape[1], d)
    depth = g_mix.shape[0]
    w_up_b, w_down_b = w_up.astype(BF16), w_down.astype(BF16)

    for layer in range(depth):
        j = layer // 2
        kx, vx = _memkv_call(mem2, _row(g_mem[layer]), wkv_x[layer].astype(BF16), _row(gk_x[layer]))
        kx = kx.reshape(batch, mem.shape[1], -1)
        vx = vx.reshape(batch, mem.shape[1], -1)
        tail = (_row(g_xattn[layer]), wq_x[layer].astype(BF16), _row(gq_x[layer]) * (X_HEAD_DIM ** -0.5),
                kx, vx, wo_x[layer].astype(BF16))
        if layer % 2 == 0:
            lam_init = 0.8 - 0.6 * float(np.exp(-0.3 * layer))
            lam = (jnp.exp(jnp.sum(lam_q1[j].astype(F32) * lam_k1[j].astype(F32)))
                   - jnp.exp(jnp.sum(lam_q2[j].astype(F32) * lam_k2[j].astype(F32))) + lam_init)
            w_in = w_in_e[j]
            n_main = 2304
            n_idx = w_in.shape[1] - n_main
            widx = jnp.pad(w_in[:, n_main:], ((0, 0), (0, 3 * LANES - n_idx)))
            if IDX_PRECISION is None:
                widx = widx.astype(BF16)
            heads8 = 512 // HEAD_DIM
            bd = jnp.kron(jnp.eye(heads8, dtype=F32), jnp.ones((HEAD_DIM, HEAD_DIM), F32)).astype(BF16)
            gkidx = jnp.pad(g_kidx[j].astype(F32), (0, LANES - IDX_DIM)).reshape(1, LANES)
            qa, ka, vat, qb, kb, vbt, qi, kik, wi = _in0_call(
                x2, _row(g_mix[layer]), w_in[:, :n_main].astype(BF16), widx, w_kv_up_b[j].astype(BF16),
                pos, inv_freq, bd, _row(gq_a[j], heads8), _row(gk_a[j], heads8), _row(gq_b[j], heads8),
                _row(gk_b[j], heads8), _row(g_kv_b[j]), gkidx, tm=tm, tk=tk_b, idx_precision=IDX_PRECISION)
            out_a = _diff_attn_call(lam.reshape(1).astype(F32), qa, ka, vat, _row(g_sub_a[j]),
                                    batch=batch, seq=seq, tq=tq_b, tk=tk_b, group=2, out_scale=1.0 - lam_init)
            tri = jnp.tril(jnp.ones((LANES, LANES), F32)).astype(BF16)
            out_b = _dsa_call(qi, wi, qb, kik, kb, vbt, tri, batch=batch, seq=seq, tq=tq_b, tk=tk_b,
                              k_sel=k_sel, idx_precision=IDX_PRECISION)
            w_out = w_out_e[j].astype(BF16)
            x2 = _out0_call(x2, out_a, out_b, w_out[:512], w_out[512:], *tail, seq=seq, tm=tm)
        else:
            taps = jnp.pad(conv_w[j].astype(F32), ((0, SUBLANES - CONV_W), (0, 0)))
            x2 = _mix1_call(x2, _row(g_mix[layer]), w_in_o[j].astype(BF16), taps, w_pool[j].astype(BF16),
                            _row(pool_scale[j]), w_out_o[j].astype(BF16), *tail, seq=seq, tm=tm)
        x2 = _mlp_call(x2, _row(g_mlp[layer]), w_up_b, w_down_b, layer, tm=min(1024, t), tf=512)
    return x2.reshape(batch, seq, d)
```

```python
import functools

import numpy as np
import jax
import jax.numpy as jnp
from jax import lax
from jax.experimental import pallas as pl
from jax.experimental.pallas import tpu as pltpu

F32 = jnp.float32
BF16 = jnp.bfloat16

HEAD_DIM = 64
ROT_DIM = HEAD_DIM // 4
ROT_HALF = ROT_DIM // 2
ROPE_THETA = 500000.0
IDX_HEADS = 4
IDX_DIM = 64
TOPK_MAX = 256
CONV_W = 3
POOL_WINDOWS = (2, 4, 8, 16)
POOL_HALO = 16
X_HEADS = 4
X_HEAD_DIM = 128
EPS = 1e-6
LOG2_E = 1.4426950408889634

LANES = 128
SUBLANES = 8
VMEM_LIMIT_BYTES = 56 * 1024 * 1024

KEY_NEG_INF = np.int32(-2139095041)
INT32_MIN = np.int32(-(2 ** 31))

NT_DIMS = (((1,), (1,)), ((), ()))

IDX_PRECISION = None


def _cparams(*sem):
    return pltpu.CompilerParams(dimension_semantics=sem, vmem_limit_bytes=VMEM_LIMIT_BYTES)


def _full_spec(arr):
    nd = arr.ndim
    return pl.BlockSpec(arr.shape, lambda *_: (0,) * nd)


def _rms(x, g):
    ms = jnp.mean(x * x, axis=-1, keepdims=True)
    return x * lax.rsqrt(ms + EPS) * g


def _tile_lanes(t, width):
    reps = width // t.shape[-1]
    return t if reps == 1 else jnp.concatenate([t] * reps, axis=1)


def _rope_tables(pos_row, inv_col):
    ang = inv_col * pos_row
    cos, sin = jnp.cos(ang), jnp.sin(ang)
    rest = (HEAD_DIM - ROT_DIM, ang.shape[1])
    head_c = jnp.concatenate([cos, cos, jnp.ones(rest, F32)], axis=0)
    head_s = jnp.concatenate([-sin, sin, jnp.zeros(rest, F32)], axis=0)
    return (jnp.concatenate([head_c, head_c], axis=0).T, jnp.concatenate([head_s, head_s], axis=0).T)


def _rope(y, c, s):
    w = y.shape[-1]
    c, s = _tile_lanes(c, w), _tile_lanes(s, w)
    lane = lax.broadcasted_iota(jnp.int32, (1, w), 1)
    upper = pltpu.roll(y, w - ROT_HALF, axis=1)
    lower = pltpu.roll(y, ROT_HALF, axis=1)
    partner = jnp.where((lane & (HEAD_DIM - 1)) < ROT_HALF, upper, lower)
    return y * c + partner * s


def _headnorm64(y, blockdiag, g):
    ss = jnp.dot((y * y).astype(BF16), blockdiag, preferred_element_type=F32)
    return y * lax.rsqrt(ss * (1.0 / HEAD_DIM) + EPS) * g


def _in0_body(x_ref, gmix_ref, wmain_ref, widx_ref, wkvup_ref, pos_ref, inv_ref,
              bd_ref, gqa_ref, gka_ref, gqb_ref, gkb_ref, gkv_ref, gkidx_ref,
              qa_ref, ka_ref, vat_ref, qb_ref, kb_ref, vbt_ref, qi_ref, kik_ref, wi_ref,
              *, idx_precision):
    h = _rms(x_ref[...], gmix_ref[...])
    hb = h.astype(BF16)
    proj = lambda lo, hi: jnp.dot(hb, wmain_ref[:, lo:hi], preferred_element_type=F32)
    rot = _rope_tables(pos_ref[0].astype(F32), inv_ref[...])
    bd = bd_ref[...]
    scale = HEAD_DIM ** -0.5 * LOG2_E

    qa = _rope(_headnorm64(proj(0, 512), bd, gqa_ref[...]), *rot) * scale
    qa_ref[...] = qa.astype(BF16)
    ka = _rope(_headnorm64(proj(512, 1024), bd, gka_ref[...]), *rot)
    ka_ref[...] = ka.astype(BF16)
    tk = vat_ref.shape[2]
    va = proj(1024, 1536)
    for r in range(vat_ref.shape[0]):
        vat_ref[r] = va[r * tk:(r + 1) * tk, :].T.astype(BF16)
    qb = _rope(_headnorm64(proj(1536, 2048), bd, gqb_ref[...]), *rot) * scale
    qb_ref[...] = qb.astype(BF16)

    ckv = _rms(proj(2048, 2304), gkv_ref[...]).astype(BF16)
    kv = jnp.dot(ckv, wkvup_ref[...], preferred_element_type=F32)
    kb = _rope(_headnorm64(kv[:, 0:512], bd, gkb_ref[...]), *rot)
    kb_ref[...] = kb.astype(BF16)
    for r in range(vbt_ref.shape[0]):
        vbt_ref[r] = kv[r * tk:(r + 1) * tk, 512:1024].T.astype(BF16)

    if idx_precision is None:
        yi = jnp.dot(hb, widx_ref[...], preferred_element_type=F32)
    else:
        yi = jnp.dot(h, widx_ref[...], preferred_element_type=F32, precision=idx_precision)
    qi_ref[...] = _rope(yi[:, 0:256], *rot).astype(qi_ref.dtype)
    blk = yi[:, 256:384]
    lane = lax.broadcasted_iota(jnp.int32, blk.shape, 1)
    ms = jnp.sum(jnp.where(lane < IDX_DIM, blk * blk, 0.0), axis=-1, keepdims=True) * (1.0 / IDX_DIM)
    kin = blk * lax.rsqrt(ms + EPS) * gkidx_ref[...]
    kir = _rope(kin, *rot)
    kik = kir + pltpu.roll(kir, IDX_DIM, axis=1)
    kik_ref[...] = kik.astype(kik_ref.dtype)
    wi_ref[...] = blk * (IDX_HEADS ** -0.5 * IDX_DIM ** -0.5)


def _in0_call(x2, gmix, wmain, widx, wkvup, pos, inv, bd, gqa, gka, gqb, gkb, gkv, gkidx,
              *, tm, tk, idx_precision):
    t, d = x2.shape
    row = lambda w: pl.BlockSpec((tm, w), lambda i: (i, 0))
    fulls = [gmix, wmain, widx, wkvup]
    gains = [inv, bd, gqa, gka, gqb, gkb, gkv, gkidx]
    in_specs = ([row(d)] + [_full_spec(a) for a in fulls] + [pl.BlockSpec((1, 1, tm), lambda i: (i, 0, 0))]
                + [_full_spec(a) for a in gains])
    idx_dt = BF16 if idx_precision is None else F32
    sds = jax.ShapeDtypeStruct
    vt_shape, vt_spec = sds((t // tk, 512, tk), BF16), pl.BlockSpec((tm // tk, 512, tk), lambda i: (i, 0, 0))
    out_shape = ([sds((t, 512), BF16)] * 2 + [vt_shape] + [sds((t, 512), BF16)] * 2
                 + [vt_shape, sds((t, 256), idx_dt), sds((t, LANES), idx_dt), sds((t, LANES), F32)])
    out_specs = [row(512)] * 2 + [vt_spec] + [row(512)] * 2 + [vt_spec, row(256), row(LANES), row(LANES)]
    return pl.pallas_call(
        functools.partial(_in0_body, idx_precision=idx_precision),
        grid=(t // tm,),
        in_specs=in_specs,
        out_specs=out_specs,
        out_shape=out_shape,
        compiler_params=_cparams("parallel"),
        name="in0",
    )(x2, gmix, wmain, widx, wkvup, pos, inv, bd, gqa, gka, gqb, gkb, gkv, gkidx)


def _diff_attn_body(lam_ref, q_ref, k_ref, vt_ref, gsub_ref, o_ref, s_ref, *, tq, tk, out_scale):
    i = pl.program_id(2)
    q0 = i * tq
    n_full = (q0 + 1) // tk
    nkc = (q0 + tq + tk - 1) // tk
    lane = lax.broadcasted_iota(jnp.int32, (1, LANES), 1)
    lane_lo = lane < HEAD_DIM
    qpos = q0 + lax.broadcasted_iota(jnp.int32, (1, tq), 1)
    qpos2 = jnp.concatenate([qpos, qpos], axis=1)
    neg_inf = jnp.float32(-jnp.inf)
    heads = range(q_ref.shape[1] // LANES)
    q_cat = []
    for g in heads:
        q_h = q_ref[:, g * LANES:(g + 1) * LANES]
        q_cat.append(jnp.concatenate([q_h * lane_lo.astype(BF16), q_h * (~lane_lo).astype(BF16)], axis=0))

    def logit_step(g, c, m, masked):
        s = lax.dot_general(k_ref[pl.ds(pl.multiple_of(c * tk, tk), tk), g * LANES:(g + 1) * LANES],
                            q_cat[g], NT_DIMS, preferred_element_type=F32)
        if masked:
            kpos = c * tk + lax.broadcasted_iota(jnp.int32, (tk, 1), 0)
            s = jnp.where(kpos <= qpos2, s, neg_inf)
        s_ref[g % s_ref.shape[0], c] = s
        return jnp.maximum(m, jnp.max(s.reshape(tk // SUBLANES, SUBLANES, 2 * tq), axis=0))

    def pv_step(g, c, shift, l8, acc):
        pe = jnp.exp2(s_ref[g % s_ref.shape[0], c] - shift)
        acc = acc + jnp.dot(vt_ref[c, g * LANES:(g + 1) * LANES, :], pe.astype(BF16),
                            preferred_element_type=F32)
        return l8 + jnp.sum(pe.reshape(tk // SUBLANES, SUBLANES, 2 * tq), axis=0), acc

    def run(body, init):
        carry = lax.fori_loop(0, n_full, functools.partial(body, masked=False), init)
        return lax.fori_loop(n_full, nkc, functools.partial(body, masked=True), carry)

    lam = lam_ref[0]
    m_init = jnp.full((SUBLANES, 2 * tq), neg_inf, F32)
    pv_init = (jnp.zeros((SUBLANES, 2 * tq), F32), jnp.zeros((LANES, 2 * tq), F32))

    def finish(g, l8, acc):
        o_all = acc / jnp.sum(l8, axis=0, keepdims=True)
        o = o_all[:, 0:tq] - lam * o_all[:, tq:2 * tq]
        o = o * lax.rsqrt(jnp.mean(o * o, axis=0, keepdims=True) + EPS)
        o_ref[:, g * LANES:(g + 1) * LANES] = (o.T * gsub_ref[...] * out_scale).astype(o_ref.dtype)

    width = s_ref.shape[0] // 2
    stages = [heads[lo:lo + width] for lo in range(0, len(heads), width)]

    def logit_stage(stage, c, ms, masked):
        return tuple(logit_step(g, c, m, masked) for g, m in zip(stage, ms))

    def pv_stage(stage, c, shifts, carry):
        return tuple(pv_step(g, c, sh, *lc) for g, sh, lc in zip(stage, shifts, carry))

    ms = run(lambda c, ms, masked: logit_stage(stages[0], c, ms, masked), (m_init,) * width)
    for prev, cur in zip(stages[:-1], stages[1:]):
        shifts = [jnp.max(m, axis=0, keepdims=True) for m in ms]

        def fused(c, carry, masked, prev=prev, cur=cur, shifts=shifts):
            pv, ms = carry
            return pv_stage(prev, c, shifts, pv), logit_stage(cur, c, ms, masked)

        pv, ms = run(fused, ((pv_init,) * width, (m_init,) * width))
        for g, lc in zip(prev, pv):
            finish(g, *lc)
    shifts = [jnp.max(m, axis=0, keepdims=True) for m in ms]
    pv = lax.fori_loop(0, nkc, lambda c, pv: pv_stage(stages[-1], c, shifts, pv), (pv_init,) * width)
    for g, lc in zip(stages[-1], pv):
        finish(g, *lc)


def _diff_attn_call(lam, qa, ka, vat, gsub, *, batch, seq, tq, tk, group, out_scale):
    t = qa.shape[0]
    w = group * LANES
    nq, nk = seq // tq, seq // tk
    q_spec = pl.BlockSpec((tq, w), lambda b, h, i: (b * nq + i, h))
    return pl.pallas_call(
        functools.partial(_diff_attn_body, tq=tq, tk=tk, out_scale=out_scale),
        grid=(batch, qa.shape[1] // w, nq),
        in_specs=[pl.BlockSpec(memory_space=pltpu.SMEM), q_spec,
                  pl.BlockSpec((seq, w), lambda b, h, i: (b, h)),
                  pl.BlockSpec((nk, w, tk), lambda b, h, i: (b, h, 0)), _full_spec(gsub)],
        out_specs=q_spec,
        out_shape=jax.ShapeDtypeStruct((t, qa.shape[1]), BF16),
        scratch_shapes=[pltpu.VMEM((group, nk, tk, 2 * tq), F32)],
        compiler_params=_cparams("parallel", "parallel", "arbitrary"),
        name="diff_attn",
    )(lam, qa, ka, vat, gsub)


def _key_to_float(key):
    bits = key ^ ((key >> 31) & jnp.int32(0x7FFFFFFF))
    return lax.bitcast_convert_type(bits, F32)


def _dsa_body(qi_ref, wi_ref, qb_ref, kik_ref, kb_ref, vbt_ref, tri_ref, o_ref, sc_ref, bias_ref, s_ref,
              *, tq, tk, k_sel, idx_precision):
    i = pl.program_id(1)
    q0 = i * tq
    nkc = (q0 + tq + tk - 1) // tk
    qpos = q0 + lax.broadcasted_iota(jnp.int32, (1, tq), 1)
    lane = lax.broadcasted_iota(jnp.int32, (1, LANES), 1)
    lane_lo = lane < HEAD_DIM
    tiles = tk // LANES
    neg_inf = jnp.float32(-jnp.inf)

    wi_t = wi_ref[...].T
    qi = qi_ref[...]
    q_heads = []
    for h in range(IDX_HEADS):
        pair = qi[:, (h // 2) * LANES:(h // 2 + 1) * LANES]
        q_heads.append(pair * (lane_lo if h % 2 == 0 else ~lane_lo).astype(pair.dtype))
    qi_cat = jnp.concatenate(q_heads, axis=0)
    w_heads = [wi_t[IDX_DIM + h:IDX_DIM + h + 1, :] for h in range(IDX_HEADS)]

    def score_chunk(c, carry):
        kk = kik_ref[pl.ds(pl.multiple_of(c * tk, tk), tk), :]
        s = lax.dot_general(kk, qi_cat, NT_DIMS, preferred_element_type=F32, precision=idx_precision)
        acc = jnp.zeros((tk, tq), F32)
        for h in range(IDX_HEADS):
            acc = acc + jnp.maximum(s[:, h * tq:(h + 1) * tq], 0.0) * w_heads[h]
        kpos = c * tk + lax.broadcasted_iota(jnp.int32, (tk, 1), 0)
        sc_ref[c] = jnp.where(kpos <= qpos, acc, neg_inf)
        return carry

    lax.fori_loop(0, nkc, score_chunk, 0)

    acc_rows = 4 * SUBLANES

    def count_ge(cand):
        def body(c, acc):
            for r in range(tk // acc_rows):
                acc = acc + jnp.where(sc_ref[c, r * acc_rows:(r + 1) * acc_rows, :] >= cand, 1.0, 0.0)
            return acc
        acc = lax.fori_loop(0, nkc, body, jnp.zeros((acc_rows, tq), F32))
        return jnp.sum(acc, axis=0, keepdims=True)

    ksel = float(k_sel)

    def count_at(cand_key):
        return jnp.where(cand_key <= KEY_NEG_INF, ksel, count_ge(_key_to_float(cand_key)))

    zero_key = jnp.zeros((1, tq), jnp.int32)
    thr = jnp.where(count_at(zero_key) >= ksel, zero_key, INT32_MIN)

    def bit_step(b, thr):
        cand = thr + jnp.left_shift(jnp.int32(1), 30 - b)
        return jnp.where(count_at(cand) >= ksel, cand, thr)

    thr = lax.fori_loop(0, 31, bit_step, thr)
    thr_f = _key_to_float(thr)
    above_f = _key_to_float(thr + 1)
    need = ksel - count_ge(above_f)
    tri = tri_ref[...]

    def select_chunk(c, taken):
        for t in range(tiles):
            st = sc_ref[c, t * LANES:(t + 1) * LANES, :]
            is_above = st >= above_f
            tied = jnp.where(is_above, 0.0, jnp.where(st >= thr_f, 1.0, 0.0))
            prefix = jnp.dot(tri, tied.astype(BF16), preferred_element_type=F32)
            tie_sel = jnp.where((taken + prefix) <= need, tied, 0.0)
            sel = jnp.where(is_above, 0.0, jnp.where(tie_sel > 0.0, 0.0, neg_inf))
            kpos = c * tk + t * LANES + lax.broadcasted_iota(jnp.int32, (LANES, 1), 0)
            bias_ref[c, t * LANES:(t + 1) * LANES, :] = jnp.where(kpos <= qpos, sel, neg_inf)
            taken = taken + prefix[LANES - 1:LANES, :]
        return taken

    lax.fori_loop(0, nkc, select_chunk, jnp.zeros((1, tq), F32))

    group = s_ref.shape[0]
    for p0 in range(0, qb_ref.shape[1] // LANES, group):
        pairs = range(p0, p0 + group)
        q_cat = []
        for p in pairs:
            q_pair = qb_ref[:, p * LANES:(p + 1) * LANES]
            q_cat.append(jnp.concatenate([q_pair * lane_lo.astype(BF16), q_pair * (~lane_lo).astype(BF16)],
                                         axis=0))

        def logit_chunk(c, carry, pairs=pairs, q_cat=q_cat):
            bias = bias_ref[c]
            bias2 = jnp.concatenate([bias, bias], axis=1)
            out = []
            for g, (p, m) in enumerate(zip(pairs, carry)):
                kc = kb_ref[pl.ds(pl.multiple_of(c * tk, tk), tk), p * LANES:(p + 1) * LANES]
                s = lax.dot_general(kc, q_cat[g], NT_DIMS, preferred_element_type=F32) + bias2
                s_ref[g, c] = s
                out.append(jnp.maximum(m, jnp.max(s.reshape(tk // SUBLANES, SUBLANES, 2 * tq), axis=0)))
            return tuple(out)

        m8 = lax.fori_loop(0, nkc, logit_chunk,
                           tuple(jnp.full((SUBLANES, 2 * tq), neg_inf, F32) for _ in pairs))
        shift = [jnp.max(m, axis=0, keepdims=True) for m in m8]

        def pv_chunk(c, carry, pairs=pairs, shift=shift):
            out = []
            for g, (p, (l8, acc)) in enumerate(zip(pairs, carry)):
                pe = jnp.exp2(s_ref[g, c] - shift[g])
                vt = vbt_ref[c, p * LANES:(p + 1) * LANES, :]
                acc = acc + jnp.dot(vt, pe.astype(BF16), preferred_element_type=F32)
                out.append((l8 + jnp.sum(pe.reshape(tk // SUBLANES, SUBLANES, 2 * tq), axis=0), acc))
            return tuple(out)

        fin = lax.fori_loop(0, nkc, pv_chunk, tuple((jnp.zeros((SUBLANES, 2 * tq), F32),
                                                     jnp.zeros((LANES, 2 * tq), F32)) for _ in pairs))
        for p, (l8, acc) in zip(pairs, fin):
            o_all = acc / jnp.sum(l8, axis=0, keepdims=True)
            o_t = jnp.concatenate([o_all[0:HEAD_DIM, 0:tq], o_all[HEAD_DIM:LANES, tq:2 * tq]], axis=0)
            o_ref[:, p * LANES:(p + 1) * LANES] = o_t.T.astype(o_ref.dtype)


def _dsa_call(qi, wi, qb, kik, kb, vbt, tri, *, batch, seq, tq, tk, k_sel, idx_precision):
    t = qb.shape[0]
    nq, nk = seq // tq, seq // tk
    row = lambda w: pl.BlockSpec((tq, w), lambda b, i: (b * nq + i, 0))
    seqb = lambda w: pl.BlockSpec((seq, w), lambda b, i: (b, 0))
    vbt_spec = pl.BlockSpec((nk,) + vbt.shape[1:], lambda b, i: (b, 0, 0))
    return pl.pallas_call(
        functools.partial(_dsa_body, tq=tq, tk=tk, k_sel=k_sel, idx_precision=idx_precision),
        grid=(batch, nq),
        in_specs=[row(qi.shape[1]), row(LANES), row(qb.shape[1]), seqb(LANES), seqb(kb.shape[1]),
                  vbt_spec, _full_spec(tri)],
        out_specs=row(qb.shape[1]),
        out_shape=jax.ShapeDtypeStruct((t, qb.shape[1]), BF16),
        scratch_shapes=[pltpu.VMEM((nk, tk, tq), F32), pltpu.VMEM((nk, tk, tq), F32),
                        pltpu.VMEM((2, nk, tk, 2 * tq), F32)],
        compiler_params=_cparams("parallel", "arbitrary"),
        name="dsa",
    )(qi, wi, qb, kik, kb, vbt, tri)


def _memkv_body(mem_ref, g_ref, w_ref, gk_ref, k_ref, v_ref):
    hm = _rms(mem_ref[...], g_ref[...]).astype(BF16)
    kv = jnp.dot(hm, w_ref[...], preferred_element_type=F32)
    half = kv.shape[1] // 2
    for h in range(X_HEADS):
        kh = kv[:, h * X_HEAD_DIM:(h + 1) * X_HEAD_DIM]
        k_ref[:, h * X_HEAD_DIM:(h + 1) * X_HEAD_DIM] = _rms(kh, gk_ref[...]).astype(BF16)
    v_ref[...] = kv[:, half:].astype(BF16)


def _memkv_call(mem2, g, w, gk):
    rows, d = mem2.shape
    half = w.shape[1] // 2
    tm = min(rows, 512)
    row = lambda wd: pl.BlockSpec((tm, wd), lambda i: (i, 0))
    return pl.pallas_call(
        _memkv_body,
        grid=(rows // tm,),
        in_specs=[row(d), _full_spec(g), _full_spec(w), _full_spec(gk)],
        out_specs=[row(half), row(half)],
        out_shape=[jax.ShapeDtypeStruct((rows, half), BF16)] * 2,
        compiler_params=_cparams("parallel"),
        name="memkv",
    )(mem2, g, w, gk)


def _xattn_tail(x1, gx_ref, wq_ref, gq_ref, kx_ref, vx_ref, wo_ref):
    hx = _rms(x1, gx_ref[...]).astype(BF16)
    q = jnp.dot(hx, wq_ref[...], preferred_element_type=F32)
    outs = []
    for h in range(X_HEADS):
        cols = slice(h * X_HEAD_DIM, (h + 1) * X_HEAD_DIM)
        qh = _rms(q[:, cols], gq_ref[...]).astype(BF16)
        s = lax.dot_general(qh, kx_ref[0, :, cols], NT_DIMS, preferred_element_type=F32)
        p = jnp.exp(s - jnp.max(s, axis=-1, keepdims=True))
        o = (jnp.dot(p.astype(BF16), vx_ref[0, :, cols], preferred_element_type=F32)
             / jnp.sum(p, axis=-1, keepdims=True))
        outs.append(o.astype(BF16))
    o = jnp.concatenate(outs, axis=1)
    return x1 + jnp.dot(o, wo_ref[...], preferred_element_type=F32)


def _out0_body(x_ref, oa_ref, ob_ref, woa_ref, wob_ref, gx_ref, wq_ref, gq_ref, kx_ref, vx_ref, wo_ref,
               o_ref):
    mix = (jnp.dot(oa_ref[...], woa_ref[...], preferred_element_type=F32)
           + jnp.dot(ob_ref[...], wob_ref[...], preferred_element_type=F32))
    o_ref[...] = _xattn_tail(x_ref[...] + mix, gx_ref, wq_ref, gq_ref, kx_ref, vx_ref, wo_ref)


def _out0_call(x2, oa, ob, woa, wob, gx, wq, gq, kx, vx, wo, *, seq, tm):
    t, d = x2.shape
    row = lambda w: pl.BlockSpec((tm, w), lambda i: (i, 0))
    mem_spec = pl.BlockSpec((1,) + kx.shape[1:], lambda i: ((i * tm) // seq, 0, 0))
    fulls = [woa, wob, gx, wq, gq]
    return pl.pallas_call(
        _out0_body,
        grid=(t // tm,),
        in_specs=[row(d), row(oa.shape[1]), row(ob.shape[1])] + [_full_spec(a) for a in fulls]
                 + [mem_spec, mem_spec, _full_spec(wo)],
        out_specs=row(d),
        out_shape=jax.ShapeDtypeStruct((t, d), F32),
        compiler_params=_cparams("parallel"),
        name="out0",
    )(x2, oa, ob, woa, wob, gx, wq, gq, kx, vx, wo)


def _mlp_body(x_ref, g_ref, wup_ref, wdn_ref, o_ref, h_ref, acc_ref):
    j = pl.program_id(1)

    @pl.when(j == 0)
    def _():
        h_ref[...] = _rms(x_ref[...], g_ref[...]).astype(BF16)
        acc_ref[...] = jnp.zeros(acc_ref.shape, F32)

    u = jnp.dot(h_ref[...], wup_ref[...], preferred_element_type=F32)
    a = jnp.square(jnp.maximum(u, 0.0)).astype(BF16)
    acc_ref[...] += jnp.dot(a, wdn_ref[...], preferred_element_type=F32)

    @pl.when(j == pl.num_programs(1) - 1)
    def _():
        o_ref[...] = x_ref[...] + acc_ref[...]


def _mlp_call(x2, g, wup, wdn, layer, *, tm, tf):
    t, d = x2.shape
    ff = wup.shape[2]
    return pl.pallas_call(
        _mlp_body,
        grid=(t // tm, ff // tf),
        in_specs=[pl.BlockSpec((tm, d), lambda i, j: (i, 0)), _full_spec(g),
                  pl.BlockSpec((None, d, tf), lambda i, j: (layer, 0, j)),
                  pl.BlockSpec((None, tf, d), lambda i, j: (layer, j, 0))],
        out_specs=pl.BlockSpec((tm, d), lambda i, j: (i, 0)),
        out_shape=jax.ShapeDtypeStruct((t, d), F32),
        scratch_shapes=[pltpu.VMEM((tm, d), BF16), pltpu.VMEM((tm, d), F32)],
        compiler_params=_cparams("parallel", "arbitrary"),
        name="mlp",
    )(x2, g, wup, wdn)


def _mix1_body(x_ref, xh_ref, gmix_ref, win_ref, cw_ref, wpool_ref, ps_ref, wout_ref,
               gx_ref, wq_ref, gq_ref, kx_ref, vx_ref, wo_ref, o_ref, *, tm, seq):
    i = pl.program_id(0)
    x = x_ref[...]
    xf = jnp.concatenate([xh_ref[...], x], axis=0)
    hf = _rms(xf, gmix_ref[...]).astype(BF16)
    y = jnp.dot(hf, win_ref[...], preferred_element_type=F32)
    cw = y.shape[1] // 4
    pos0 = (i * tm) % seq
    rid = lax.broadcasted_iota(jnp.int32, (tm + POOL_HALO, 1), 0)
    keep = jnp.where(rid >= POOL_HALO, 1.0, jnp.where(pos0 > 0, 1.0, 0.0))

    gb = y[POOL_HALO:, 0:cw]
    u = y[:, cw:2 * cw] * y[:, 2 * cw:3 * cw] * keep
    z = y[:, 3 * cw:4 * cw] * keep

    taps = cw_ref[...]
    conv = taps[CONV_W - 1:CONV_W, :] * u[POOL_HALO:]
    for back in range(1, CONV_W):
        conv = conv + taps[CONV_W - 1 - back:CONV_W - back, :] * pltpu.roll(u, back, axis=0)[POOL_HALO:]
    yc = gb * conv

    pos = pos0 + lax.broadcasted_iota(jnp.int32, (tm, 1), 0)
    gw = cw // len(POOL_WINDOWS)
    parts = []
    for g, w in enumerate(POOL_WINDOWS):
        zg = z[:, g * gw:(g + 1) * gw]
        s = zg
        sh = 1
        while sh < w:
            s = s + pltpu.roll(s, sh, axis=0)
            sh *= 2
        cnt = jnp.minimum(pos + 1, w).astype(F32)
        pooled = s[POOL_HALO:] / cnt - zg[POOL_HALO:]
        parts.append(jnp.dot(pooled.astype(BF16), wpool_ref[g], preferred_element_type=F32))
    yd = jnp.concatenate(parts, axis=1) * ps_ref[...]

    mix = (jnp.dot(yc.astype(BF16), wout_ref[0:cw, :], preferred_element_type=F32)
           + jnp.dot(yd.astype(BF16), wout_ref[cw:2 * cw, :], preferred_element_type=F32))
    o_ref[...] = _xattn_tail(x + mix, gx_ref, wq_ref, gq_ref, kx_ref, vx_ref, wo_ref)


def _mix1_call(x2, gmix, win, cw, wpool, ps, wout, gx, wq, gq, kx, vx, wo, *, seq, tm):
    t, d = x2.shape
    row = pl.BlockSpec((tm, d), lambda i: (i, 0))
    halo = pl.BlockSpec((POOL_HALO, d), lambda i: (jnp.maximum(i * (tm // POOL_HALO) - 1, 0), 0))
    mem_spec = pl.BlockSpec((1,) + kx.shape[1:], lambda i: ((i * tm) // seq, 0, 0))
    fulls = [gmix, win, cw, wpool, ps, wout, gx, wq, gq]
    return pl.pallas_call(
        functools.partial(_mix1_body, tm=tm, seq=seq),
        grid=(t // tm,),
        in_specs=[row, halo] + [_full_spec(a) for a in fulls] + [mem_spec, mem_spec, _full_spec(wo)],
        out_specs=row,
        out_shape=jax.ShapeDtypeStruct((t, d), F32),
        compiler_params=_cparams("parallel"),
        name="mix1",
    )(x2, x2, gmix, win, cw, wpool, ps, wout, gx, wq, gq, kx, vx, wo)


def _row(v, reps=1):
    return jnp.tile(v.astype(F32), reps).reshape(1, -1)


def kernel(x, mem, positions, g_mix, g_xattn, g_mem, g_mlp, wq_x, wkv_x, gq_x, gk_x, wo_x, w_up, w_down,
           w_in_e, gq_a, gk_a, lam_q1, lam_k1, lam_q2, lam_k2, g_sub_a, g_kv_b, w_kv_up_b, gq_b, gk_b,
           g_kidx, w_out_e, w_in_o, conv_w, w_pool, pool_scale, w_out_o):
    batch, seq, d = x.shape
    t = batch * seq
    k_sel = min(TOPK_MAX, seq // 4)
    tm = min(1024, seq)
    tq_b, tk_b = min(256, seq), min(512, seq)
    assert seq % tm == 0 and seq % tq_b == 0 and seq % tk_b == 0 and tk_b >= k_sel

    x2 = x.reshape(t, d)
    pos = positions.reshape(t // tm, 1, tm)
    inv_freq = (ROPE_THETA ** (-jnp.arange(0, ROT_DIM, 2, dtype=F32) / ROT_DIM)).reshape(ROT_HALF, 1)
    mem2 = mem.reshape(batch * mem.shape[1], d)
    depth = g_mix.shape[0]
    w_up_b, w_down_b = w_up.astype(BF16), w_down.astype(BF16)

    for layer in range(depth):
        j = layer // 2
        kx, vx = _memkv_call(mem2, _row(g_mem[layer]), wkv_x[layer].astype(BF16), _row(gk_x[layer]))
        kx = kx.reshape(batch, mem.shape[1], -1)
        vx = vx.reshape(batch, mem.shape[1], -1)
        tail = (_row(g_xattn[layer]), wq_x[layer].astype(BF16), _row(gq_x[layer]) * (X_HEAD_DIM ** -0.5),
                kx, vx, wo_x[layer].astype(BF16))
        if layer % 2 == 0:
            lam_init = 0.8 - 0.6 * float(np.exp(-0.3 * layer))
            lam = (jnp.exp(jnp.sum(lam_q1[j].astype(F32) * lam_k1[j].astype(F32)))
                   - jnp.exp(jnp.sum(lam_q2[j].astype(F32) * lam_k2[j].astype(F32))) + lam_init)
            w_in = w_in_e[j]
            n_main = 2304
            n_idx = w_in.shape[1] - n_main
            widx = jnp.pad(w_in[:, n_main:], ((0, 0), (0, 3 * LANES - n_idx)))
            if IDX_PRECISION is None:
                widx = widx.astype(BF16)
            heads8 = 512 // HEAD_DIM
            bd = jnp.kron(jnp.eye(heads8, dtype=F32), jnp.ones((HEAD_DIM, HEAD_DIM), F32)).astype(BF16)
            gkidx = jnp.pad(g_kidx[j].astype(F32), (0, LANES - IDX_DIM)).reshape(1, LANES)
            qa, ka, vat, qb, kb, vbt, qi, kik, wi = _in0_call(
                x2, _row(g_mix[layer]), w_in[:, :n_main].astype(BF16), widx, w_kv_up_b[j].astype(BF16),
                pos, inv_freq, bd, _row(gq_a[j], heads8), _row(gk_a[j], heads8), _row(gq_b[j], heads8),
                _row(gk_b[j], heads8), _row(g_kv_b[j]), gkidx, tm=tm, tk=tk_b, idx_precision=IDX_PRECISION)
            out_a = _diff_attn_call(lam.reshape(1).astype(F32), qa, ka, vat, _row(g_sub_a[j]),
                                    batch=batch, seq=seq, tq=tq_b, tk=tk_b, group=qa.shape[1] // LANES, out_scale=1.0 - lam_init)
            tri = jnp.tril(jnp.ones((LANES, LANES), F32)).astype(BF16)
            out_b = _dsa_call(qi, wi, qb, kik, kb, vbt, tri, batch=batch, seq=seq, tq=tq_b, tk=tk_b,
                              k_sel=k_sel, idx_precision=IDX_PRECISION)
            w_out = w_out_e[j].astype(BF16)
            x2 = _out0_call(x2, out_a, out_b, w_out[:512], w_out[512:], *tail, seq=seq, tm=tm)
        else:
            taps = jnp.pad(conv_w[j].astype(F32), ((0, SUBLANES - CONV_W), (0, 0)))
            x2 = _mix1_call(x2, _row(g_mix[layer]), w_in_o[j].astype(BF16), taps, w_pool[j].astype(BF16),
                            _row(pool_scale[j]), w_out_o[j].astype(BF16), *tail, seq=seq, tm=tm)
        x2 = _mlp_call(x2, _row(g_mlp[layer]), w_up_b, w_down_b, layer, tm=min(1024, t), tf=1024)
    return x2.reshape(batch, seq, d)
```

```python
import functools

import numpy as np
import jax
import jax.numpy as jnp
from jax import lax
from jax.experimental import pallas as pl
from jax.experimental.pallas import tpu as pltpu

F32 = jnp.float32
BF16 = jnp.bfloat16

HEAD_DIM = 64
ROT_DIM = HEAD_DIM // 4
ROT_HALF = ROT_DIM // 2
ROPE_THETA = 500000.0
IDX_HEADS = 4
IDX_DIM = 64
TOPK_MAX = 256
CONV_W = 3
POOL_WINDOWS = (2, 4, 8, 16)
POOL_HALO = 16
X_HEADS = 4
X_HEAD_DIM = 128
EPS = 1e-6
LOG2_E = 1.4426950408889634

LANES = 128
SUBLANES = 8
VMEM_LIMIT_BYTES = 56 * 1024 * 1024

KEY_NEG_INF = np.int32(-2139095041)
INT32_MIN = np.int32(-(2 ** 31))

NT_DIMS = (((1,), (1,)), ((), ()))

IDX_PRECISION = None


def _cparams(*sem):
    return pltpu.CompilerParams(dimension_semantics=sem, vmem_limit_bytes=VMEM_LIMIT_BYTES)


def _full_spec(arr):
    nd = arr.ndim
    return pl.BlockSpec(arr.shape, lambda *_: (0,) * nd)


def _rms(x, g):
    ms = jnp.mean(x * x, axis=-1, keepdims=True)
    return x * lax.rsqrt(ms + EPS) * g


def _tile_lanes(t, width):
    reps = width // t.shape[-1]
    return t if reps == 1 else jnp.concatenate([t] * reps, axis=1)


def _rope_tables(pos_row, inv_col):
    ang = inv_col * pos_row
    cos, sin = jnp.cos(ang), jnp.sin(ang)
    rest = (HEAD_DIM - ROT_DIM, ang.shape[1])
    head_c = jnp.concatenate([cos, cos, jnp.ones(rest, F32)], axis=0)
    head_s = jnp.concatenate([-sin, sin, jnp.zeros(rest, F32)], axis=0)
    return (jnp.concatenate([head_c, head_c], axis=0).T, jnp.concatenate([head_s, head_s], axis=0).T)


def _rope(y, c, s):
    w = y.shape[-1]
    c, s = _tile_lanes(c, w), _tile_lanes(s, w)
    lane = lax.broadcasted_iota(jnp.int32, (1, w), 1)
    upper = pltpu.roll(y, w - ROT_HALF, axis=1)
    lower = pltpu.roll(y, ROT_HALF, axis=1)
    partner = jnp.where((lane & (HEAD_DIM - 1)) < ROT_HALF, upper, lower)
    return y * c + partner * s


def _headnorm64(y, blockdiag, g):
    ss = jnp.dot((y * y).astype(BF16), blockdiag, preferred_element_type=F32)
    return y * lax.rsqrt(ss * (1.0 / HEAD_DIM) + EPS) * g


def _in0_body(x_ref, gmix_ref, wmain_ref, widx_ref, wkvup_ref, pos_ref, inv_ref,
              bd_ref, gqa_ref, gka_ref, gqb_ref, gkb_ref, gkv_ref, gkidx_ref,
              qa_ref, ka_ref, vat_ref, qb_ref, kb_ref, vbt_ref, qi_ref, kik_ref, wi_ref,
              *, idx_precision):
    h = _rms(x_ref[...], gmix_ref[...])
    hb = h.astype(BF16)
    proj = lambda lo, hi: jnp.dot(hb, wmain_ref[:, lo:hi], preferred_element_type=F32)
    rot = _rope_tables(pos_ref[0].astype(F32), inv_ref[...])
    bd = bd_ref[...]
    scale = HEAD_DIM ** -0.5 * LOG2_E

    qa = _rope(_headnorm64(proj(0, 512), bd, gqa_ref[...]), *rot) * scale
    qa_ref[...] = qa.astype(BF16)
    ka = _rope(_headnorm64(proj(512, 1024), bd, gka_ref[...]), *rot)
    ka_ref[...] = ka.astype(BF16)
    tk = vat_ref.shape[2]
    va = proj(1024, 1536)
    for r in range(vat_ref.shape[0]):
        vat_ref[r] = va[r * tk:(r + 1) * tk, :].T.astype(BF16)
    qb = _rope(_headnorm64(proj(1536, 2048), bd, gqb_ref[...]), *rot) * scale
    qb_ref[...] = qb.astype(BF16)

    ckv = _rms(proj(2048, 2304), gkv_ref[...]).astype(BF16)
    kv = jnp.dot(ckv, wkvup_ref[...], preferred_element_type=F32)
    kb = _rope(_headnorm64(kv[:, 0:512], bd, gkb_ref[...]), *rot)
    kb_ref[...] = kb.astype(BF16)
    for r in range(vbt_ref.shape[0]):
        vbt_ref[r] = kv[r * tk:(r + 1) * tk, 512:1024].T.astype(BF16)

    if idx_precision is None:
        yi = jnp.dot(hb, widx_ref[...], preferred_element_type=F32)
    else:
        yi = jnp.dot(h, widx_ref[...], preferred_element_type=F32, precision=idx_precision)
    qi_ref[...] = _rope(yi[:, 0:256], *rot).astype(qi_ref.dtype)
    blk = yi[:, 256:384]
    lane = lax.broadcasted_iota(jnp.int32, blk.shape, 1)
    ms = jnp.sum(jnp.where(lane < IDX_DIM, blk * blk, 0.0), axis=-1, keepdims=True) * (1.0 / IDX_DIM)
    kin = blk * lax.rsqrt(ms + EPS) * gkidx_ref[...]
    kir = _rope(kin, *rot)
    kik = kir + pltpu.roll(kir, IDX_DIM, axis=1)
    kik_ref[...] = kik.astype(kik_ref.dtype)
    wi_ref[...] = blk * (IDX_HEADS ** -0.5 * IDX_DIM ** -0.5)


def _in0_call(x2, gmix, wmain, widx, wkvup, pos, inv, bd, gqa, gka, gqb, gkb, gkv, gkidx,
              *, tm, tk, idx_precision):
    t, d = x2.shape
    row = lambda w: pl.BlockSpec((tm, w), lambda i: (i, 0))
    fulls = [gmix, wmain, widx, wkvup]
    gains = [inv, bd, gqa, gka, gqb, gkb, gkv, gkidx]
    in_specs = ([row(d)] + [_full_spec(a) for a in fulls] + [pl.BlockSpec((1, 1, tm), lambda i: (i, 0, 0))]
                + [_full_spec(a) for a in gains])
    idx_dt = BF16 if idx_precision is None else F32
    sds = jax.ShapeDtypeStruct
    vt_shape, vt_spec = sds((t // tk, 512, tk), BF16), pl.BlockSpec((tm // tk, 512, tk), lambda i: (i, 0, 0))
    out_shape = ([sds((t, 512), BF16)] * 2 + [vt_shape] + [sds((t, 512), BF16)] * 2
                 + [vt_shape, sds((t, 256), idx_dt), sds((t, LANES), idx_dt), sds((t, LANES), F32)])
    out_specs = [row(512)] * 2 + [vt_spec] + [row(512)] * 2 + [vt_spec, row(256), row(LANES), row(LANES)]
    return pl.pallas_call(
        functools.partial(_in0_body, idx_precision=idx_precision),
        grid=(t // tm,),
        in_specs=in_specs,
        out_specs=out_specs,
        out_shape=out_shape,
        compiler_params=_cparams("parallel"),
        name="in0",
    )(x2, gmix, wmain, widx, wkvup, pos, inv, bd, gqa, gka, gqb, gkb, gkv, gkidx)


def _diff_attn_body(lam_ref, q_ref, k_ref, vt_ref, gsub_ref, o_ref, s_ref, pe_ref, *, tq, tk, out_scale):
    i = pl.program_id(2)
    q0 = i * tq
    n_full = (q0 + 1) // tk
    nkc = (q0 + tq + tk - 1) // tk
    lane = lax.broadcasted_iota(jnp.int32, (1, LANES), 1)
    lane_lo = lane < HEAD_DIM
    qpos = q0 + lax.broadcasted_iota(jnp.int32, (1, tq), 1)
    qpos2 = jnp.concatenate([qpos, qpos], axis=1)
    neg_inf = jnp.float32(-jnp.inf)
    heads = range(q_ref.shape[1] // LANES)
    q_cat = []
    for g in heads:
        q_h = q_ref[:, g * LANES:(g + 1) * LANES]
        q_cat.append(jnp.concatenate([q_h * lane_lo.astype(BF16), q_h * (~lane_lo).astype(BF16)], axis=0))

    def logit_step(g, c, m, masked):
        s = lax.dot_general(k_ref[pl.ds(pl.multiple_of(c * tk, tk), tk), g * LANES:(g + 1) * LANES],
                            q_cat[g], NT_DIMS, preferred_element_type=F32)
        if masked:
            kpos = c * tk + lax.broadcasted_iota(jnp.int32, (tk, 1), 0)
            s = jnp.where(kpos <= qpos2, s, neg_inf)
        s_ref[g % s_ref.shape[0], c] = s
        return jnp.maximum(m, jnp.max(s.reshape(tk // SUBLANES, SUBLANES, 2 * tq), axis=0))

    def pv_matmul(g, c):
        return jnp.dot(vt_ref[c, g * LANES:(g + 1) * LANES, :], pe_ref[g % pe_ref.shape[0]],
                       preferred_element_type=F32)

    def exp_step(g, c, shift, l8):
        pe = jnp.exp2(s_ref[g % s_ref.shape[0], c] - shift)
        pe_ref[g % pe_ref.shape[0]] = pe.astype(BF16)
        return l8 + jnp.sum(pe.reshape(tk // SUBLANES, SUBLANES, 2 * tq), axis=0)

    def pv_step(g, c, shift, l8, acc):
        pe = jnp.exp2(s_ref[g % s_ref.shape[0], c] - shift)
        acc = acc + jnp.dot(vt_ref[c, g * LANES:(g + 1) * LANES, :], pe.astype(BF16),
                            preferred_element_type=F32)
        return l8 + jnp.sum(pe.reshape(tk // SUBLANES, SUBLANES, 2 * tq), axis=0), acc

    def run(body, init, by_twos):
        loop = _loop_by_twos if by_twos else lax.fori_loop
        carry = loop(0, n_full, functools.partial(body, masked=False), init)
        return lax.fori_loop(n_full, nkc, functools.partial(body, masked=True), carry)

    lam = lam_ref[0]
    m_init = jnp.full((SUBLANES, 2 * tq), neg_inf, F32)
    pv_init = (jnp.zeros((SUBLANES, 2 * tq), F32), jnp.zeros((LANES, 2 * tq), F32))

    def finish(g, l8, acc):
        o_all = acc / jnp.sum(l8, axis=0, keepdims=True)
        o = o_all[:, 0:tq] - lam * o_all[:, tq:2 * tq]
        o = o * lax.rsqrt(jnp.mean(o * o, axis=0, keepdims=True) + EPS)
        o_ref[:, g * LANES:(g + 1) * LANES] = (o.T * gsub_ref[...] * out_scale).astype(o_ref.dtype)

    width = s_ref.shape[0] // 2
    stages = [heads[lo:lo + width] for lo in range(0, len(heads), width)]

    def logit_stage(stage, c, ms, masked):
        return tuple(logit_step(g, c, m, masked) for g, m in zip(stage, ms))

    def pv_stage(stage, c, shifts, carry):
        return tuple(pv_step(g, c, sh, *lc) for g, sh, lc in zip(stage, shifts, carry))

    ms = run(lambda c, ms, masked: logit_stage(stages[0], c, ms, masked), (m_init,) * width, True)
    for prev, cur in zip(stages[:-1], stages[1:]):
        shifts = [jnp.max(m, axis=0, keepdims=True) for m in ms]

        def fused(c, carry, masked, prev=prev, cur=cur, shifts=shifts):
            pv, ms = carry
            return pv_stage(prev, c, shifts, pv), logit_stage(cur, c, ms, masked)

        pv, ms = run(fused, ((pv_init,) * width, (m_init,) * width), True)
        for g, lc in zip(prev, pv):
            finish(g, *lc)
    shifts = [jnp.max(m, axis=0, keepdims=True) for m in ms]

    last = stages[-1]

    def last_chunk(c, carry):
        l8s, accs = carry
        accs = tuple(acc + pv_matmul(g, c - 1) for g, acc in zip(last, accs))
        return tuple(exp_step(g, c, sh, l8) for g, sh, l8 in zip(last, shifts, l8s)), accs

    l8s = tuple(exp_step(g, 0, sh, pv_init[0]) for g, sh in zip(last, shifts))
    l8s, accs = lax.fori_loop(1, nkc, last_chunk, (l8s, (pv_init[1],) * width))
    for g, l8, acc in zip(last, l8s, accs):
        finish(g, l8, acc + pv_matmul(g, nkc - 1))


def _diff_attn_call(lam, qa, ka, vat, gsub, *, batch, seq, tq, tk, group, out_scale):
    t = qa.shape[0]
    w = group * LANES
    nq, nk = seq // tq, seq // tk
    q_spec = pl.BlockSpec((tq, w), lambda b, h, i: (b * nq + i, h))
    return pl.pallas_call(
        functools.partial(_diff_attn_body, tq=tq, tk=tk, out_scale=out_scale),
        grid=(batch, qa.shape[1] // w, nq),
        in_specs=[pl.BlockSpec(memory_space=pltpu.SMEM), q_spec,
                  pl.BlockSpec((seq, w), lambda b, h, i: (b, h)),
                  pl.BlockSpec((nk, w, tk), lambda b, h, i: (b, h, 0)), _full_spec(gsub)],
        out_specs=q_spec,
        out_shape=jax.ShapeDtypeStruct((t, qa.shape[1]), BF16),
        scratch_shapes=[pltpu.VMEM((group, nk, tk, 2 * tq), F32), pltpu.VMEM((group // 2, tk, 2 * tq), BF16)],
        compiler_params=_cparams("parallel", "parallel", "arbitrary"),
        name="diff_attn",
    )(lam, qa, ka, vat, gsub)


def _loop_by_twos(lo, hi, step, carry):
    pairs = (hi - lo) // 2
    carry = lax.fori_loop(0, pairs, lambda i, cr: step(lo + 2 * i + 1, step(lo + 2 * i, cr)), carry)
    return lax.fori_loop(lo + 2 * pairs, hi, step, carry)


def _key_to_float(key):
    bits = key ^ ((key >> 31) & jnp.int32(0x7FFFFFFF))
    return lax.bitcast_convert_type(bits, F32)


def _dsa_body(qi_ref, wi_ref, qb_ref, kik_ref, kb_ref, vbt_ref, tri_ref, o_ref, sc_ref, bias_ref, s_ref, pe_ref,
              *, tq, tk, k_sel, idx_precision):
    i = pl.program_id(1)
    q0 = i * tq
    nkc = (q0 + tq + tk - 1) // tk
    qpos = q0 + lax.broadcasted_iota(jnp.int32, (1, tq), 1)
    lane = lax.broadcasted_iota(jnp.int32, (1, LANES), 1)
    lane_lo = lane < HEAD_DIM
    tiles = tk // LANES
    neg_inf = jnp.float32(-jnp.inf)

    wi_t = wi_ref[...].T
    qi = qi_ref[...]
    q_heads = []
    for h in range(IDX_HEADS):
        pair = qi[:, (h // 2) * LANES:(h // 2 + 1) * LANES]
        q_heads.append(pair * (lane_lo if h % 2 == 0 else ~lane_lo).astype(pair.dtype))
    qi_cat = jnp.concatenate(q_heads, axis=0)
    w_heads = [wi_t[IDX_DIM + h:IDX_DIM + h + 1, :] for h in range(IDX_HEADS)]

    def score_chunk(c, carry):
        kk = kik_ref[pl.ds(pl.multiple_of(c * tk, tk), tk), :]
        s = lax.dot_general(kk, qi_cat, NT_DIMS, preferred_element_type=F32, precision=idx_precision)
        acc = jnp.zeros((tk, tq), F32)
        for h in range(IDX_HEADS):
            acc = acc + jnp.maximum(s[:, h * tq:(h + 1) * tq], 0.0) * w_heads[h]
        kpos = c * tk + lax.broadcasted_iota(jnp.int32, (tk, 1), 0)
        sc_ref[c] = jnp.where(kpos <= qpos, acc, neg_inf)
        return carry

    _loop_by_twos(0, nkc, score_chunk, 0)

    acc_rows = 4 * SUBLANES

    def count_ge(cand):
        def body(c, acc):
            for r in range(tk // acc_rows):
                acc = acc + jnp.where(sc_ref[c, r * acc_rows:(r + 1) * acc_rows, :] >= cand, 1.0, 0.0)
            return acc
        acc = lax.fori_loop(0, nkc, body, jnp.zeros((acc_rows, tq), F32))
        return jnp.sum(acc, axis=0, keepdims=True)

    ksel = float(k_sel)

    def count_at(cand_key):
        return jnp.where(cand_key <= KEY_NEG_INF, ksel, count_ge(_key_to_float(cand_key)))

    zero_key = jnp.zeros((1, tq), jnp.int32)
    thr = jnp.where(count_at(zero_key) >= ksel, zero_key, INT32_MIN)

    def bit_step(b, thr):
        cand = thr + jnp.left_shift(jnp.int32(1), 30 - b)
        return jnp.where(count_at(cand) >= ksel, cand, thr)

    thr = lax.fori_loop(0, 31, bit_step, thr)
    thr_f = _key_to_float(thr)
    above_f = _key_to_float(thr + 1)
    need = ksel - count_ge(above_f)
    tri = tri_ref[...]

    def select_chunk(c, taken):
        for t in range(tiles):
            st = sc_ref[c, t * LANES:(t + 1) * LANES, :]
            is_above = st >= above_f
            tied = jnp.where(is_above, 0.0, jnp.where(st >= thr_f, 1.0, 0.0))
            prefix = jnp.dot(tri, tied.astype(BF16), preferred_element_type=F32)
            tie_sel = jnp.where((taken + prefix) <= need, tied, 0.0)
            sel = jnp.where(is_above, 0.0, jnp.where(tie_sel > 0.0, 0.0, neg_inf))
            kpos = c * tk + t * LANES + lax.broadcasted_iota(jnp.int32, (LANES, 1), 0)
            bias_ref[c, t * LANES:(t + 1) * LANES, :] = jnp.where(kpos <= qpos, sel, neg_inf)
            taken = taken + prefix[LANES - 1:LANES, :]
        return taken

    lax.fori_loop(0, nkc, select_chunk, jnp.zeros((1, tq), F32))

    group = s_ref.shape[0]
    for p0 in range(0, qb_ref.shape[1] // LANES, group):
        pairs = range(p0, p0 + group)
        q_cat = []
        for p in pairs:
            q_pair = qb_ref[:, p * LANES:(p + 1) * LANES]
            q_cat.append(jnp.concatenate([q_pair * lane_lo.astype(BF16), q_pair * (~lane_lo).astype(BF16)],
                                         axis=0))

        def logit_chunk(c, carry, pairs=pairs, q_cat=q_cat):
            bias = bias_ref[c]
            bias2 = jnp.concatenate([bias, bias], axis=1)
            out = []
            for g, (p, m) in enumerate(zip(pairs, carry)):
                kc = kb_ref[pl.ds(pl.multiple_of(c * tk, tk), tk), p * LANES:(p + 1) * LANES]
                s = lax.dot_general(kc, q_cat[g], NT_DIMS, preferred_element_type=F32) + bias2
                s_ref[g, c] = s
                out.append(jnp.maximum(m, jnp.max(s.reshape(tk // SUBLANES, SUBLANES, 2 * tq), axis=0)))
            return tuple(out)

        m8 = _loop_by_twos(0, nkc, logit_chunk,
                           tuple(jnp.full((SUBLANES, 2 * tq), neg_inf, F32) for _ in pairs))
        shift = [jnp.max(m, axis=0, keepdims=True) for m in m8]

        def exp_stage(c, l8s, shift=shift):
            out = []
            for g, l8 in enumerate(l8s):
                pe = jnp.exp2(s_ref[g, c] - shift[g])
                pe_ref[g] = pe.astype(BF16)
                out.append(l8 + jnp.sum(pe.reshape(tk // SUBLANES, SUBLANES, 2 * tq), axis=0))
            return tuple(out)

        def pv_stage(c, accs, pairs=pairs):
            return tuple(acc + jnp.dot(vbt_ref[c, p * LANES:(p + 1) * LANES, :], pe_ref[g],
                                       preferred_element_type=F32)
                         for g, (p, acc) in enumerate(zip(pairs, accs)))

        def pv_chunk(c, carry):
            l8s, accs = carry
            accs = pv_stage(c - 1, accs)
            return exp_stage(c, l8s), accs

        l8s = exp_stage(0, tuple(jnp.zeros((SUBLANES, 2 * tq), F32) for _ in pairs))
        l8s, accs = lax.fori_loop(1, nkc, pv_chunk,
                                  (l8s, tuple(jnp.zeros((LANES, 2 * tq), F32) for _ in pairs)))
        accs = pv_stage(nkc - 1, accs)
        for p, l8, acc in zip(pairs, l8s, accs):
            o_all = acc / jnp.sum(l8, axis=0, keepdims=True)
            o_t = jnp.concatenate([o_all[0:HEAD_DIM, 0:tq], o_all[HEAD_DIM:LANES, tq:2 * tq]], axis=0)
            o_ref[:, p * LANES:(p + 1) * LANES] = o_t.T.astype(o_ref.dtype)


def _dsa_call(qi, wi, qb, kik, kb, vbt, tri, *, batch, seq, tq, tk, k_sel, idx_precision):
    t = qb.shape[0]
    nq, nk = seq // tq, seq // tk
    row = lambda w: pl.BlockSpec((tq, w), lambda b, i: (b * nq + i, 0))
    seqb = lambda w: pl.BlockSpec((seq, w), lambda b, i: (b, 0))
    vbt_spec = pl.BlockSpec((nk,) + vbt.shape[1:], lambda b, i: (b, 0, 0))
    return pl.pallas_call(
        functools.partial(_dsa_body, tq=tq, tk=tk, k_sel=k_sel, idx_precision=idx_precision),
        grid=(batch, nq),
        in_specs=[row(qi.shape[1]), row(LANES), row(qb.shape[1]), seqb(LANES), seqb(kb.shape[1]),
                  vbt_spec, _full_spec(tri)],
        out_specs=row(qb.shape[1]),
        out_shape=jax.ShapeDtypeStruct((t, qb.shape[1]), BF16),
        scratch_shapes=[pltpu.VMEM((nk, tk, tq), F32), pltpu.VMEM((nk, tk, tq), F32),
                        pltpu.VMEM((2, nk, tk, 2 * tq), F32), pltpu.VMEM((2, tk, 2 * tq), BF16)],
        compiler_params=_cparams("parallel", "arbitrary"),
        name="dsa",
    )(qi, wi, qb, kik, kb, vbt, tri)


def _memkv_body(mem_ref, g_ref, w_ref, gk_ref, k_ref, v_ref):
    hm = _rms(mem_ref[...], g_ref[...]).astype(BF16)
    kv = jnp.dot(hm, w_ref[...], preferred_element_type=F32)
    half = kv.shape[1] // 2
    for h in range(X_HEADS):
        kh = kv[:, h * X_HEAD_DIM:(h + 1) * X_HEAD_DIM]
        k_ref[:, h * X_HEAD_DIM:(h + 1) * X_HEAD_DIM] = _rms(kh, gk_ref[...]).astype(BF16)
    v_ref[...] = kv[:, half:].astype(BF16)


def _memkv_call(mem2, g, w, gk):
    rows, d = mem2.shape
    half = w.shape[1] // 2
    tm = min(rows, 512)
    row = lambda wd: pl.BlockSpec((tm, wd), lambda i: (i, 0))
    return pl.pallas_call(
        _memkv_body,
        grid=(rows // tm,),
        in_specs=[row(d), _full_spec(g), _full_spec(w), _full_spec(gk)],
        out_specs=[row(half), row(half)],
        out_shape=[jax.ShapeDtypeStruct((rows, half), BF16)] * 2,
        compiler_params=_cparams("parallel"),
        name="memkv",
    )(mem2, g, w, gk)


def _xattn_tail(x1, gx_ref, wq_ref, gq_ref, kx_ref, vx_ref, wo_ref):
    hx = _rms(x1, gx_ref[...]).astype(BF16)
    q = jnp.dot(hx, wq_ref[...], preferred_element_type=F32)
    outs = []
    for h in range(X_HEADS):
        cols = slice(h * X_HEAD_DIM, (h + 1) * X_HEAD_DIM)
        qh = _rms(q[:, cols], gq_ref[...]).astype(BF16)
        s = lax.dot_general(qh, kx_ref[0, :, cols], NT_DIMS, preferred_element_type=F32)
        p = jnp.exp(s - jnp.max(s, axis=-1, keepdims=True))
        o = (jnp.dot(p.astype(BF16), vx_ref[0, :, cols], preferred_element_type=F32)
             / jnp.sum(p, axis=-1, keepdims=True))
        outs.append(o.astype(BF16))
    o = jnp.concatenate(outs, axis=1)
    return x1 + jnp.dot(o, wo_ref[...], preferred_element_type=F32)


def _out0_body(x_ref, oa_ref, ob_ref, woa_ref, wob_ref, gx_ref, wq_ref, gq_ref, kx_ref, vx_ref, wo_ref,
               o_ref):
    mix = (jnp.dot(oa_ref[...], woa_ref[...], preferred_element_type=F32)
           + jnp.dot(ob_ref[...], wob_ref[...], preferred_element_type=F32))
    o_ref[...] = _xattn_tail(x_ref[...] + mix, gx_ref, wq_ref, gq_ref, kx_ref, vx_ref, wo_ref)


def _out0_call(x2, oa, ob, woa, wob, gx, wq, gq, kx, vx, wo, *, seq, tm):
    t, d = x2.shape
    row = lambda w: pl.BlockSpec((tm, w), lambda i: (i, 0))
    mem_spec = pl.BlockSpec((1,) + kx.shape[1:], lambda i: ((i * tm) // seq, 0, 0))
    fulls = [woa, wob, gx, wq, gq]
    return pl.pallas_call(
        _out0_body,
        grid=(t // tm,),
        in_specs=[row(d), row(oa.shape[1]), row(ob.shape[1])] + [_full_spec(a) for a in fulls]
                 + [mem_spec, mem_spec, _full_spec(wo)],
        out_specs=row(d),
        out_shape=jax.ShapeDtypeStruct((t, d), F32),
        compiler_params=_cparams("parallel"),
        name="out0",
    )(x2, oa, ob, woa, wob, gx, wq, gq, kx, vx, wo)


def _mlp_body(x_ref, g_ref, wup_ref, wdn_ref, o_ref, h_ref, acc_ref):
    j = pl.program_id(1)

    @pl.when(j == 0)
    def _():
        h_ref[...] = _rms(x_ref[...], g_ref[...]).astype(BF16)
        acc_ref[...] = jnp.zeros(acc_ref.shape, F32)

    u = jnp.dot(h_ref[...], wup_ref[...], preferred_element_type=F32)
    a = jnp.square(jnp.maximum(u, 0.0)).astype(BF16)
    acc_ref[...] += jnp.dot(a, wdn_ref[...], preferred_element_type=F32)

    @pl.when(j == pl.num_programs(1) - 1)
    def _():
        o_ref[...] = x_ref[...] + acc_ref[...]


def _mlp_call(x2, g, wup, wdn, layer, *, tm, tf):
    t, d = x2.shape
    ff = wup.shape[2]
    return pl.pallas_call(
        _mlp_body,
        grid=(t // tm, ff // tf),
        in_specs=[pl.BlockSpec((tm, d), lambda i, j: (i, 0)), _full_spec(g),
                  pl.BlockSpec((None, d, tf), lambda i, j: (layer, 0, j)),
                  pl.BlockSpec((None, tf, d), lambda i, j: (layer, j, 0))],
        out_specs=pl.BlockSpec((tm, d), lambda i, j: (i, 0)),
        out_shape=jax.ShapeDtypeStruct((t, d), F32),
        scratch_shapes=[pltpu.VMEM((tm, d), BF16), pltpu.VMEM((tm, d), F32)],
        compiler_params=_cparams("parallel", "arbitrary"),
        name="mlp",
    )(x2, g, wup, wdn)


def _mix1_body(x_ref, xh_ref, gmix_ref, win_ref, cw_ref, wpool_ref, ps_ref, wout_ref,
               gx_ref, wq_ref, gq_ref, kx_ref, vx_ref, wo_ref, o_ref, *, tm, seq):
    i = pl.program_id(0)
    x = x_ref[...]
    xf = jnp.concatenate([xh_ref[...], x], axis=0)
    hf = _rms(xf, gmix_ref[...]).astype(BF16)
    y = jnp.dot(hf, win_ref[...], preferred_element_type=F32)
    cw = y.shape[1] // 4
    pos0 = (i * tm) % seq
    rid = lax.broadcasted_iota(jnp.int32, (tm + POOL_HALO, 1), 0)
    keep = jnp.where(rid >= POOL_HALO, 1.0, jnp.where(pos0 > 0, 1.0, 0.0))

    gb = y[POOL_HALO:, 0:cw]
    u = y[:, cw:2 * cw] * y[:, 2 * cw:3 * cw] * keep
    z = y[:, 3 * cw:4 * cw] * keep

    taps = cw_ref[...]
    conv = taps[CONV_W - 1:CONV_W, :] * u[POOL_HALO:]
    for back in range(1, CONV_W):
        conv = conv + taps[CONV_W - 1 - back:CONV_W - back, :] * pltpu.roll(u, back, axis=0)[POOL_HALO:]
    yc = gb * conv

    pos = pos0 + lax.broadcasted_iota(jnp.int32, (tm, 1), 0)
    gw = cw // len(POOL_WINDOWS)
    parts = []
    for g, w in enumerate(POOL_WINDOWS):
        zg = z[:, g * gw:(g + 1) * gw]
        s = zg
        sh = 1
        while sh < w:
            s = s + pltpu.roll(s, sh, axis=0)
            sh *= 2
        cnt = jnp.minimum(pos + 1, w).astype(F32)
        pooled = s[POOL_HALO:] / cnt - zg[POOL_HALO:]
        parts.append(jnp.dot(pooled.astype(BF16), wpool_ref[g], preferred_element_type=F32))
    yd = jnp.concatenate(parts, axis=1) * ps_ref[...]

    mix = (jnp.dot(yc.astype(BF16), wout_ref[0:cw, :], preferred_element_type=F32)
           + jnp.dot(yd.astype(BF16), wout_ref[cw:2 * cw, :], preferred_element_type=F32))
    o_ref[...] = _xattn_tail(x + mix, gx_ref, wq_ref, gq_ref, kx_ref, vx_ref, wo_ref)


def _mix1_call(x2, gmix, win, cw, wpool, ps, wout, gx, wq, gq, kx, vx, wo, *, seq, tm):
    t, d = x2.shape
    row = pl.BlockSpec((tm, d), lambda i: (i, 0))
    halo = pl.BlockSpec((POOL_HALO, d), lambda i: (jnp.maximum(i * (tm // POOL_HALO) - 1, 0), 0))
    mem_spec = pl.BlockSpec((1,) + kx.shape[1:], lambda i: ((i * tm) // seq, 0, 0))
    fulls = [gmix, win, cw, wpool, ps, wout, gx, wq, gq]
    return pl.pallas_call(
        functools.partial(_mix1_body, tm=tm, seq=seq),
        grid=(t // tm,),
        in_specs=[row, halo] + [_full_spec(a) for a in fulls] + [mem_spec, mem_spec, _full_spec(wo)],
        out_specs=row,
        out_shape=jax.ShapeDtypeStruct((t, d), F32),
        compiler_params=_cparams("parallel"),
        name="mix1",
    )(x2, x2, gmix, win, cw, wpool, ps, wout, gx, wq, gq, kx, vx, wo)


def _row(v, reps=1):
    return jnp.tile(v.astype(F32), reps).reshape(1, -1)


def kernel(x, mem, positions, g_mix, g_xattn, g_mem, g_mlp, wq_x, wkv_x, gq_x, gk_x, wo_x, w_up, w_down,
           w_in_e, gq_a, gk_a, lam_q1, lam_k1, lam_q2, lam_k2, g_sub_a, g_kv_b, w_kv_up_b, gq_b, gk_b,
           g_kidx, w_out_e, w_in_o, conv_w, w_pool, pool_scale, w_out_o):
    batch, seq, d = x.shape
    t = batch * seq
    k_sel = min(TOPK_MAX, seq // 4)
    tm = min(1024, seq)
    tq_b, tk_b = min(256, seq), min(512, seq)
    assert seq % tm == 0 and seq % tq_b == 0 and seq % tk_b == 0 and tk_b >= k_sel

    x2 = x.reshape(t, d)
    pos = positions.reshape(t // tm, 1, tm)
    inv_freq = (ROPE_THETA ** (-jnp.arange(0, ROT_DIM, 2, dtype=F32) / ROT_DIM)).reshape(ROT_HALF, 1)
    mem2 = mem.reshape(batch * mem.shape[1], d)
    depth = g_mix.shape[0]
    w_up_b, w_down_b = w_up.astype(BF16), w_down.astype(BF16)

    for layer in range(depth):
        j = layer // 2
        kx, vx = _memkv_call(mem2, _row(g_mem[layer]), wkv_x[layer].astype(BF16), _row(gk_x[layer]))
        kx = kx.reshape(batch, mem.shape[1], -1)
        vx = vx.reshape(batch, mem.shape[1], -1)
        tail = (_row(g_xattn[layer]), wq_x[layer].astype(BF16), _row(gq_x[layer]) * (X_HEAD_DIM ** -0.5),
                kx, vx, wo_x[layer].astype(BF16))
        if layer % 2 == 0:
            lam_init = 0.8 - 0.6 * float(np.exp(-0.3 * layer))
            lam = (jnp.exp(jnp.sum(lam_q1[j].astype(F32) * lam_k1[j].astype(F32)))
                   - jnp.exp(jnp.sum(lam_q2[j].astype(F32) * lam_k2[j].astype(F32))) + lam_init)
            w_in = w_in_e[j]
            n_main = 2304
            n_idx = w_in.shape[1] - n_main
            widx = jnp.pad(w_in[:, n_main:], ((0, 0), (0, 3 * LANES - n_idx)))
            if IDX_PRECISION is None:
                widx = widx.astype(BF16)
            heads8 = 512 // HEAD_DIM
            bd = jnp.kron(jnp.eye(heads8, dtype=F32), jnp.ones((HEAD_DIM, HEAD_DIM), F32)).astype(BF16)
            gkidx = jnp.pad(g_kidx[j].astype(F32), (0, LANES - IDX_DIM)).reshape(1, LANES)
            qa, ka, vat, qb, kb, vbt, qi, kik, wi = _in0_call(
                x2, _row(g_mix[layer]), w_in[:, :n_main].astype(BF16), widx, w_kv_up_b[j].astype(BF16),
                pos, inv_freq, bd, _row(gq_a[j], heads8), _row(gk_a[j], heads8), _row(gq_b[j], heads8),
                _row(gk_b[j], heads8), _row(g_kv_b[j]), gkidx, tm=tm, tk=tk_b, idx_precision=IDX_PRECISION)
            out_a = _diff_attn_call(lam.reshape(1).astype(F32), qa, ka, vat, _row(g_sub_a[j]),
                                    batch=batch, seq=seq, tq=tq_b, tk=tk_b, group=qa.shape[1] // LANES, out_scale=1.0 - lam_init)
            tri = jnp.tril(jnp.ones((LANES, LANES), F32)).astype(BF16)
            out_b = _dsa_call(qi, wi, qb, kik, kb, vbt, tri, batch=batch, seq=seq, tq=tq_b, tk=tk_b,
                              k_sel=k_sel, idx_precision=IDX_PRECISION)
            w_out = w_out_e[j].astype(BF16)
            x2 = _out0_call(x2, out_a, out_b, w_out[:512], w_out[512:], *tail, seq=seq, tm=tm)
        else:
            taps = jnp.pad(conv_w[j].astype(F32), ((0, SUBLANES - CONV_W), (0, 0)))
            x2 = _mix1_call(x2, _row(g_mix[layer]), w_in_o[j].astype(BF16), taps, w_pool[j].astype(BF16),
                            _row(pool_scale[j]), w_out_o[j].astype(BF16), *tail, seq=seq, tm=tm)
        x2 = _mlp_call(x2, _row(g_mlp[layer]), w_up_b, w_down_b, layer, tm=min(1024, t), tf=1024)
    return x2.reshape(batch, seq, d)
```

```python
import functools

import numpy as np
import jax
import jax.numpy as jnp
from jax import lax
from jax.experimental import pallas as pl
from jax.experimental.pallas import tpu as pltpu

F32 = jnp.float32
BF16 = jnp.bfloat16

HEAD_DIM = 64
ROT_DIM = HEAD_DIM // 4
ROT_HALF = ROT_DIM // 2
ROPE_THETA = 500000.0
IDX_HEADS = 4
IDX_DIM = 64
TOPK_MAX = 256
CONV_W = 3
POOL_WINDOWS = (2, 4, 8, 16)
POOL_HALO = 16
X_HEADS = 4
X_HEAD_DIM = 128
EPS = 1e-6
LOG2_E = 1.4426950408889634

LANES = 128
SUBLANES = 8
VMEM_LIMIT_BYTES = 56 * 1024 * 1024

KEY_NEG_INF = np.int32(-2139095041)
INT32_MIN = np.int32(-(2 ** 31))

NT_DIMS = (((1,), (1,)), ((), ()))

IDX_PRECISION = None


def _cparams(*sem):
    return pltpu.CompilerParams(dimension_semantics=sem, vmem_limit_bytes=VMEM_LIMIT_BYTES)


def _full_spec(arr):
    nd = arr.ndim
    return pl.BlockSpec(arr.shape, lambda *_: (0,) * nd)


def _rms(x, g):
    ms = jnp.mean(x * x, axis=-1, keepdims=True)
    return x * lax.rsqrt(ms + EPS) * g


def _tile_lanes(t, width):
    reps = width // t.shape[-1]
    return t if reps == 1 else jnp.concatenate([t] * reps, axis=1)


def _rope_tables(pos_row, inv_col):
    ang = inv_col * pos_row
    cos, sin = jnp.cos(ang), jnp.sin(ang)
    rest = (HEAD_DIM - ROT_DIM, ang.shape[1])
    head_c = jnp.concatenate([cos, cos, jnp.ones(rest, F32)], axis=0)
    head_s = jnp.concatenate([-sin, sin, jnp.zeros(rest, F32)], axis=0)
    return (jnp.concatenate([head_c, head_c], axis=0).T, jnp.concatenate([head_s, head_s], axis=0).T)


def _rope(y, c, s):
    w = y.shape[-1]
    c, s = _tile_lanes(c, w), _tile_lanes(s, w)
    lane = lax.broadcasted_iota(jnp.int32, (1, w), 1)
    upper = pltpu.roll(y, w - ROT_HALF, axis=1)
    lower = pltpu.roll(y, ROT_HALF, axis=1)
    partner = jnp.where((lane & (HEAD_DIM - 1)) < ROT_HALF, upper, lower)
    return y * c + partner * s


def _headnorm64(y, blockdiag, g):
    ss = jnp.dot((y * y).astype(BF16), blockdiag, preferred_element_type=F32)
    return y * lax.rsqrt(ss * (1.0 / HEAD_DIM) + EPS) * g


def _in0_body(x_ref, gmix_ref, wmain_ref, widx_ref, wkvup_ref, pos_ref, inv_ref,
              bd_ref, gqa_ref, gka_ref, gqb_ref, gkb_ref, gkv_ref, gkidx_ref,
              qa_ref, ka_ref, vat_ref, qb_ref, kb_ref, vbt_ref, qi_ref, kik_ref, wi_ref,
              *, idx_precision):
    h = _rms(x_ref[...], gmix_ref[...])
    hb = h.astype(BF16)
    proj = lambda lo, hi: jnp.dot(hb, wmain_ref[:, lo:hi], preferred_element_type=F32)
    rot = _rope_tables(pos_ref[0].astype(F32), inv_ref[...])
    bd = bd_ref[...]
    scale = HEAD_DIM ** -0.5 * LOG2_E

    qa = _rope(_headnorm64(proj(0, 512), bd, gqa_ref[...]), *rot) * scale
    qa_ref[...] = qa.astype(BF16)
    ka = _rope(_headnorm64(proj(512, 1024), bd, gka_ref[...]), *rot)
    ka_ref[...] = ka.astype(BF16)
    tk = vat_ref.shape[2]
    va = proj(1024, 1536)
    for r in range(vat_ref.shape[0]):
        vat_ref[r] = va[r * tk:(r + 1) * tk, :].T.astype(BF16)
    qb = _rope(_headnorm64(proj(1536, 2048), bd, gqb_ref[...]), *rot) * scale
    qb_ref[...] = qb.astype(BF16)

    ckv = _rms(proj(2048, 2304), gkv_ref[...]).astype(BF16)
    kv = jnp.dot(ckv, wkvup_ref[...], preferred_element_type=F32)
    kb = _rope(_headnorm64(kv[:, 0:512], bd, gkb_ref[...]), *rot)
    kb_ref[...] = kb.astype(BF16)
    for r in range(vbt_ref.shape[0]):
        vbt_ref[r] = kv[r * tk:(r + 1) * tk, 512:1024].T.astype(BF16)

    if idx_precision is None:
        yi = jnp.dot(hb, widx_ref[...], preferred_element_type=F32)
    else:
        yi = jnp.dot(h, widx_ref[...], preferred_element_type=F32, precision=idx_precision)
    qi_ref[...] = _rope(yi[:, 0:256], *rot).astype(qi_ref.dtype)
    blk = yi[:, 256:384]
    lane = lax.broadcasted_iota(jnp.int32, blk.shape, 1)
    ms = jnp.sum(jnp.where(lane < IDX_DIM, blk * blk, 0.0), axis=-1, keepdims=True) * (1.0 / IDX_DIM)
    kin = blk * lax.rsqrt(ms + EPS) * gkidx_ref[...]
    kir = _rope(kin, *rot)
    kik = kir + pltpu.roll(kir, IDX_DIM, axis=1)
    kik_ref[...] = kik.astype(kik_ref.dtype)
    wi_ref[...] = blk * (IDX_HEADS ** -0.5 * IDX_DIM ** -0.5)


def _in0_call(x2, gmix, wmain, widx, wkvup, pos, inv, bd, gqa, gka, gqb, gkb, gkv, gkidx,
              *, tm, tk, idx_precision):
    t, d = x2.shape
    row = lambda w: pl.BlockSpec((tm, w), lambda i: (i, 0))
    fulls = [gmix, wmain, widx, wkvup]
    gains = [inv, bd, gqa, gka, gqb, gkb, gkv, gkidx]
    in_specs = ([row(d)] + [_full_spec(a) for a in fulls] + [pl.BlockSpec((1, 1, tm), lambda i: (i, 0, 0))]
                + [_full_spec(a) for a in gains])
    idx_dt = BF16 if idx_precision is None else F32
    sds = jax.ShapeDtypeStruct
    vt_shape, vt_spec = sds((t // tk, 512, tk), BF16), pl.BlockSpec((tm // tk, 512, tk), lambda i: (i, 0, 0))
    out_shape = ([sds((t, 512), BF16)] * 2 + [vt_shape] + [sds((t, 512), BF16)] * 2
                 + [vt_shape, sds((t, 256), idx_dt), sds((t, LANES), idx_dt), sds((t, LANES), F32)])
    out_specs = [row(512)] * 2 + [vt_spec] + [row(512)] * 2 + [vt_spec, row(256), row(LANES), row(LANES)]
    return pl.pallas_call(
        functools.partial(_in0_body, idx_precision=idx_precision),
        grid=(t // tm,),
        in_specs=in_specs,
        out_specs=out_specs,
        out_shape=out_shape,
        compiler_params=_cparams("parallel"),
        name="in0",
    )(x2, gmix, wmain, widx, wkvup, pos, inv, bd, gqa, gka, gqb, gkb, gkv, gkidx)


def _diff_attn_body(lam_ref, q_ref, k_ref, vt_ref, gsub_ref, o_ref, s_ref, pe_ref, *, tq, tk, out_scale):
    i = pl.program_id(2)
    q0 = i * tq
    n_full = (q0 + 1) // tk
    nkc = (q0 + tq + tk - 1) // tk
    lane = lax.broadcasted_iota(jnp.int32, (1, LANES), 1)
    lane_lo = lane < HEAD_DIM
    qpos = q0 + lax.broadcasted_iota(jnp.int32, (1, tq), 1)
    qpos2 = jnp.concatenate([qpos, qpos], axis=1)
    neg_inf = jnp.float32(-jnp.inf)
    heads = range(q_ref.shape[1] // LANES)
    q_cat = []
    for g in heads:
        q_h = q_ref[:, g * LANES:(g + 1) * LANES]
        q_cat.append(jnp.concatenate([q_h * lane_lo.astype(BF16), q_h * (~lane_lo).astype(BF16)], axis=0))

    def logit_step(g, c, m, masked):
        s = lax.dot_general(k_ref[pl.ds(pl.multiple_of(c * tk, tk), tk), g * LANES:(g + 1) * LANES],
                            q_cat[g], NT_DIMS, preferred_element_type=F32)
        if masked:
            kpos = c * tk + lax.broadcasted_iota(jnp.int32, (tk, 1), 0)
            s = jnp.where(kpos <= qpos2, s, neg_inf)
        s_ref[g % s_ref.shape[0], c] = s
        return jnp.maximum(m, jnp.max(s.reshape(tk // SUBLANES, SUBLANES, 2 * tq), axis=0))

    def pv_matmul(g, c):
        return jnp.dot(vt_ref[c, g * LANES:(g + 1) * LANES, :], pe_ref[g % pe_ref.shape[0]],
                       preferred_element_type=F32)

    def exp_step(g, c, shift, l8):
        pe = jnp.exp2(s_ref[g % s_ref.shape[0], c] - shift)
        pe_ref[g % pe_ref.shape[0]] = pe.astype(BF16)
        return l8 + jnp.sum(pe.reshape(tk // SUBLANES, SUBLANES, 2 * tq), axis=0)

    def pv_step(g, c, shift, l8, acc):
        pe = jnp.exp2(s_ref[g % s_ref.shape[0], c] - shift)
        acc = acc + jnp.dot(vt_ref[c, g * LANES:(g + 1) * LANES, :], pe.astype(BF16),
                            preferred_element_type=F32)
        return l8 + jnp.sum(pe.reshape(tk // SUBLANES, SUBLANES, 2 * tq), axis=0), acc

    def run(body, init, by_twos):
        loop = _loop_by_twos if by_twos else lax.fori_loop
        carry = loop(0, n_full, functools.partial(body, masked=False), init)
        return lax.fori_loop(n_full, nkc, functools.partial(body, masked=True), carry)

    lam = lam_ref[0]
    m_init = jnp.full((SUBLANES, 2 * tq), neg_inf, F32)
    pv_init = (jnp.zeros((SUBLANES, 2 * tq), F32), jnp.zeros((LANES, 2 * tq), F32))

    def finish(g, l8, acc):
        o_all = acc / jnp.sum(l8, axis=0, keepdims=True)
        o = o_all[:, 0:tq] - lam * o_all[:, tq:2 * tq]
        o = o * lax.rsqrt(jnp.mean(o * o, axis=0, keepdims=True) + EPS)
        o_ref[:, g * LANES:(g + 1) * LANES] = (o.T * gsub_ref[...] * out_scale).astype(o_ref.dtype)

    width = s_ref.shape[0] // 2
    stages = [heads[lo:lo + width] for lo in range(0, len(heads), width)]

    def logit_stage(stage, c, ms, masked):
        return tuple(logit_step(g, c, m, masked) for g, m in zip(stage, ms))

    def pv_stage(stage, c, shifts, carry):
        return tuple(pv_step(g, c, sh, *lc) for g, sh, lc in zip(stage, shifts, carry))

    ms = run(lambda c, ms, masked: logit_stage(stages[0], c, ms, masked), (m_init,) * width, True)
    for prev, cur in zip(stages[:-1], stages[1:]):
        shifts = [jnp.max(m, axis=0, keepdims=True) for m in ms]

        def fused(c, carry, masked, prev=prev, cur=cur, shifts=shifts):
            pv, ms = carry
            return pv_stage(prev, c, shifts, pv), logit_stage(cur, c, ms, masked)

        pv, ms = run(fused, ((pv_init,) * width, (m_init,) * width), True)
        for g, lc in zip(prev, pv):
            finish(g, *lc)
    shifts = [jnp.max(m, axis=0, keepdims=True) for m in ms]

    last = stages[-1]

    def last_chunk(c, carry):
        l8s, accs = carry
        accs = tuple(acc + pv_matmul(g, c - 1) for g, acc in zip(last, accs))
        return tuple(exp_step(g, c, sh, l8) for g, sh, l8 in zip(last, shifts, l8s)), accs

    l8s = tuple(exp_step(g, 0, sh, pv_init[0]) for g, sh in zip(last, shifts))
    l8s, accs = lax.fori_loop(1, nkc, last_chunk, (l8s, (pv_init[1],) * width))
    for g, l8, acc in zip(last, l8s, accs):
        finish(g, l8, acc + pv_matmul(g, nkc - 1))


def _diff_attn_call(lam, qa, ka, vat, gsub, *, batch, seq, tq, tk, group, out_scale):
    t = qa.shape[0]
    w = group * LANES
    nq, nk = seq // tq, seq // tk
    q_spec = pl.BlockSpec((tq, w), lambda b, h, i: (b * nq + i, h))
    return pl.pallas_call(
        functools.partial(_diff_attn_body, tq=tq, tk=tk, out_scale=out_scale),
        grid=(batch, qa.shape[1] // w, nq),
        in_specs=[pl.BlockSpec(memory_space=pltpu.SMEM), q_spec,
                  pl.BlockSpec((seq, w), lambda b, h, i: (b, h)),
                  pl.BlockSpec((nk, w, tk), lambda b, h, i: (b, h, 0)), _full_spec(gsub)],
        out_specs=q_spec,
        out_shape=jax.ShapeDtypeStruct((t, qa.shape[1]), BF16),
        scratch_shapes=[pltpu.VMEM((group, nk, tk, 2 * tq), F32), pltpu.VMEM((group // 2, tk, 2 * tq), BF16)],
        compiler_params=_cparams("parallel", "parallel", "arbitrary"),
        name="diff_attn",
    )(lam, qa, ka, vat, gsub)


def _loop_by_twos(lo, hi, step, carry):
    pairs = (hi - lo) // 2
    carry = lax.fori_loop(0, pairs, lambda i, cr: step(lo + 2 * i + 1, step(lo + 2 * i, cr)), carry)
    return lax.fori_loop(lo + 2 * pairs, hi, step, carry)


def _key_to_float(key):
    bits = key ^ ((key >> 31) & jnp.int32(0x7FFFFFFF))
    return lax.bitcast_convert_type(bits, F32)


def _dsa_body(qi_ref, wi_ref, qb_ref, kik_ref, kb_ref, vbt_ref, tri_ref, o_ref, sc_ref, bias_ref, s_ref, pe_ref,
              *, tq, tk, k_sel, idx_precision):
    i = pl.program_id(1)
    q0 = i * tq
    nkc = (q0 + tq + tk - 1) // tk
    qpos = q0 + lax.broadcasted_iota(jnp.int32, (1, tq), 1)
    lane = lax.broadcasted_iota(jnp.int32, (1, LANES), 1)
    lane_lo = lane < HEAD_DIM
    tiles = tk // LANES
    neg_inf = jnp.float32(-jnp.inf)

    wi_t = wi_ref[...].T
    qi = qi_ref[...]
    q_heads = []
    for h in range(IDX_HEADS):
        pair = qi[:, (h // 2) * LANES:(h // 2 + 1) * LANES]
        q_heads.append(pair * (lane_lo if h % 2 == 0 else ~lane_lo).astype(pair.dtype))
    qi_cat = jnp.concatenate(q_heads, axis=0)
    w_heads = [wi_t[IDX_DIM + h:IDX_DIM + h + 1, :] for h in range(IDX_HEADS)]

    def score_chunk(c, carry):
        kk = kik_ref[pl.ds(pl.multiple_of(c * tk, tk), tk), :]
        s = lax.dot_general(kk, qi_cat, NT_DIMS, preferred_element_type=F32, precision=idx_precision)
        acc = jnp.zeros((tk, tq), F32)
        for h in range(IDX_HEADS):
            acc = acc + jnp.maximum(s[:, h * tq:(h + 1) * tq], 0.0) * w_heads[h]
        kpos = c * tk + lax.broadcasted_iota(jnp.int32, (tk, 1), 0)
        sc_ref[c] = jnp.where(kpos <= qpos, acc, neg_inf)
        return carry

    _loop_by_twos(0, nkc, score_chunk, 0)

    acc_rows = 4 * SUBLANES

    def count_ge(cand):
        def body(c, acc):
            for r in range(tk // acc_rows):
                acc = acc + jnp.where(sc_ref[c, r * acc_rows:(r + 1) * acc_rows, :] >= cand, 1.0, 0.0)
            return acc
        acc = lax.fori_loop(0, nkc, body, jnp.zeros((acc_rows, tq), F32))
        return jnp.sum(acc, axis=0, keepdims=True)

    ksel = float(k_sel)

    def count_at(cand_key):
        return jnp.where(cand_key <= KEY_NEG_INF, ksel, count_ge(_key_to_float(cand_key)))

    zero_key = jnp.zeros((1, tq), jnp.int32)
    thr = jnp.where(count_at(zero_key) >= ksel, zero_key, INT32_MIN)

    def bit_step(b, thr):
        cand = thr + jnp.left_shift(jnp.int32(1), 30 - b)
        return jnp.where(count_at(cand) >= ksel, cand, thr)

    thr = lax.fori_loop(0, 31, bit_step, thr)
    thr_f = _key_to_float(thr)
    above_f = _key_to_float(thr + 1)
    need = ksel - count_ge(above_f)
    tri = tri_ref[...]

    def select_chunk(c, taken):
        for t in range(tiles):
            st = sc_ref[c, t * LANES:(t + 1) * LANES, :]
            is_above = st >= above_f
            tied = jnp.where(is_above, 0.0, jnp.where(st >= thr_f, 1.0, 0.0))
            prefix = jnp.dot(tri, tied.astype(BF16), preferred_element_type=F32)
            tie_sel = jnp.where((taken + prefix) <= need, tied, 0.0)
            sel = jnp.where(is_above, 0.0, jnp.where(tie_sel > 0.0, 0.0, neg_inf))
            kpos = c * tk + t * LANES + lax.broadcasted_iota(jnp.int32, (LANES, 1), 0)
            bias_ref[c, t * LANES:(t + 1) * LANES, :] = jnp.where(kpos <= qpos, sel, neg_inf)
            taken = taken + prefix[LANES - 1:LANES, :]
        return taken

    _loop_by_twos(0, nkc, select_chunk, jnp.zeros((1, tq), F32))

    group = s_ref.shape[0]
    for p0 in range(0, qb_ref.shape[1] // LANES, group):
        pairs = range(p0, p0 + group)
        q_cat = []
        for p in pairs:
            q_pair = qb_ref[:, p * LANES:(p + 1) * LANES]
            q_cat.append(jnp.concatenate([q_pair * lane_lo.astype(BF16), q_pair * (~lane_lo).astype(BF16)],
                                         axis=0))

        def logit_chunk(c, carry, pairs=pairs, q_cat=q_cat):
            bias = bias_ref[c]
            bias2 = jnp.concatenate([bias, bias], axis=1)
            out = []
            for g, (p, m) in enumerate(zip(pairs, carry)):
                kc = kb_ref[pl.ds(pl.multiple_of(c * tk, tk), tk), p * LANES:(p + 1) * LANES]
                s = lax.dot_general(kc, q_cat[g], NT_DIMS, preferred_element_type=F32) + bias2
                s_ref[g, c] = s
                out.append(jnp.maximum(m, jnp.max(s.reshape(tk // SUBLANES, SUBLANES, 2 * tq), axis=0)))
            return tuple(out)

        m8 = _loop_by_twos(0, nkc, logit_chunk,
                           tuple(jnp.full((SUBLANES, 2 * tq), neg_inf, F32) for _ in pairs))
        shift = [jnp.max(m, axis=0, keepdims=True) for m in m8]

        def exp_stage(c, l8s, shift=shift):
            out = []
            for g, l8 in enumerate(l8s):
                pe = jnp.exp2(s_ref[g, c] - shift[g])
                pe_ref[g] = pe.astype(BF16)
                out.append(l8 + jnp.sum(pe.reshape(tk // SUBLANES, SUBLANES, 2 * tq), axis=0))
            return tuple(out)

        def pv_stage(c, accs, pairs=pairs):
            return tuple(acc + jnp.dot(vbt_ref[c, p * LANES:(p + 1) * LANES, :], pe_ref[g],
                                       preferred_element_type=F32)
                         for g, (p, acc) in enumerate(zip(pairs, accs)))

        def pv_chunk(c, carry):
            l8s, accs = carry
            accs = pv_stage(c - 1, accs)
            return exp_stage(c, l8s), accs

        l8s = exp_stage(0, tuple(jnp.zeros((SUBLANES, 2 * tq), F32) for _ in pairs))
        l8s, accs = lax.fori_loop(1, nkc, pv_chunk,
                                  (l8s, tuple(jnp.zeros((LANES, 2 * tq), F32) for _ in pairs)))
        accs = pv_stage(nkc - 1, accs)
        for p, l8, acc in zip(pairs, l8s, accs):
            o_all = acc / jnp.sum(l8, axis=0, keepdims=True)
            o_t = jnp.concatenate([o_all[0:HEAD_DIM, 0:tq], o_all[HEAD_DIM:LANES, tq:2 * tq]], axis=0)
            o_ref[:, p * LANES:(p + 1) * LANES] = o_t.T.astype(o_ref.dtype)


def _dsa_call(qi, wi, qb, kik, kb, vbt, tri, *, batch, seq, tq, tk, k_sel, idx_precision):
    t = qb.shape[0]
    nq, nk = seq // tq, seq // tk
    row = lambda w: pl.BlockSpec((tq, w), lambda b, i: (b * nq + i, 0))
    seqb = lambda w: pl.BlockSpec((seq, w), lambda b, i: (b, 0))
    vbt_spec = pl.BlockSpec((nk,) + vbt.shape[1:], lambda b, i: (b, 0, 0))
    return pl.pallas_call(
        functools.partial(_dsa_body, tq=tq, tk=tk, k_sel=k_sel, idx_precision=idx_precision),
        grid=(batch, nq),
        in_specs=[row(qi.shape[1]), row(LANES), row(qb.shape[1]), seqb(LANES), seqb(kb.shape[1]),
                  vbt_spec, _full_spec(tri)],
        out_specs=row(qb.shape[1]),
        out_shape=jax.ShapeDtypeStruct((t, qb.shape[1]), BF16),
        scratch_shapes=[pltpu.VMEM((nk, tk, tq), F32), pltpu.VMEM((nk, tk, tq), F32),
                        pltpu.VMEM((2, nk, tk, 2 * tq), F32), pltpu.VMEM((2, tk, 2 * tq), BF16)],
        compiler_params=_cparams("parallel", "arbitrary"),
        name="dsa",
    )(qi, wi, qb, kik, kb, vbt, tri)


def _memkv_body(mem_ref, g_ref, w_ref, gk_ref, k_ref, v_ref):
    hm = _rms(mem_ref[...], g_ref[...]).astype(BF16)
    kv = jnp.dot(hm, w_ref[...], preferred_element_type=F32)
    half = kv.shape[1] // 2
    for h in range(X_HEADS):
        kh = kv[:, h * X_HEAD_DIM:(h + 1) * X_HEAD_DIM]
        k_ref[:, h * X_HEAD_DIM:(h + 1) * X_HEAD_DIM] = _rms(kh, gk_ref[...]).astype(BF16)
    v_ref[...] = kv[:, half:].astype(BF16)


def _memkv_call(mem2, g, w, gk):
    rows, d = mem2.shape
    half = w.shape[1] // 2
    tm = min(rows, 512)
    row = lambda wd: pl.BlockSpec((tm, wd), lambda i: (i, 0))
    return pl.pallas_call(
        _memkv_body,
        grid=(rows // tm,),
        in_specs=[row(d), _full_spec(g), _full_spec(w), _full_spec(gk)],
        out_specs=[row(half), row(half)],
        out_shape=[jax.ShapeDtypeStruct((rows, half), BF16)] * 2,
        compiler_params=_cparams("parallel"),
        name="memkv",
    )(mem2, g, w, gk)


def _xattn_tail(x1, gx_ref, wq_ref, gq_ref, kx_ref, vx_ref, wo_ref):
    hx = _rms(x1, gx_ref[...]).astype(BF16)
    q = jnp.dot(hx, wq_ref[...], preferred_element_type=F32)
    outs = []
    for h in range(X_HEADS):
        cols = slice(h * X_HEAD_DIM, (h + 1) * X_HEAD_DIM)
        qh = _rms(q[:, cols], gq_ref[...]).astype(BF16)
        s = lax.dot_general(qh, kx_ref[0, :, cols], NT_DIMS, preferred_element_type=F32)
        p = jnp.exp(s - jnp.max(s, axis=-1, keepdims=True))
        o = (jnp.dot(p.astype(BF16), vx_ref[0, :, cols], preferred_element_type=F32)
             / jnp.sum(p, axis=-1, keepdims=True))
        outs.append(o.astype(BF16))
    o = jnp.concatenate(outs, axis=1)
    return x1 + jnp.dot(o, wo_ref[...], preferred_element_type=F32)


def _out0_body(x_ref, oa_ref, ob_ref, woa_ref, wob_ref, gx_ref, wq_ref, gq_ref, kx_ref, vx_ref, wo_ref,
               o_ref):
    mix = (jnp.dot(oa_ref[...], woa_ref[...], preferred_element_type=F32)
           + jnp.dot(ob_ref[...], wob_ref[...], preferred_element_type=F32))
    o_ref[...] = _xattn_tail(x_ref[...] + mix, gx_ref, wq_ref, gq_ref, kx_ref, vx_ref, wo_ref)


def _out0_call(x2, oa, ob, woa, wob, gx, wq, gq, kx, vx, wo, *, seq, tm):
    t, d = x2.shape
    row = lambda w: pl.BlockSpec((tm, w), lambda i: (i, 0))
    mem_spec = pl.BlockSpec((1,) + kx.shape[1:], lambda i: ((i * tm) // seq, 0, 0))
    fulls = [woa, wob, gx, wq, gq]
    return pl.pallas_call(
        _out0_body,
        grid=(t // tm,),
        in_specs=[row(d), row(oa.shape[1]), row(ob.shape[1])] + [_full_spec(a) for a in fulls]
                 + [mem_spec, mem_spec, _full_spec(wo)],
        out_specs=row(d),
        out_shape=jax.ShapeDtypeStruct((t, d), F32),
        compiler_params=_cparams("parallel"),
        name="out0",
    )(x2, oa, ob, woa, wob, gx, wq, gq, kx, vx, wo)


def _mlp_body(x_ref, g_ref, wup_ref, wdn_ref, o_ref, h_ref, acc_ref):
    j = pl.program_id(1)

    @pl.when(j == 0)
    def _():
        h_ref[...] = _rms(x_ref[...], g_ref[...]).astype(BF16)
        acc_ref[...] = jnp.zeros(acc_ref.shape, F32)

    u = jnp.dot(h_ref[...], wup_ref[...], preferred_element_type=F32)
    a = jnp.square(jnp.maximum(u, 0.0)).astype(BF16)
    acc_ref[...] += jnp.dot(a, wdn_ref[...], preferred_element_type=F32)

    @pl.when(j == pl.num_programs(1) - 1)
    def _():
        o_ref[...] = x_ref[...] + acc_ref[...]


def _mlp_call(x2, g, wup, wdn, layer, *, tm, tf):
    t, d = x2.shape
    ff = wup.shape[2]
    return pl.pallas_call(
        _mlp_body,
        grid=(t // tm, ff // tf),
        in_specs=[pl.BlockSpec((tm, d), lambda i, j: (i, 0)), _full_spec(g),
                  pl.BlockSpec((None, d, tf), lambda i, j: (layer, 0, j)),
                  pl.BlockSpec((None, tf, d), lambda i, j: (layer, j, 0))],
        out_specs=pl.BlockSpec((tm, d), lambda i, j: (i, 0)),
        out_shape=jax.ShapeDtypeStruct((t, d), F32),
        scratch_shapes=[pltpu.VMEM((tm, d), BF16), pltpu.VMEM((tm, d), F32)],
        compiler_params=_cparams("parallel", "arbitrary"),
        name="mlp",
    )(x2, g, wup, wdn)


def _mix1_body(x_ref, xh_ref, gmix_ref, win_ref, cw_ref, wpool_ref, ps_ref, wout_ref,
               gx_ref, wq_ref, gq_ref, kx_ref, vx_ref, wo_ref, o_ref, *, tm, seq):
    i = pl.program_id(0)
    x = x_ref[...]
    xf = jnp.concatenate([xh_ref[...], x], axis=0)
    hf = _rms(xf, gmix_ref[...]).astype(BF16)
    y = jnp.dot(hf, win_ref[...], preferred_element_type=F32)
    cw = y.shape[1] // 4
    pos0 = (i * tm) % seq
    rid = lax.broadcasted_iota(jnp.int32, (tm + POOL_HALO, 1), 0)
    keep = jnp.where(rid >= POOL_HALO, 1.0, jnp.where(pos0 > 0, 1.0, 0.0))

    gb = y[POOL_HALO:, 0:cw]
    u = y[:, cw:2 * cw] * y[:, 2 * cw:3 * cw] * keep
    z = y[:, 3 * cw:4 * cw] * keep

    taps = cw_ref[...]
    conv = taps[CONV_W - 1:CONV_W, :] * u[POOL_HALO:]
    for back in range(1, CONV_W):
        conv = conv + taps[CONV_W - 1 - back:CONV_W - back, :] * pltpu.roll(u, back, axis=0)[POOL_HALO:]
    yc = gb * conv

    pos = pos0 + lax.broadcasted_iota(jnp.int32, (tm, 1), 0)
    gw = cw // len(POOL_WINDOWS)
    parts = []
    for g, w in enumerate(POOL_WINDOWS):
        zg = z[:, g * gw:(g + 1) * gw]
        s = zg
        sh = 1
        while sh < w:
            s = s + pltpu.roll(s, sh, axis=0)
            sh *= 2
        cnt = jnp.minimum(pos + 1, w).astype(F32)
        pooled = s[POOL_HALO:] / cnt - zg[POOL_HALO:]
        parts.append(jnp.dot(pooled.astype(BF16), wpool_ref[g], preferred_element_type=F32))
    yd = jnp.concatenate(parts, axis=1) * ps_ref[...]

    mix = (jnp.dot(yc.astype(BF16), wout_ref[0:cw, :], preferred_element_type=F32)
           + jnp.dot(yd.astype(BF16), wout_ref[cw:2 * cw, :], preferred_element_type=F32))
    o_ref[...] = _xattn_tail(x + mix, gx_ref, wq_ref, gq_ref, kx_ref, vx_ref, wo_ref)


def _mix1_call(x2, gmix, win, cw, wpool, ps, wout, gx, wq, gq, kx, vx, wo, *, seq, tm):
    t, d = x2.shape
    row = pl.BlockSpec((tm, d), lambda i: (i, 0))
    halo = pl.BlockSpec((POOL_HALO, d), lambda i: (jnp.maximum(i * (tm // POOL_HALO) - 1, 0), 0))
    mem_spec = pl.BlockSpec((1,) + kx.shape[1:], lambda i: ((i * tm) // seq, 0, 0))
    fulls = [gmix, win, cw, wpool, ps, wout, gx, wq, gq]
    return pl.pallas_call(
        functools.partial(_mix1_body, tm=tm, seq=seq),
        grid=(t // tm,),
        in_specs=[row, halo] + [_full_spec(a) for a in fulls] + [mem_spec, mem_spec, _full_spec(wo)],
        out_specs=row,
        out_shape=jax.ShapeDtypeStruct((t, d), F32),
        compiler_params=_cparams("parallel"),
        name="mix1",
    )(x2, x2, gmix, win, cw, wpool, ps, wout, gx, wq, gq, kx, vx, wo)


def _row(v, reps=1):
    return jnp.tile(v.astype(F32), reps).reshape(1, -1)


def kernel(x, mem, positions, g_mix, g_xattn, g_mem, g_mlp, wq_x, wkv_x, gq_x, gk_x, wo_x, w_up, w_down,
           w_in_e, gq_a, gk_a, lam_q1, lam_k1, lam_q2, lam_k2, g_sub_a, g_kv_b, w_kv_up_b, gq_b, gk_b,
           g_kidx, w_out_e, w_in_o, conv_w, w_pool, pool_scale, w_out_o):
    batch, seq, d = x.shape
    t = batch * seq
    k_sel = min(TOPK_MAX, seq // 4)
    tm = min(1024, seq)
    tq_b, tk_b = min(256, seq), min(512, seq)
    assert seq % tm == 0 and seq % tq_b == 0 and seq % tk_b == 0 and tk_b >= k_sel

    x2 = x.reshape(t, d)
    pos = positions.reshape(t // tm, 1, tm)
    inv_freq = (ROPE_THETA ** (-jnp.arange(0, ROT_DIM, 2, dtype=F32) / ROT_DIM)).reshape(ROT_HALF, 1)
    mem2 = mem.reshape(batch * mem.shape[1], d)
    depth = g_mix.shape[0]
    w_up_b, w_down_b = w_up.astype(BF16), w_down.astype(BF16)

    for layer in range(depth):
        j = layer // 2
        kx, vx = _memkv_call(mem2, _row(g_mem[layer]), wkv_x[layer].astype(BF16), _row(gk_x[layer]))
        kx = kx.reshape(batch, mem.shape[1], -1)
        vx = vx.reshape(batch, mem.shape[1], -1)
        tail = (_row(g_xattn[layer]), wq_x[layer].astype(BF16), _row(gq_x[layer]) * (X_HEAD_DIM ** -0.5),
                kx, vx, wo_x[layer].astype(BF16))
        if layer % 2 == 0:
            lam_init = 0.8 - 0.6 * float(np.exp(-0.3 * layer))
            lam = (jnp.exp(jnp.sum(lam_q1[j].astype(F32) * lam_k1[j].astype(F32)))
                   - jnp.exp(jnp.sum(lam_q2[j].astype(F32) * lam_k2[j].astype(F32))) + lam_init)
            w_in = w_in_e[j]
            n_main = 2304
            n_idx = w_in.shape[1] - n_main
            widx = jnp.pad(w_in[:, n_main:], ((0, 0), (0, 3 * LANES - n_idx)))
            if IDX_PRECISION is None:
                widx = widx.astype(BF16)
            heads8 = 512 // HEAD_DIM
            bd = jnp.kron(jnp.eye(heads8, dtype=F32), jnp.ones((HEAD_DIM, HEAD_DIM), F32)).astype(BF16)
            gkidx = jnp.pad(g_kidx[j].astype(F32), (0, LANES - IDX_DIM)).reshape(1, LANES)
            qa, ka, vat, qb, kb, vbt, qi, kik, wi = _in0_call(
                x2, _row(g_mix[layer]), w_in[:, :n_main].astype(BF16), widx, w_kv_up_b[j].astype(BF16),
                pos, inv_freq, bd, _row(gq_a[j], heads8), _row(gk_a[j], heads8), _row(gq_b[j], heads8),
                _row(gk_b[j], heads8), _row(g_kv_b[j]), gkidx, tm=tm, tk=tk_b, idx_precision=IDX_PRECISION)
            out_a = _diff_attn_call(lam.reshape(1).astype(F32), qa, ka, vat, _row(g_sub_a[j]),
                                    batch=batch, seq=seq, tq=tq_b, tk=tk_b, group=qa.shape[1] // LANES, out_scale=1.0 - lam_init)
            tri = jnp.tril(jnp.ones((LANES, LANES), F32)).astype(BF16)
            out_b = _dsa_call(qi, wi, qb, kik, kb, vbt, tri, batch=batch, seq=seq, tq=tq_b, tk=tk_b,
                              k_sel=k_sel, idx_precision=IDX_PRECISION)
            w_out = w_out_e[j].astype(BF16)
            x2 = _out0_call(x2, out_a, out_b, w_out[:512], w_out[512:], *tail, seq=seq, tm=tm)
        else:
            taps = jnp.pad(conv_w[j].astype(F32), ((0, SUBLANES - CONV_W), (0, 0)))
            x2 = _mix1_call(x2, _row(g_mix[layer]), w_in_o[j].astype(BF16), taps, w_pool[j].astype(BF16),
                            _row(pool_scale[j]), w_out_o[j].astype(BF16), *tail, seq=seq, tm=tm)
        x2 = _mlp_call(x2, _row(g_mlp[layer]), w_up_b, w_down_b, layer, tm=min(1024, t), tf=2048)
    return x2.reshape(batch, seq, d)
```

```python
import functools

import numpy as np
import jax
import jax.numpy as jnp
from jax import lax
from jax.experimental import pallas as pl
from jax.experimental.pallas import tpu as pltpu

F32 = jnp.float32
BF16 = jnp.bfloat16

HEAD_DIM = 64
ROT_DIM = HEAD_DIM // 4
ROT_HALF = ROT_DIM // 2
ROPE_THETA = 500000.0
IDX_HEADS = 4
IDX_DIM = 64
TOPK_MAX = 256
CONV_W = 3
POOL_WINDOWS = (2, 4, 8, 16)
POOL_HALO = 16
X_HEADS = 4
X_HEAD_DIM = 128
EPS = 1e-6
LOG2_E = 1.4426950408889634

LANES = 128
SUBLANES = 8
VMEM_LIMIT_BYTES = 56 * 1024 * 1024

KEY_NEG_INF = np.int32(-2139095041)
INT32_MIN = np.int32(-(2 ** 31))

NT_DIMS = (((1,), (1,)), ((), ()))

IDX_PRECISION = None


def _cparams(*sem):
    return pltpu.CompilerParams(dimension_semantics=sem, vmem_limit_bytes=VMEM_LIMIT_BYTES)


def _full_spec(arr):
    nd = arr.ndim
    return pl.BlockSpec(arr.shape, lambda *_: (0,) * nd)


def _rms(x, g):
    ms = jnp.mean(x * x, axis=-1, keepdims=True)
    return x * lax.rsqrt(ms + EPS) * g


def _tile_lanes(t, width):
    reps = width // t.shape[-1]
    return t if reps == 1 else jnp.concatenate([t] * reps, axis=1)


def _rope_tables(pos_row, inv_col):
    ang = inv_col * pos_row
    cos, sin = jnp.cos(ang), jnp.sin(ang)
    rest = (HEAD_DIM - ROT_DIM, ang.shape[1])
    head_c = jnp.concatenate([cos, cos, jnp.ones(rest, F32)], axis=0)
    head_s = jnp.concatenate([-sin, sin, jnp.zeros(rest, F32)], axis=0)
    return (jnp.concatenate([head_c, head_c], axis=0).T, jnp.concatenate([head_s, head_s], axis=0).T)


def _rope(y, c, s):
    w = y.shape[-1]
    c, s = _tile_lanes(c, w), _tile_lanes(s, w)
    lane = lax.broadcasted_iota(jnp.int32, (1, w), 1)
    upper = pltpu.roll(y, w - ROT_HALF, axis=1)
    lower = pltpu.roll(y, ROT_HALF, axis=1)
    partner = jnp.where((lane & (HEAD_DIM - 1)) < ROT_HALF, upper, lower)
    return y * c + partner * s


def _headnorm64(y, blockdiag, g):
    ss = jnp.dot((y * y).astype(BF16), blockdiag, preferred_element_type=F32)
    return y * lax.rsqrt(ss * (1.0 / HEAD_DIM) + EPS) * g


def _in0_body(x_ref, gmix_ref, wmain_ref, widx_ref, wkvup_ref, pos_ref, inv_ref,
              bd_ref, gqa_ref, gka_ref, gqb_ref, gkb_ref, gkv_ref, gkidx_ref,
              qa_ref, ka_ref, vat_ref, qb_ref, kb_ref, vbt_ref, qi_ref, kik_ref, wi_ref,
              *, idx_precision):
    h = _rms(x_ref[...], gmix_ref[...])
    hb = h.astype(BF16)
    proj = lambda lo, hi: jnp.dot(hb, wmain_ref[:, lo:hi], preferred_element_type=F32)
    rot = _rope_tables(pos_ref[0].astype(F32), inv_ref[...])
    bd = bd_ref[...]
    scale = HEAD_DIM ** -0.5 * LOG2_E

    qa = _rope(_headnorm64(proj(0, 512), bd, gqa_ref[...]), *rot) * scale
    qa_ref[...] = qa.astype(BF16)
    ka = _rope(_headnorm64(proj(512, 1024), bd, gka_ref[...]), *rot)
    ka_ref[...] = ka.astype(BF16)
    tk = vat_ref.shape[2]
    va = proj(1024, 1536)
    for r in range(vat_ref.shape[0]):
        vat_ref[r] = va[r * tk:(r + 1) * tk, :].T.astype(BF16)
    qb = _rope(_headnorm64(proj(1536, 2048), bd, gqb_ref[...]), *rot) * scale
    qb_ref[...] = qb.astype(BF16)

    ckv = _rms(proj(2048, 2304), gkv_ref[...]).astype(BF16)
    kv = jnp.dot(ckv, wkvup_ref[...], preferred_element_type=F32)
    kb = _rope(_headnorm64(kv[:, 0:512], bd, gkb_ref[...]), *rot)
    kb_ref[...] = kb.astype(BF16)
    for r in range(vbt_ref.shape[0]):
        vbt_ref[r] = kv[r * tk:(r + 1) * tk, 512:1024].T.astype(BF16)

    if idx_precision is None:
        yi = jnp.dot(hb, widx_ref[...], preferred_element_type=F32)
    else:
        yi = jnp.dot(h, widx_ref[...], preferred_element_type=F32, precision=idx_precision)
    qi_ref[...] = _rope(yi[:, 0:256], *rot).astype(qi_ref.dtype)
    blk = yi[:, 256:384]
    lane = lax.broadcasted_iota(jnp.int32, blk.shape, 1)
    ms = jnp.sum(jnp.where(lane < IDX_DIM, blk * blk, 0.0), axis=-1, keepdims=True) * (1.0 / IDX_DIM)
    kin = blk * lax.rsqrt(ms + EPS) * gkidx_ref[...]
    kir = _rope(kin, *rot)
    kik = kir + pltpu.roll(kir, IDX_DIM, axis=1)
    kik_ref[...] = kik.astype(kik_ref.dtype)
    wi_ref[...] = blk * (IDX_HEADS ** -0.5 * IDX_DIM ** -0.5)


def _in0_call(x2, gmix, wmain, widx, wkvup, pos, inv, bd, gqa, gka, gqb, gkb, gkv, gkidx,
              *, tm, tk, idx_precision):
    t, d = x2.shape
    row = lambda w: pl.BlockSpec((tm, w), lambda i: (i, 0))
    fulls = [gmix, wmain, widx, wkvup]
    gains = [inv, bd, gqa, gka, gqb, gkb, gkv, gkidx]
    in_specs = ([row(d)] + [_full_spec(a) for a in fulls] + [pl.BlockSpec((1, 1, tm), lambda i: (i, 0, 0))]
                + [_full_spec(a) for a in gains])
    idx_dt = BF16 if idx_precision is None else F32
    sds = jax.ShapeDtypeStruct
    vt_shape, vt_spec = sds((t // tk, 512, tk), BF16), pl.BlockSpec((tm // tk, 512, tk), lambda i: (i, 0, 0))
    out_shape = ([sds((t, 512), BF16)] * 2 + [vt_shape] + [sds((t, 512), BF16)] * 2
                 + [vt_shape, sds((t, 256), idx_dt), sds((t, LANES), idx_dt), sds((t, LANES), F32)])
    out_specs = [row(512)] * 2 + [vt_spec] + [row(512)] * 2 + [vt_spec, row(256), row(LANES), row(LANES)]
    return pl.pallas_call(
        functools.partial(_in0_body, idx_precision=idx_precision),
        grid=(t // tm,),
        in_specs=in_specs,
        out_specs=out_specs,
        out_shape=out_shape,
        compiler_params=_cparams("parallel"),
        name="in0",
    )(x2, gmix, wmain, widx, wkvup, pos, inv, bd, gqa, gka, gqb, gkb, gkv, gkidx)


def _diff_attn_body(lam_ref, q_ref, k_ref, vt_ref, gsub_ref, o_ref, s_ref, pe_ref, *, tq, tk, out_scale):
    i = pl.program_id(2)
    q0 = i * tq
    n_full = (q0 + 1) // tk
    nkc = (q0 + tq + tk - 1) // tk
    lane = lax.broadcasted_iota(jnp.int32, (1, LANES), 1)
    lane_lo = lane < HEAD_DIM
    qpos = q0 + lax.broadcasted_iota(jnp.int32, (1, tq), 1)
    qpos2 = jnp.concatenate([qpos, qpos], axis=1)
    neg_inf = jnp.float32(-jnp.inf)
    heads = range(q_ref.shape[1] // LANES)
    q_cat = []
    for g in heads:
        q_h = q_ref[:, g * LANES:(g + 1) * LANES]
        q_cat.append(jnp.concatenate([q_h * lane_lo.astype(BF16), q_h * (~lane_lo).astype(BF16)], axis=0))

    def logit_step(g, c, m, masked):
        s = lax.dot_general(k_ref[pl.ds(pl.multiple_of(c * tk, tk), tk), g * LANES:(g + 1) * LANES],
                            q_cat[g], NT_DIMS, preferred_element_type=F32)
        if masked:
            kpos = c * tk + lax.broadcasted_iota(jnp.int32, (tk, 1), 0)
            s = jnp.where(kpos <= qpos2, s, neg_inf)
        s_ref[g % s_ref.shape[0], c] = s
        return jnp.maximum(m, jnp.max(s.reshape(tk // SUBLANES, SUBLANES, 2 * tq), axis=0))

    def pv_matmul(g, c):
        return jnp.dot(vt_ref[c, g * LANES:(g + 1) * LANES, :], pe_ref[g % pe_ref.shape[0]],
                       preferred_element_type=F32)

    def exp_step(g, c, shift, l8):
        pe = jnp.exp2(s_ref[g % s_ref.shape[0], c] - shift)
        pe_ref[g % pe_ref.shape[0]] = pe.astype(BF16)
        return l8 + jnp.sum(pe.reshape(tk // SUBLANES, SUBLANES, 2 * tq), axis=0)

    def pv_step(g, c, shift, l8, acc):
        pe = jnp.exp2(s_ref[g % s_ref.shape[0], c] - shift)
        acc = acc + jnp.dot(vt_ref[c, g * LANES:(g + 1) * LANES, :], pe.astype(BF16),
                            preferred_element_type=F32)
        return l8 + jnp.sum(pe.reshape(tk // SUBLANES, SUBLANES, 2 * tq), axis=0), acc

    def run(body, init, by_twos):
        loop = _loop_by_twos if by_twos else lax.fori_loop
        carry = loop(0, n_full, functools.partial(body, masked=False), init)
        return lax.fori_loop(n_full, nkc, functools.partial(body, masked=True), carry)

    lam = lam_ref[0]
    m_init = jnp.full((SUBLANES, 2 * tq), neg_inf, F32)
    pv_init = (jnp.zeros((SUBLANES, 2 * tq), F32), jnp.zeros((LANES, 2 * tq), F32))

    def finish(g, l8, acc):
        o_all = acc / jnp.sum(l8, axis=0, keepdims=True)
        o = o_all[:, 0:tq] - lam * o_all[:, tq:2 * tq]
        o = o * lax.rsqrt(jnp.mean(o * o, axis=0, keepdims=True) + EPS)
        o_ref[:, g * LANES:(g + 1) * LANES] = (o.T * gsub_ref[...] * out_scale).astype(o_ref.dtype)

    width = s_ref.shape[0] // 2
    stages = [heads[lo:lo + width] for lo in range(0, len(heads), width)]

    def logit_stage(stage, c, ms, masked):
        return tuple(logit_step(g, c, m, masked) for g, m in zip(stage, ms))

    def pv_stage(stage, c, shifts, carry):
        return tuple(pv_step(g, c, sh, *lc) for g, sh, lc in zip(stage, shifts, carry))

    ms = run(lambda c, ms, masked: logit_stage(stages[0], c, ms, masked), (m_init,) * width, True)
    for prev, cur in zip(stages[:-1], stages[1:]):
        shifts = [jnp.max(m, axis=0, keepdims=True) for m in ms]

        def fused(c, carry, masked, prev=prev, cur=cur, shifts=shifts):
            pv, ms = carry
            return pv_stage(prev, c, shifts, pv), logit_stage(cur, c, ms, masked)

        pv, ms = run(fused, ((pv_init,) * width, (m_init,) * width), True)
        for g, lc in zip(prev, pv):
            finish(g, *lc)
    shifts = [jnp.max(m, axis=0, keepdims=True) for m in ms]

    last = stages[-1]

    def last_chunk(c, carry):
        l8s, accs = carry
        accs = tuple(acc + pv_matmul(g, c - 1) for g, acc in zip(last, accs))
        return tuple(exp_step(g, c, sh, l8) for g, sh, l8 in zip(last, shifts, l8s)), accs

    l8s = tuple(exp_step(g, 0, sh, pv_init[0]) for g, sh in zip(last, shifts))
    l8s, accs = lax.fori_loop(1, nkc, last_chunk, (l8s, (pv_init[1],) * width))
    for g, l8, acc in zip(last, l8s, accs):
        finish(g, l8, acc + pv_matmul(g, nkc - 1))


def _diff_attn_call(lam, qa, ka, vat, gsub, *, batch, seq, tq, tk, group, out_scale):
    t = qa.shape[0]
    w = group * LANES
    nq, nk = seq // tq, seq // tk
    q_spec = pl.BlockSpec((tq, w), lambda b, h, i: (b * nq + i, h))
    return pl.pallas_call(
        functools.partial(_diff_attn_body, tq=tq, tk=tk, out_scale=out_scale),
        grid=(batch, qa.shape[1] // w, nq),
        in_specs=[pl.BlockSpec(memory_space=pltpu.SMEM), q_spec,
                  pl.BlockSpec((seq, w), lambda b, h, i: (b, h)),
                  pl.BlockSpec((nk, w, tk), lambda b, h, i: (b, h, 0)), _full_spec(gsub)],
        out_specs=q_spec,
        out_shape=jax.ShapeDtypeStruct((t, qa.shape[1]), BF16),
        scratch_shapes=[pltpu.VMEM((group, nk, tk, 2 * tq), F32), pltpu.VMEM((group // 2, tk, 2 * tq), BF16)],
        compiler_params=_cparams("parallel", "parallel", "arbitrary"),
        name="diff_attn",
    )(lam, qa, ka, vat, gsub)


def _loop_by_twos(lo, hi, step, carry):
    pairs = (hi - lo) // 2
    carry = lax.fori_loop(0, pairs, lambda i, cr: step(lo + 2 * i + 1, step(lo + 2 * i, cr)), carry)
    return lax.fori_loop(lo + 2 * pairs, hi, step, carry)


def _key_to_float(key):
    bits = key ^ ((key >> 31) & jnp.int32(0x7FFFFFFF))
    return lax.bitcast_convert_type(bits, F32)


def _dsa_body(qi_ref, wi_ref, qb_ref, kik_ref, kb_ref, vbt_ref, tri_ref, o_ref, sc_ref, bias_ref, s_ref, pe_ref,
              *, tq, tk, k_sel, idx_precision):
    i = pl.program_id(1)
    q0 = i * tq
    nkc = (q0 + tq + tk - 1) // tk
    qpos = q0 + lax.broadcasted_iota(jnp.int32, (1, tq), 1)
    lane = lax.broadcasted_iota(jnp.int32, (1, LANES), 1)
    lane_lo = lane < HEAD_DIM
    tiles = tk // LANES
    neg_inf = jnp.float32(-jnp.inf)

    wi_t = wi_ref[...].T
    qi = qi_ref[...]
    q_heads = []
    for h in range(IDX_HEADS):
        pair = qi[:, (h // 2) * LANES:(h // 2 + 1) * LANES]
        q_heads.append(pair * (lane_lo if h % 2 == 0 else ~lane_lo).astype(pair.dtype))
    qi_cat = jnp.concatenate(q_heads, axis=0)
    w_heads = [wi_t[IDX_DIM + h:IDX_DIM + h + 1, :] for h in range(IDX_HEADS)]

    def score_chunk(c, carry):
        kk = kik_ref[pl.ds(pl.multiple_of(c * tk, tk), tk), :]
        s = lax.dot_general(kk, qi_cat, NT_DIMS, preferred_element_type=F32, precision=idx_precision)
        acc = jnp.zeros((tk, tq), F32)
        for h in range(IDX_HEADS):
            acc = acc + jnp.maximum(s[:, h * tq:(h + 1) * tq], 0.0) * w_heads[h]
        kpos = c * tk + lax.broadcasted_iota(jnp.int32, (tk, 1), 0)
        sc_ref[c] = jnp.where(kpos <= qpos, acc, neg_inf)
        return carry

    _loop_by_twos(0, nkc, score_chunk, 0)

    acc_rows = 4 * SUBLANES

    def count_ge(cand):
        def body(c, acc):
            for r in range(tk // acc_rows):
                acc = acc + jnp.where(sc_ref[c, r * acc_rows:(r + 1) * acc_rows, :] >= cand, 1.0, 0.0)
            return acc
        acc = lax.fori_loop(0, nkc, body, jnp.zeros((acc_rows, tq), F32))
        return jnp.sum(acc, axis=0, keepdims=True)

    ksel = float(k_sel)

    def count_at(cand_key):
        return jnp.where(cand_key <= KEY_NEG_INF, ksel, count_ge(_key_to_float(cand_key)))

    zero_key = jnp.zeros((1, tq), jnp.int32)
    cnt = count_at(zero_key)
    thr = jnp.where(cnt >= ksel, zero_key, INT32_MIN)
    n_above = jnp.where(cnt >= ksel, 0.0, cnt)

    def bit_step(b, carry):
        thr, n_above = carry
        cand = thr + jnp.left_shift(jnp.int32(1), 30 - b)
        cnt = count_at(cand)
        return jnp.where(cnt >= ksel, cand, thr), jnp.where(cnt >= ksel, n_above, cnt)

    thr, n_above = lax.fori_loop(0, 31, bit_step, (thr, n_above))
    thr_f = _key_to_float(thr)
    above_f = _key_to_float(thr + 1)
    need = ksel - n_above
    tri = tri_ref[...]

    def select_chunk(c, taken):
        for t in range(tiles):
            st = sc_ref[c, t * LANES:(t + 1) * LANES, :]
            is_above = st >= above_f
            tied = jnp.where(is_above, 0.0, jnp.where(st >= thr_f, 1.0, 0.0))
            prefix = jnp.dot(tri, tied.astype(BF16), preferred_element_type=F32)
            tie_sel = jnp.where((taken + prefix) <= need, tied, 0.0)
            sel = jnp.where(is_above, 0.0, jnp.where(tie_sel > 0.0, 0.0, neg_inf))
            kpos = c * tk + t * LANES + lax.broadcasted_iota(jnp.int32, (LANES, 1), 0)
            bias_ref[c, t * LANES:(t + 1) * LANES, :] = jnp.where(kpos <= qpos, sel, neg_inf)
            taken = taken + prefix[LANES - 1:LANES, :]
        return taken

    _loop_by_twos(0, nkc, select_chunk, jnp.zeros((1, tq), F32))

    group = s_ref.shape[0]
    for p0 in range(0, qb_ref.shape[1] // LANES, group):
        pairs = range(p0, p0 + group)
        q_cat = []
        for p in pairs:
            q_pair = qb_ref[:, p * LANES:(p + 1) * LANES]
            q_cat.append(jnp.concatenate([q_pair * lane_lo.astype(BF16), q_pair * (~lane_lo).astype(BF16)],
                                         axis=0))

        def logit_chunk(c, carry, pairs=pairs, q_cat=q_cat):
            bias = bias_ref[c]
            bias2 = jnp.concatenate([bias, bias], axis=1)
            out = []
            for g, (p, m) in enumerate(zip(pairs, carry)):
                kc = kb_ref[pl.ds(pl.multiple_of(c * tk, tk), tk), p * LANES:(p + 1) * LANES]
                s = lax.dot_general(kc, q_cat[g], NT_DIMS, preferred_element_type=F32) + bias2
                s_ref[g, c] = s
                out.append(jnp.maximum(m, jnp.max(s.reshape(tk // SUBLANES, SUBLANES, 2 * tq), axis=0)))
            return tuple(out)

        m8 = _loop_by_twos(0, nkc, logit_chunk,
                           tuple(jnp.full((SUBLANES, 2 * tq), neg_inf, F32) for _ in pairs))
        shift = [jnp.max(m, axis=0, keepdims=True) for m in m8]

        def exp_stage(c, l8s, shift=shift):
            out = []
            for g, l8 in enumerate(l8s):
                pe = jnp.exp2(s_ref[g, c] - shift[g])
                pe_ref[g] = pe.astype(BF16)
                out.append(l8 + jnp.sum(pe.reshape(tk // SUBLANES, SUBLANES, 2 * tq), axis=0))
            return tuple(out)

        def pv_stage(c, accs, pairs=pairs):
            return tuple(acc + jnp.dot(vbt_ref[c, p * LANES:(p + 1) * LANES, :], pe_ref[g],
                                       preferred_element_type=F32)
                         for g, (p, acc) in enumerate(zip(pairs, accs)))

        def pv_chunk(c, carry):
            l8s, accs = carry
            accs = pv_stage(c - 1, accs)
            return exp_stage(c, l8s), accs

        l8s = exp_stage(0, tuple(jnp.zeros((SUBLANES, 2 * tq), F32) for _ in pairs))
        l8s, accs = lax.fori_loop(1, nkc, pv_chunk,
                                  (l8s, tuple(jnp.zeros((LANES, 2 * tq), F32) for _ in pairs)))
        accs = pv_stage(nkc - 1, accs)
        for p, l8, acc in zip(pairs, l8s, accs):
            o_all = acc / jnp.sum(l8, axis=0, keepdims=True)
            o_t = jnp.concatenate([o_all[0:HEAD_DIM, 0:tq], o_all[HEAD_DIM:LANES, tq:2 * tq]], axis=0)
            o_ref[:, p * LANES:(p + 1) * LANES] = o_t.T.astype(o_ref.dtype)


def _dsa_call(qi, wi, qb, kik, kb, vbt, tri, *, batch, seq, tq, tk, k_sel, idx_precision):
    t = qb.shape[0]
    nq, nk = seq // tq, seq // tk
    row = lambda w: pl.BlockSpec((tq, w), lambda b, i: (b * nq + i, 0))
    seqb = lambda w: pl.BlockSpec((seq, w), lambda b, i: (b, 0))
    vbt_spec = pl.BlockSpec((nk,) + vbt.shape[1:], lambda b, i: (b, 0, 0))
    return pl.pallas_call(
        functools.partial(_dsa_body, tq=tq, tk=tk, k_sel=k_sel, idx_precision=idx_precision),
        grid=(batch, nq),
        in_specs=[row(qi.shape[1]), row(LANES), row(qb.shape[1]), seqb(LANES), seqb(kb.shape[1]),
                  vbt_spec, _full_spec(tri)],
        out_specs=row(qb.shape[1]),
        out_shape=jax.ShapeDtypeStruct((t, qb.shape[1]), BF16),
        scratch_shapes=[pltpu.VMEM((nk, tk, tq), F32), pltpu.VMEM((nk, tk, tq), F32),
                        pltpu.VMEM((2, nk, tk, 2 * tq), F32), pltpu.VMEM((2, tk, 2 * tq), BF16)],
        compiler_params=_cparams("parallel", "arbitrary"),
        name="dsa",
    )(qi, wi, qb, kik, kb, vbt, tri)


def _memkv_body(mem_ref, g_ref, w_ref, gk_ref, k_ref, v_ref):
    hm = _rms(mem_ref[...], g_ref[...]).astype(BF16)
    kv = jnp.dot(hm, w_ref[...], preferred_element_type=F32)
    half = kv.shape[1] // 2
    for h in range(X_HEADS):
        kh = kv[:, h * X_HEAD_DIM:(h + 1) * X_HEAD_DIM]
        k_ref[:, h * X_HEAD_DIM:(h + 1) * X_HEAD_DIM] = _rms(kh, gk_ref[...]).astype(BF16)
    v_ref[...] = kv[:, half:].astype(BF16)


def _memkv_call(mem2, g, w, gk):
    rows, d = mem2.shape
    half = w.shape[1] // 2
    tm = min(rows, 512)
    row = lambda wd: pl.BlockSpec((tm, wd), lambda i: (i, 0))
    return pl.pallas_call(
        _memkv_body,
        grid=(rows // tm,),
        in_specs=[row(d), _full_spec(g), _full_spec(w), _full_spec(gk)],
        out_specs=[row(half), row(half)],
        out_shape=[jax.ShapeDtypeStruct((rows, half), BF16)] * 2,
        compiler_params=_cparams("parallel"),
        name="memkv",
    )(mem2, g, w, gk)


def _xattn_tail(x1, gx_ref, wq_ref, gq_ref, kx_ref, vx_ref, wo_ref):
    hx = _rms(x1, gx_ref[...]).astype(BF16)
    q = jnp.dot(hx, wq_ref[...], preferred_element_type=F32)
    outs = []
    for h in range(X_HEADS):
        cols = slice(h * X_HEAD_DIM, (h + 1) * X_HEAD_DIM)
        qh = _rms(q[:, cols], gq_ref[...]).astype(BF16)
        s = lax.dot_general(qh, kx_ref[0, :, cols], NT_DIMS, preferred_element_type=F32)
        p = jnp.exp(s - jnp.max(s, axis=-1, keepdims=True))
        o = (jnp.dot(p.astype(BF16), vx_ref[0, :, cols], preferred_element_type=F32)
             / jnp.sum(p, axis=-1, keepdims=True))
        outs.append(o.astype(BF16))
    o = jnp.concatenate(outs, axis=1)
    return x1 + jnp.dot(o, wo_ref[...], preferred_element_type=F32)


def _out0_body(x_ref, oa_ref, ob_ref, woa_ref, wob_ref, gx_ref, wq_ref, gq_ref, kx_ref, vx_ref, wo_ref,
               o_ref):
    mix = (jnp.dot(oa_ref[...], woa_ref[...], preferred_element_type=F32)
           + jnp.dot(ob_ref[...], wob_ref[...], preferred_element_type=F32))
    o_ref[...] = _xattn_tail(x_ref[...] + mix, gx_ref, wq_ref, gq_ref, kx_ref, vx_ref, wo_ref)


def _out0_call(x2, oa, ob, woa, wob, gx, wq, gq, kx, vx, wo, *, seq, tm):
    t, d = x2.shape
    row = lambda w: pl.BlockSpec((tm, w), lambda i: (i, 0))
    mem_spec = pl.BlockSpec((1,) + kx.shape[1:], lambda i: ((i * tm) // seq, 0, 0))
    fulls = [woa, wob, gx, wq, gq]
    return pl.pallas_call(
        _out0_body,
        grid=(t // tm,),
        in_specs=[row(d), row(oa.shape[1]), row(ob.shape[1])] + [_full_spec(a) for a in fulls]
                 + [mem_spec, mem_spec, _full_spec(wo)],
        out_specs=row(d),
        out_shape=jax.ShapeDtypeStruct((t, d), F32),
        compiler_params=_cparams("parallel"),
        name="out0",
    )(x2, oa, ob, woa, wob, gx, wq, gq, kx, vx, wo)


def _mlp_body(x_ref, g_ref, wup_ref, wdn_ref, o_ref, h_ref, acc_ref):
    j = pl.program_id(1)

    @pl.when(j == 0)
    def _():
        h_ref[...] = _rms(x_ref[...], g_ref[...]).astype(BF16)
        acc_ref[...] = jnp.zeros(acc_ref.shape, F32)

    u = jnp.dot(h_ref[...], wup_ref[...], preferred_element_type=F32)
    a = jnp.square(jnp.maximum(u, 0.0)).astype(BF16)
    acc_ref[...] += jnp.dot(a, wdn_ref[...], preferred_element_type=F32)

    @pl.when(j == pl.num_programs(1) - 1)
    def _():
        o_ref[...] = x_ref[...] + acc_ref[...]


def _mlp_call(x2, g, wup, wdn, layer, *, tm, tf):
    t, d = x2.shape
    ff = wup.shape[2]
    return pl.pallas_call(
        _mlp_body,
        grid=(t // tm, ff // tf),
        in_specs=[pl.BlockSpec((tm, d), lambda i, j: (i, 0)), _full_spec(g),
                  pl.BlockSpec((None, d, tf), lambda i, j: (layer, 0, j)),
                  pl.BlockSpec((None, tf, d), lambda i, j: (layer, j, 0))],
        out_specs=pl.BlockSpec((tm, d), lambda i, j: (i, 0)),
        out_shape=jax.ShapeDtypeStruct((t, d), F32),
        scratch_shapes=[pltpu.VMEM((tm, d), BF16), pltpu.VMEM((tm, d), F32)],
        compiler_params=_cparams("parallel", "arbitrary"),
        name="mlp",
    )(x2, g, wup, wdn)


def _mix1_body(x_ref, xh_ref, gmix_ref, win_ref, cw_ref, wpool_ref, ps_ref, wout_ref,
               gx_ref, wq_ref, gq_ref, kx_ref, vx_ref, wo_ref, o_ref, *, tm, seq):
    i = pl.program_id(0)
    x = x_ref[...]
    xf = jnp.concatenate([xh_ref[...], x], axis=0)
    hf = _rms(xf, gmix_ref[...]).astype(BF16)
    y = jnp.dot(hf, win_ref[...], preferred_element_type=F32)
    cw = y.shape[1] // 4
    pos0 = (i * tm) % seq
    rid = lax.broadcasted_iota(jnp.int32, (tm + POOL_HALO, 1), 0)
    keep = jnp.where(rid >= POOL_HALO, 1.0, jnp.where(pos0 > 0, 1.0, 0.0))

    gb = y[POOL_HALO:, 0:cw]
    u = y[:, cw:2 * cw] * y[:, 2 * cw:3 * cw] * keep
    z = y[:, 3 * cw:4 * cw] * keep

    taps = cw_ref[...]
    conv = taps[CONV_W - 1:CONV_W, :] * u[POOL_HALO:]
    for back in range(1, CONV_W):
        conv = conv + taps[CONV_W - 1 - back:CONV_W - back, :] * pltpu.roll(u, back, axis=0)[POOL_HALO:]
    yc = gb * conv

    pos = pos0 + lax.broadcasted_iota(jnp.int32, (tm, 1), 0)
    gw = cw // len(POOL_WINDOWS)
    parts = []
    for g, w in enumerate(POOL_WINDOWS):
        zg = z[:, g * gw:(g + 1) * gw]
        s = zg
        sh = 1
        while sh < w:
            s = s + pltpu.roll(s, sh, axis=0)
            sh *= 2
        cnt = jnp.minimum(pos + 1, w).astype(F32)
        pooled = s[POOL_HALO:] / cnt - zg[POOL_HALO:]
        parts.append(jnp.dot(pooled.astype(BF16), wpool_ref[g], preferred_element_type=F32))
    yd = jnp.concatenate(parts, axis=1) * ps_ref[...]

    mix = (jnp.dot(yc.astype(BF16), wout_ref[0:cw, :], preferred_element_type=F32)
           + jnp.dot(yd.astype(BF16), wout_ref[cw:2 * cw, :], preferred_element_type=F32))
    o_ref[...] = _xattn_tail(x + mix, gx_ref, wq_ref, gq_ref, kx_ref, vx_ref, wo_ref)


def _mix1_call(x2, gmix, win, cw, wpool, ps, wout, gx, wq, gq, kx, vx, wo, *, seq, tm):
    t, d = x2.shape
    row = pl.BlockSpec((tm, d), lambda i: (i, 0))
    halo = pl.BlockSpec((POOL_HALO, d), lambda i: (jnp.maximum(i * (tm // POOL_HALO) - 1, 0), 0))
    mem_spec = pl.BlockSpec((1,) + kx.shape[1:], lambda i: ((i * tm) // seq, 0, 0))
    fulls = [gmix, win, cw, wpool, ps, wout, gx, wq, gq]
    return pl.pallas_call(
        functools.partial(_mix1_body, tm=tm, seq=seq),
        grid=(t // tm,),
        in_specs=[row, halo] + [_full_spec(a) for a in fulls] + [mem_spec, mem_spec, _full_spec(wo)],
        out_specs=row,
        out_shape=jax.ShapeDtypeStruct((t, d), F32),
        compiler_params=_cparams("parallel"),
        name="mix1",
    )(x2, x2, gmix, win, cw, wpool, ps, wout, gx, wq, gq, kx, vx, wo)


def _row(v, reps=1):
    return jnp.tile(v.astype(F32), reps).reshape(1, -1)


def kernel(x, mem, positions, g_mix, g_xattn, g_mem, g_mlp, wq_x, wkv_x, gq_x, gk_x, wo_x, w_up, w_down,
           w_in_e, gq_a, gk_a, lam_q1, lam_k1, lam_q2, lam_k2, g_sub_a, g_kv_b, w_kv_up_b, gq_b, gk_b,
           g_kidx, w_out_e, w_in_o, conv_w, w_pool, pool_scale, w_out_o):
    batch, seq, d = x.shape
    t = batch * seq
    k_sel = min(TOPK_MAX, seq // 4)
    tm = min(1024, seq)
    tq_b, tk_b = min(256, seq), min(512, seq)
    assert seq % tm == 0 and seq % tq_b == 0 and seq % tk_b == 0 and tk_b >= k_sel

    x2 = x.reshape(t, d)
    pos = positions.reshape(t // tm, 1, tm)
    inv_freq = (ROPE_THETA ** (-jnp.arange(0, ROT_DIM, 2, dtype=F32) / ROT_DIM)).reshape(ROT_HALF, 1)
    mem2 = mem.reshape(batch * mem.shape[1], d)
    depth = g_mix.shape[0]
    w_up_b, w_down_b = w_up.astype(BF16), w_down.astype(BF16)

    for layer in range(depth):
        j = layer // 2
        kx, vx = _memkv_call(mem2, _row(g_mem[layer]), wkv_x[layer].astype(BF16), _row(gk_x[layer]))
        kx = kx.reshape(batch, mem.shape[1], -1)
        vx = vx.reshape(batch, mem.shape[1], -1)
        tail = (_row(g_xattn[layer]), wq_x[layer].astype(BF16), _row(gq_x[layer]) * (X_HEAD_DIM ** -0.5),
                kx, vx, wo_x[layer].astype(BF16))
        if layer % 2 == 0:
            lam_init = 0.8 - 0.6 * float(np.exp(-0.3 * layer))
            lam = (jnp.exp(jnp.sum(lam_q1[j].astype(F32) * lam_k1[j].astype(F32)))
                   - jnp.exp(jnp.sum(lam_q2[j].astype(F32) * lam_k2[j].astype(F32))) + lam_init)
            w_in = w_in_e[j]
            n_main = 2304
            n_idx = w_in.shape[1] - n_main
            widx = jnp.pad(w_in[:, n_main:], ((0, 0), (0, 3 * LANES - n_idx)))
            if IDX_PRECISION is None:
                widx = widx.astype(BF16)
            heads8 = 512 // HEAD_DIM
            bd = jnp.kron(jnp.eye(heads8, dtype=F32), jnp.ones((HEAD_DIM, HEAD_DIM), F32)).astype(BF16)
            gkidx = jnp.pad(g_kidx[j].astype(F32), (0, LANES - IDX_DIM)).reshape(1, LANES)
            qa, ka, vat, qb, kb, vbt, qi, kik, wi = _in0_call(
                x2, _row(g_mix[layer]), w_in[:, :n_main].astype(BF16), widx, w_kv_up_b[j].astype(BF16),
                pos, inv_freq, bd, _row(gq_a[j], heads8), _row(gk_a[j], heads8), _row(gq_b[j], heads8),
                _row(gk_b[j], heads8), _row(g_kv_b[j]), gkidx, tm=tm, tk=tk_b, idx_precision=IDX_PRECISION)
            out_a = _diff_attn_call(lam.reshape(1).astype(F32), qa, ka, vat, _row(g_sub_a[j]),
                                    batch=batch, seq=seq, tq=tq_b, tk=tk_b, group=qa.shape[1] // LANES, out_scale=1.0 - lam_init)
            tri = jnp.tril(jnp.ones((LANES, LANES), F32)).astype(BF16)
            out_b = _dsa_call(qi, wi, qb, kik, kb, vbt, tri, batch=batch, seq=seq, tq=tq_b, tk=tk_b,
                              k_sel=k_sel, idx_precision=IDX_PRECISION)
            w_out = w_out_e[j].astype(BF16)
            x2 = _out0_call(x2, out_a, out_b, w_out[:512], w_out[512:], *tail, seq=seq, tm=tm)
        else:
            taps = jnp.pad(conv_w[j].astype(F32), ((0, SUBLANES - CONV_W), (0, 0)))
            x2 = _mix1_call(x2, _row(g_mix[layer]), w_in_o[j].astype(BF16), taps, w_pool[j].astype(BF16),
                            _row(pool_scale[j]), w_out_o[j].astype(BF16), *tail, seq=seq, tm=tm)
        x2 = _mlp_call(x2, _row(g_mlp[layer]), w_up_b, w_down_b, layer, tm=min(1024, t), tf=2048)
    return x2.reshape(batch, seq, d)
```

```python
import functools

import numpy as np
import jax
import jax.numpy as jnp
from jax import lax
from jax.experimental import pallas as pl
from jax.experimental.pallas import tpu as pltpu

F32 = jnp.float32
BF16 = jnp.bfloat16

HEAD_DIM = 64
ROT_DIM = HEAD_DIM // 4
ROT_HALF = ROT_DIM // 2
ROPE_THETA = 500000.0
IDX_HEADS = 4
IDX_DIM = 64
TOPK_MAX = 256
CONV_W = 3
POOL_WINDOWS = (2, 4, 8, 16)
POOL_HALO = 16
X_HEADS = 4
X_HEAD_DIM = 128
EPS = 1e-6
LOG2_E = 1.4426950408889634

LANES = 128
SUBLANES = 8
VMEM_LIMIT_BYTES = 56 * 1024 * 1024

KEY_NEG_INF = np.int32(-2139095041)
INT32_MIN = np.int32(-(2 ** 31))

NT_DIMS = (((1,), (1,)), ((), ()))

IDX_PRECISION = None


def _cparams(*sem):
    return pltpu.CompilerParams(dimension_semantics=sem, vmem_limit_bytes=VMEM_LIMIT_BYTES)


def _full_spec(arr):
    nd = arr.ndim
    return pl.BlockSpec(arr.shape, lambda *_: (0,) * nd)


def _rms(x, g):
    ms = jnp.mean(x * x, axis=-1, keepdims=True)
    return x * lax.rsqrt(ms + EPS) * g


def _tile_lanes(t, width):
    reps = width // t.shape[-1]
    return t if reps == 1 else jnp.concatenate([t] * reps, axis=1)


def _rope_tables(pos_row, inv_col):
    ang = inv_col * pos_row
    cos, sin = jnp.cos(ang), jnp.sin(ang)
    rest = (HEAD_DIM - ROT_DIM, ang.shape[1])
    head_c = jnp.concatenate([cos, cos, jnp.ones(rest, F32)], axis=0)
    head_s = jnp.concatenate([-sin, sin, jnp.zeros(rest, F32)], axis=0)
    return (jnp.concatenate([head_c, head_c], axis=0).T, jnp.concatenate([head_s, head_s], axis=0).T)


def _rope(y, c, s):
    w = y.shape[-1]
    c, s = _tile_lanes(c, w), _tile_lanes(s, w)
    lane = lax.broadcasted_iota(jnp.int32, (1, w), 1)
    upper = pltpu.roll(y, w - ROT_HALF, axis=1)
    lower = pltpu.roll(y, ROT_HALF, axis=1)
    partner = jnp.where((lane & (HEAD_DIM - 1)) < ROT_HALF, upper, lower)
    return y * c + partner * s


def _headnorm64(y, blockdiag, g):
    ss = jnp.dot((y * y).astype(BF16), blockdiag, preferred_element_type=F32)
    return y * lax.rsqrt(ss * (1.0 / HEAD_DIM) + EPS) * g


def _in0_body(x_ref, gmix_ref, wmain_ref, widx_ref, wkvup_ref, pos_ref, inv_ref,
              bd_ref, gqa_ref, gka_ref, gqb_ref, gkb_ref, gkv_ref, gkidx_ref,
              qa_ref, ka_ref, vat_ref, qb_ref, kb_ref, vbt_ref, qi_ref, kik_ref, wi_ref,
              *, idx_precision):
    h = _rms(x_ref[...], gmix_ref[...])
    hb = h.astype(BF16)
    proj = lambda lo, hi: jnp.dot(hb, wmain_ref[:, lo:hi], preferred_element_type=F32)
    rot = _rope_tables(pos_ref[0].astype(F32), inv_ref[...])
    bd = bd_ref[...]
    scale = HEAD_DIM ** -0.5 * LOG2_E

    qa = _rope(_headnorm64(proj(0, 512), bd, gqa_ref[...]), *rot) * scale
    qa_ref[...] = qa.astype(BF16)
    ka = _rope(_headnorm64(proj(512, 1024), bd, gka_ref[...]), *rot)
    ka_ref[...] = ka.astype(BF16)
    tk = vat_ref.shape[2]
    va = proj(1024, 1536)
    for r in range(vat_ref.shape[0]):
        vat_ref[r] = va[r * tk:(r + 1) * tk, :].T.astype(BF16)
    qb = _rope(_headnorm64(proj(1536, 2048), bd, gqb_ref[...]), *rot) * scale
    qb_ref[...] = qb.astype(BF16)

    ckv = _rms(proj(2048, 2304), gkv_ref[...]).astype(BF16)
    kv = jnp.dot(ckv, wkvup_ref[...], preferred_element_type=F32)
    kb = _rope(_headnorm64(kv[:, 0:512], bd, gkb_ref[...]), *rot)
    kb_ref[...] = kb.astype(BF16)
    for r in range(vbt_ref.shape[0]):
        vbt_ref[r] = kv[r * tk:(r + 1) * tk, 512:1024].T.astype(BF16)

    if idx_precision is None:
        yi = jnp.dot(hb, widx_ref[...], preferred_element_type=F32)
    else:
        yi = jnp.dot(h, widx_ref[...], preferred_element_type=F32, precision=idx_precision)
    qi_ref[...] = _rope(yi[:, 0:256], *rot).astype(qi_ref.dtype)
    blk = yi[:, 256:384]
    lane = lax.broadcasted_iota(jnp.int32, blk.shape, 1)
    ms = jnp.sum(jnp.where(lane < IDX_DIM, blk * blk, 0.0), axis=-1, keepdims=True) * (1.0 / IDX_DIM)
    kin = blk * lax.rsqrt(ms + EPS) * gkidx_ref[...]
    kir = _rope(kin, *rot)
    kik = kir + pltpu.roll(kir, IDX_DIM, axis=1)
    kik_ref[...] = kik.astype(kik_ref.dtype)
    wi_ref[...] = blk * (IDX_HEADS ** -0.5 * IDX_DIM ** -0.5)


def _in0_call(x2, gmix, wmain, widx, wkvup, pos, inv, bd, gqa, gka, gqb, gkb, gkv, gkidx,
              *, tm, tk, idx_precision):
    t, d = x2.shape
    row = lambda w: pl.BlockSpec((tm, w), lambda i: (i, 0))
    fulls = [gmix, wmain, widx, wkvup]
    gains = [inv, bd, gqa, gka, gqb, gkb, gkv, gkidx]
    in_specs = ([row(d)] + [_full_spec(a) for a in fulls] + [pl.BlockSpec((1, 1, tm), lambda i: (i, 0, 0))]
                + [_full_spec(a) for a in gains])
    idx_dt = BF16 if idx_precision is None else F32
    sds = jax.ShapeDtypeStruct
    vt_shape, vt_spec = sds((t // tk, 512, tk), BF16), pl.BlockSpec((tm // tk, 512, tk), lambda i: (i, 0, 0))
    out_shape = ([sds((t, 512), BF16)] * 2 + [vt_shape] + [sds((t, 512), BF16)] * 2
                 + [vt_shape, sds((t, 256), idx_dt), sds((t, LANES), idx_dt), sds((t, LANES), F32)])
    out_specs = [row(512)] * 2 + [vt_spec] + [row(512)] * 2 + [vt_spec, row(256), row(LANES), row(LANES)]
    return pl.pallas_call(
        functools.partial(_in0_body, idx_precision=idx_precision),
        grid=(t // tm,),
        in_specs=in_specs,
        out_specs=out_specs,
        out_shape=out_shape,
        compiler_params=_cparams("parallel"),
        name="in0",
    )(x2, gmix, wmain, widx, wkvup, pos, inv, bd, gqa, gka, gqb, gkb, gkv, gkidx)


def _diff_attn_body(lam_ref, q_ref, k_ref, vt_ref, gsub_ref, o_ref, s_ref, pe_ref, *, tq, tk, out_scale):
    i = pl.program_id(2)
    q0 = i * tq
    n_full = (q0 + 1) // tk
    nkc = (q0 + tq + tk - 1) // tk
    lane = lax.broadcasted_iota(jnp.int32, (1, LANES), 1)
    lane_lo = lane < HEAD_DIM
    qpos = q0 + lax.broadcasted_iota(jnp.int32, (1, tq), 1)
    qpos2 = jnp.concatenate([qpos, qpos], axis=1)
    neg_inf = jnp.float32(-jnp.inf)
    heads = range(q_ref.shape[1] // LANES)
    q_cat = []
    for g in heads:
        q_h = q_ref[:, g * LANES:(g + 1) * LANES]
        q_cat.append(jnp.concatenate([q_h * lane_lo.astype(BF16), q_h * (~lane_lo).astype(BF16)], axis=0))

    def logit_step(g, c, m, masked):
        s = lax.dot_general(k_ref[pl.ds(pl.multiple_of(c * tk, tk), tk), g * LANES:(g + 1) * LANES],
                            q_cat[g], NT_DIMS, preferred_element_type=F32)
        if masked:
            kpos = c * tk + lax.broadcasted_iota(jnp.int32, (tk, 1), 0)
            s = jnp.where(kpos <= qpos2, s, neg_inf)
        s_ref[g % s_ref.shape[0], c] = s
        return jnp.maximum(m, jnp.max(s.reshape(tk // SUBLANES, SUBLANES, 2 * tq), axis=0))

    def pv_matmul(g, c):
        return jnp.dot(vt_ref[c, g * LANES:(g + 1) * LANES, :], pe_ref[g % pe_ref.shape[0]],
                       preferred_element_type=F32)

    def exp_step(g, c, shift, l8):
        pe = jnp.exp2(s_ref[g % s_ref.shape[0], c] - shift)
        pe_ref[g % pe_ref.shape[0]] = pe.astype(BF16)
        return l8 + jnp.sum(pe.reshape(tk // SUBLANES, SUBLANES, 2 * tq), axis=0)

    def pv_step(g, c, shift, l8, acc):
        pe = jnp.exp2(s_ref[g % s_ref.shape[0], c] - shift)
        acc = acc + jnp.dot(vt_ref[c, g * LANES:(g + 1) * LANES, :], pe.astype(BF16),
                            preferred_element_type=F32)
        return l8 + jnp.sum(pe.reshape(tk // SUBLANES, SUBLANES, 2 * tq), axis=0), acc

    def run(body, init, by_twos):
        loop = _loop_by_twos if by_twos else lax.fori_loop
        carry = loop(0, n_full, functools.partial(body, masked=False), init)
        return lax.fori_loop(n_full, nkc, functools.partial(body, masked=True), carry)

    lam = lam_ref[0]
    m_init = jnp.full((SUBLANES, 2 * tq), neg_inf, F32)
    pv_init = (jnp.zeros((SUBLANES, 2 * tq), F32), jnp.zeros((LANES, 2 * tq), F32))

    def finish(g, l8, acc):
        o_all = acc / jnp.sum(l8, axis=0, keepdims=True)
        o = o_all[:, 0:tq] - lam * o_all[:, tq:2 * tq]
        o = o * lax.rsqrt(jnp.mean(o * o, axis=0, keepdims=True) + EPS)
        o_ref[:, g * LANES:(g + 1) * LANES] = (o.T * gsub_ref[...] * out_scale).astype(o_ref.dtype)

    width = s_ref.shape[0] // 2
    stages = [heads[lo:lo + width] for lo in range(0, len(heads), width)]

    def logit_stage(stage, c, ms, masked):
        return tuple(logit_step(g, c, m, masked) for g, m in zip(stage, ms))

    def pv_stage(stage, c, shifts, carry):
        return tuple(pv_step(g, c, sh, *lc) for g, sh, lc in zip(stage, shifts, carry))

    ms = run(lambda c, ms, masked: logit_stage(stages[0], c, ms, masked), (m_init,) * width, True)
    for prev, cur in zip(stages[:-1], stages[1:]):
        shifts = [jnp.max(m, axis=0, keepdims=True) for m in ms]

        def fused(c, carry, masked, prev=prev, cur=cur, shifts=shifts):
            pv, ms = carry
            return pv_stage(prev, c, shifts, pv), logit_stage(cur, c, ms, masked)

        pv, ms = run(fused, ((pv_init,) * width, (m_init,) * width), True)
        for g, lc in zip(prev, pv):
            finish(g, *lc)
    shifts = [jnp.max(m, axis=0, keepdims=True) for m in ms]

    last = stages[-1]

    def last_chunk(c, carry):
        l8s, accs = carry
        accs = tuple(acc + pv_matmul(g, c - 1) for g, acc in zip(last, accs))
        return tuple(exp_step(g, c, sh, l8) for g, sh, l8 in zip(last, shifts, l8s)), accs

    l8s = tuple(exp_step(g, 0, sh, pv_init[0]) for g, sh in zip(last, shifts))
    l8s, accs = lax.fori_loop(1, nkc, last_chunk, (l8s, (pv_init[1],) * width))
    for g, l8, acc in zip(last, l8s, accs):
        finish(g, l8, acc + pv_matmul(g, nkc - 1))


def _diff_attn_call(lam, qa, ka, vat, gsub, *, batch, seq, tq, tk, group, out_scale):
    t = qa.shape[0]
    w = group * LANES
    nq, nk = seq // tq, seq // tk
    q_spec = pl.BlockSpec((tq, w), lambda b, h, i: (b * nq + i, h))
    return pl.pallas_call(
        functools.partial(_diff_attn_body, tq=tq, tk=tk, out_scale=out_scale),
        grid=(batch, qa.shape[1] // w, nq),
        in_specs=[pl.BlockSpec(memory_space=pltpu.SMEM), q_spec,
                  pl.BlockSpec((seq, w), lambda b, h, i: (b, h)),
                  pl.BlockSpec((nk, w, tk), lambda b, h, i: (b, h, 0)), _full_spec(gsub)],
        out_specs=q_spec,
        out_shape=jax.ShapeDtypeStruct((t, qa.shape[1]), BF16),
        scratch_shapes=[pltpu.VMEM((group, nk, tk, 2 * tq), F32), pltpu.VMEM((group // 2, tk, 2 * tq), BF16)],
        compiler_params=_cparams("parallel", "parallel", "arbitrary"),
        name="diff_attn",
    )(lam, qa, ka, vat, gsub)


def _loop_by_twos(lo, hi, step, carry):
    pairs = (hi - lo) // 2
    carry = lax.fori_loop(0, pairs, lambda i, cr: step(lo + 2 * i + 1, step(lo + 2 * i, cr)), carry)
    return lax.fori_loop(lo + 2 * pairs, hi, step, carry)


def _key_to_float(key):
    bits = key ^ ((key >> 31) & jnp.int32(0x7FFFFFFF))
    return lax.bitcast_convert_type(bits, F32)


def _dsa_body(qi_ref, wi_ref, qb_ref, kik_ref, kb_ref, vbt_ref, tri_ref, o_ref, sc_ref, bias_ref, s_ref, pe_ref,
              *, tq, tk, k_sel, idx_precision):
    i = pl.program_id(1)
    q0 = i * tq
    nkc = (q0 + tq + tk - 1) // tk
    qpos = q0 + lax.broadcasted_iota(jnp.int32, (1, tq), 1)
    lane = lax.broadcasted_iota(jnp.int32, (1, LANES), 1)
    lane_lo = lane < HEAD_DIM
    tiles = tk // LANES
    neg_inf = jnp.float32(-jnp.inf)

    wi_t = wi_ref[...].T
    qi = qi_ref[...]
    q_heads = []
    for h in range(IDX_HEADS):
        pair = qi[:, (h // 2) * LANES:(h // 2 + 1) * LANES]
        q_heads.append(pair * (lane_lo if h % 2 == 0 else ~lane_lo).astype(pair.dtype))
    qi_cat = jnp.concatenate(q_heads, axis=0)
    w_heads = [wi_t[IDX_DIM + h:IDX_DIM + h + 1, :] for h in range(IDX_HEADS)]

    def score_chunk(c, carry):
        kk = kik_ref[pl.ds(pl.multiple_of(c * tk, tk), tk), :]
        s = lax.dot_general(kk, qi_cat, NT_DIMS, preferred_element_type=F32, precision=idx_precision)
        acc = jnp.zeros((tk, tq), F32)
        for h in range(IDX_HEADS):
            acc = acc + jnp.maximum(s[:, h * tq:(h + 1) * tq], 0.0) * w_heads[h]
        kpos = c * tk + lax.broadcasted_iota(jnp.int32, (tk, 1), 0)
        sc_ref[c] = jnp.where(kpos <= qpos, acc, neg_inf)
        return carry

    _loop_by_twos(0, nkc, score_chunk, 0)

    acc_rows = 4 * SUBLANES

    def count_ge(cand):
        def body(c, acc):
            for r in range(tk // acc_rows):
                acc = acc + jnp.where(sc_ref[c, r * acc_rows:(r + 1) * acc_rows, :] >= cand, 1.0, 0.0)
            return acc
        acc = lax.fori_loop(0, nkc, body, jnp.zeros((acc_rows, tq), F32))
        return jnp.sum(acc, axis=0, keepdims=True)

    ksel = float(k_sel)

    def count_at(cand_key):
        return jnp.where(cand_key <= KEY_NEG_INF, ksel, count_ge(_key_to_float(cand_key)))

    zero_key = jnp.zeros((1, tq), jnp.int32)
    cnt = count_at(zero_key)
    thr = jnp.where(cnt >= ksel, zero_key, INT32_MIN)
    n_above = jnp.where(cnt >= ksel, 0.0, cnt)

    def bit_step(b, carry):
        thr, n_above = carry
        cand = thr + jnp.left_shift(jnp.int32(1), 30 - b)
        cnt = count_at(cand)
        return jnp.where(cnt >= ksel, cand, thr), jnp.where(cnt >= ksel, n_above, cnt)

    thr, n_above = lax.fori_loop(0, 31, bit_step, (thr, n_above))
    thr_f = _key_to_float(thr)
    above_f = _key_to_float(thr + 1)
    need = ksel - n_above
    tri = tri_ref[...]

    def select_chunk(c, taken):
        for t in range(tiles):
            st = sc_ref[c, t * LANES:(t + 1) * LANES, :]
            is_above = st >= above_f
            tied = jnp.where(is_above, 0.0, jnp.where(st >= thr_f, 1.0, 0.0))
            prefix = jnp.dot(tri, tied.astype(BF16), preferred_element_type=F32)
            tie_sel = jnp.where((taken + prefix) <= need, tied, 0.0)
            sel = jnp.where(is_above, 0.0, jnp.where(tie_sel > 0.0, 0.0, neg_inf))
            kpos = c * tk + t * LANES + lax.broadcasted_iota(jnp.int32, (LANES, 1), 0)
            bias_ref[c, t * LANES:(t + 1) * LANES, :] = jnp.where(kpos <= qpos, sel, neg_inf)
            taken = taken + prefix[LANES - 1:LANES, :]
        return taken

    _loop_by_twos(0, nkc, select_chunk, jnp.zeros((1, tq), F32))

    group = s_ref.shape[0]
    for p0 in range(0, qb_ref.shape[1] // LANES, group):
        pairs = range(p0, p0 + group)
        q_cat = []
        for p in pairs:
            q_pair = qb_ref[:, p * LANES:(p + 1) * LANES]
            q_cat.append(jnp.concatenate([q_pair * lane_lo.astype(BF16), q_pair * (~lane_lo).astype(BF16)],
                                         axis=0))

        def logit_chunk(c, carry, pairs=pairs, q_cat=q_cat):
            bias = bias_ref[c]
            bias2 = jnp.concatenate([bias, bias], axis=1)
            out = []
            for g, (p, m) in enumerate(zip(pairs, carry)):
                kc = kb_ref[pl.ds(pl.multiple_of(c * tk, tk), tk), p * LANES:(p + 1) * LANES]
                s = lax.dot_general(kc, q_cat[g], NT_DIMS, preferred_element_type=F32) + bias2
                s_ref[g, c] = s
                out.append(jnp.maximum(m, jnp.max(s.reshape(tk // SUBLANES, SUBLANES, 2 * tq), axis=0)))
            return tuple(out)

        m8 = _loop_by_twos(0, nkc, logit_chunk,
                           tuple(jnp.full((SUBLANES, 2 * tq), neg_inf, F32) for _ in pairs))
        shift = [jnp.max(m, axis=0, keepdims=True) for m in m8]

        def exp_stage(c, l8s, shift=shift):
            out = []
            for g, l8 in enumerate(l8s):
                pe = jnp.exp2(s_ref[g, c] - shift[g])
                pe_ref[g] = pe.astype(BF16)
                out.append(l8 + jnp.sum(pe.reshape(tk // SUBLANES, SUBLANES, 2 * tq), axis=0))
            return tuple(out)

        def pv_stage(c, accs, pairs=pairs):
            return tuple(acc + jnp.dot(vbt_ref[c, p * LANES:(p + 1) * LANES, :], pe_ref[g],
                                       preferred_element_type=F32)
                         for g, (p, acc) in enumerate(zip(pairs, accs)))

        def pv_chunk(c, carry):
            l8s, accs = carry
            accs = pv_stage(c - 1, accs)
            return exp_stage(c, l8s), accs

        l8s = exp_stage(0, tuple(jnp.zeros((SUBLANES, 2 * tq), F32) for _ in pairs))
        l8s, accs = lax.fori_loop(1, nkc, pv_chunk,
                                  (l8s, tuple(jnp.zeros((LANES, 2 * tq), F32) for _ in pairs)))
        accs = pv_stage(nkc - 1, accs)
        for p, l8, acc in zip(pairs, l8s, accs):
            o_all = acc / jnp.sum(l8, axis=0, keepdims=True)
            o_t = jnp.concatenate([o_all[0:HEAD_DIM, 0:tq], o_all[HEAD_DIM:LANES, tq:2 * tq]], axis=0)
            o_ref[:, p * LANES:(p + 1) * LANES] = o_t.T.astype(o_ref.dtype)


def _dsa_call(qi, wi, qb, kik, kb, vbt, tri, *, batch, seq, tq, tk, k_sel, idx_precision):
    t = qb.shape[0]
    nq, nk = seq // tq, seq // tk
    row = lambda w: pl.BlockSpec((tq, w), lambda b, i: (b * nq + i, 0))
    seqb = lambda w: pl.BlockSpec((seq, w), lambda b, i: (b, 0))
    vbt_spec = pl.BlockSpec((nk,) + vbt.shape[1:], lambda b, i: (b, 0, 0))
    return pl.pallas_call(
        functools.partial(_dsa_body, tq=tq, tk=tk, k_sel=k_sel, idx_precision=idx_precision),
        grid=(batch, nq),
        in_specs=[row(qi.shape[1]), row(LANES), row(qb.shape[1]), seqb(LANES), seqb(kb.shape[1]),
                  vbt_spec, _full_spec(tri)],
        out_specs=row(qb.shape[1]),
        out_shape=jax.ShapeDtypeStruct((t, qb.shape[1]), BF16),
        scratch_shapes=[pltpu.VMEM((nk, tk, tq), F32), pltpu.VMEM((nk, tk, tq), F32),
                        pltpu.VMEM((2, nk, tk, 2 * tq), F32), pltpu.VMEM((2, tk, 2 * tq), BF16)],
        compiler_params=_cparams("parallel", "arbitrary"),
        name="dsa",
    )(qi, wi, qb, kik, kb, vbt, tri)


def _memkv_body(mem_ref, g_ref, w_ref, gk_ref, k_ref, v_ref):
    hm = _rms(mem_ref[...], g_ref[...]).astype(BF16)
    kv = jnp.dot(hm, w_ref[...], preferred_element_type=F32)
    half = kv.shape[1] // 2
    for h in range(X_HEADS):
        kh = kv[:, h * X_HEAD_DIM:(h + 1) * X_HEAD_DIM]
        k_ref[:, h * X_HEAD_DIM:(h + 1) * X_HEAD_DIM] = _rms(kh, gk_ref[...]).astype(BF16)
    v_ref[...] = kv[:, half:].astype(BF16)


def _memkv_call(mem2, g, w, gk):
    rows, d = mem2.shape
    half = w.shape[1] // 2
    tm = min(rows, 512)
    row = lambda wd: pl.BlockSpec((tm, wd), lambda i: (i, 0))
    return pl.pallas_call(
        _memkv_body,
        grid=(rows // tm,),
        in_specs=[row(d), _full_spec(g), _full_spec(w), _full_spec(gk)],
        out_specs=[row(half), row(half)],
        out_shape=[jax.ShapeDtypeStruct((rows, half), BF16)] * 2,
        compiler_params=_cparams("parallel"),
        name="memkv",
    )(mem2, g, w, gk)


def _xattn_tail(x1, gx_ref, wq_ref, gq_ref, kx_ref, vx_ref, wo_ref):
    hx = _rms(x1, gx_ref[...]).astype(BF16)
    q = jnp.dot(hx, wq_ref[...], preferred_element_type=F32)
    outs = []
    for h in range(X_HEADS):
        cols = slice(h * X_HEAD_DIM, (h + 1) * X_HEAD_DIM)
        qh = _rms(q[:, cols], gq_ref[...]).astype(BF16)
        s = lax.dot_general(qh, kx_ref[0, :, cols], NT_DIMS, preferred_element_type=F32)
        p = jnp.exp(s - jnp.max(s, axis=-1, keepdims=True))
        o = (jnp.dot(p.astype(BF16), vx_ref[0, :, cols], preferred_element_type=F32)
             / jnp.sum(p, axis=-1, keepdims=True))
        outs.append(o.astype(BF16))
    o = jnp.concatenate(outs, axis=1)
    return x1 + jnp.dot(o, wo_ref[...], preferred_element_type=F32)


def _out0_body(x_ref, oa_ref, ob_ref, woa_ref, wob_ref, gx_ref, wq_ref, gq_ref, kx_ref, vx_ref, wo_ref,
               o_ref):
    mix = (jnp.dot(oa_ref[...], woa_ref[...], preferred_element_type=F32)
           + jnp.dot(ob_ref[...], wob_ref[...], preferred_element_type=F32))
    o_ref[...] = _xattn_tail(x_ref[...] + mix, gx_ref, wq_ref, gq_ref, kx_ref, vx_ref, wo_ref)


def _out0_call(x2, oa, ob, woa, wob, gx, wq, gq, kx, vx, wo, *, seq, tm):
    t, d = x2.shape
    row = lambda w: pl.BlockSpec((tm, w), lambda i: (i, 0))
    mem_spec = pl.BlockSpec((1,) + kx.shape[1:], lambda i: ((i * tm) // seq, 0, 0))
    fulls = [woa, wob, gx, wq, gq]
    return pl.pallas_call(
        _out0_body,
        grid=(t // tm,),
        in_specs=[row(d), row(oa.shape[1]), row(ob.shape[1])] + [_full_spec(a) for a in fulls]
                 + [mem_spec, mem_spec, _full_spec(wo)],
        out_specs=row(d),
        out_shape=jax.ShapeDtypeStruct((t, d), F32),
        compiler_params=_cparams("parallel"),
        name="out0",
    )(x2, oa, ob, woa, wob, gx, wq, gq, kx, vx, wo)


def _mlp_body(x_ref, g_ref, wup_ref, wdn_ref, o_ref, h_ref, acc_ref):
    j = pl.program_id(1)

    @pl.when(j == 0)
    def _():
        h_ref[...] = _rms(x_ref[...], g_ref[...]).astype(BF16)
        acc_ref[...] = jnp.zeros(acc_ref.shape, F32)

    u = jnp.dot(h_ref[...], wup_ref[...], preferred_element_type=F32)
    a = jnp.square(jnp.maximum(u, 0.0)).astype(BF16)
    acc_ref[...] += jnp.dot(a, wdn_ref[...], preferred_element_type=F32)

    @pl.when(j == pl.num_programs(1) - 1)
    def _():
        o_ref[...] = x_ref[...] + acc_ref[...]


def _mlp_call(x2, g, wup, wdn, layer, *, tm, tf):
    t, d = x2.shape
    ff = wup.shape[2]
    return pl.pallas_call(
        _mlp_body,
        grid=(t // tm, ff // tf),
        in_specs=[pl.BlockSpec((tm, d), lambda i, j: (i, 0)), _full_spec(g),
                  pl.BlockSpec((None, d, tf), lambda i, j: (layer, 0, j)),
                  pl.BlockSpec((None, tf, d), lambda i, j: (layer, j, 0))],
        out_specs=pl.BlockSpec((tm, d), lambda i, j: (i, 0)),
        out_shape=jax.ShapeDtypeStruct((t, d), F32),
        scratch_shapes=[pltpu.VMEM((tm, d), BF16), pltpu.VMEM((tm, d), F32)],
        compiler_params=_cparams("parallel", "arbitrary"),
        name="mlp",
    )(x2, g, wup, wdn)


def _mix1_body(x_ref, xh_ref, gmix_ref, win_ref, cw_ref, wpool_ref, ps_ref, wout_ref,
               gx_ref, wq_ref, gq_ref, kx_ref, vx_ref, wo_ref, o_ref, *, tm, seq):
    i = pl.program_id(0)
    x = x_ref[...]
    xf = jnp.concatenate([xh_ref[...], x], axis=0)
    hf = _rms(xf, gmix_ref[...]).astype(BF16)
    cw = win_ref.shape[1] // 4
    proj = lambda n: jnp.dot(hf, win_ref[:, n * cw:(n + 1) * cw], preferred_element_type=F32)
    pos0 = (i * tm) % seq
    rid = lax.broadcasted_iota(jnp.int32, (tm + POOL_HALO, 1), 0)
    keep = jnp.where(rid >= POOL_HALO, 1.0, jnp.where(pos0 > 0, 1.0, 0.0))

    u = proj(1) * proj(2) * keep
    z = proj(3) * keep
    gb = proj(0)[POOL_HALO:]

    taps = cw_ref[...]
    conv = taps[CONV_W - 1:CONV_W, :] * u[POOL_HALO:]
    for back in range(1, CONV_W):
        conv = conv + taps[CONV_W - 1 - back:CONV_W - back, :] * pltpu.roll(u, back, axis=0)[POOL_HALO:]
    yc = gb * conv

    pos = pos0 + lax.broadcasted_iota(jnp.int32, (tm, 1), 0)
    gw = cw // len(POOL_WINDOWS)
    parts = []
    for g, w in enumerate(POOL_WINDOWS):
        zg = z[:, g * gw:(g + 1) * gw]
        s = zg
        sh = 1
        while sh < w:
            s = s + pltpu.roll(s, sh, axis=0)
            sh *= 2
        cnt = jnp.minimum(pos + 1, w).astype(F32)
        pooled = s[POOL_HALO:] / cnt - zg[POOL_HALO:]
        parts.append(jnp.dot(pooled.astype(BF16), wpool_ref[g], preferred_element_type=F32))
    yd = jnp.concatenate(parts, axis=1) * ps_ref[...]

    mix = (jnp.dot(yc.astype(BF16), wout_ref[0:cw, :], preferred_element_type=F32)
           + jnp.dot(yd.astype(BF16), wout_ref[cw:2 * cw, :], preferred_element_type=F32))
    o_ref[...] = _xattn_tail(x + mix, gx_ref, wq_ref, gq_ref, kx_ref, vx_ref, wo_ref)


def _mix1_call(x2, gmix, win, cw, wpool, ps, wout, gx, wq, gq, kx, vx, wo, *, seq, tm):
    t, d = x2.shape
    row = pl.BlockSpec((tm, d), lambda i: (i, 0))
    halo = pl.BlockSpec((POOL_HALO, d), lambda i: (jnp.maximum(i * (tm // POOL_HALO) - 1, 0), 0))
    mem_spec = pl.BlockSpec((1,) + kx.shape[1:], lambda i: ((i * tm) // seq, 0, 0))
    fulls = [gmix, win, cw, wpool, ps, wout, gx, wq, gq]
    return pl.pallas_call(
        functools.partial(_mix1_body, tm=tm, seq=seq),
        grid=(t // tm,),
        in_specs=[row, halo] + [_full_spec(a) for a in fulls] + [mem_spec, mem_spec, _full_spec(wo)],
        out_specs=row,
        out_shape=jax.ShapeDtypeStruct((t, d), F32),
        compiler_params=_cparams("parallel"),
        name="mix1",
    )(x2, x2, gmix, win, cw, wpool, ps, wout, gx, wq, gq, kx, vx, wo)


def _row(v, reps=1):
    return jnp.tile(v.astype(F32), reps).reshape(1, -1)


def kernel(x, mem, positions, g_mix, g_xattn, g_mem, g_mlp, wq_x, wkv_x, gq_x, gk_x, wo_x, w_up, w_down,
           w_in_e, gq_a, gk_a, lam_q1, lam_k1, lam_q2, lam_k2, g_sub_a, g_kv_b, w_kv_up_b, gq_b, gk_b,
           g_kidx, w_out_e, w_in_o, conv_w, w_pool, pool_scale, w_out_o):
    batch, seq, d = x.shape
    t = batch * seq
    k_sel = min(TOPK_MAX, seq // 4)
    tm = min(1024, seq)
    tq_b, tk_b = min(256, seq), min(512, seq)
    assert seq % tm == 0 and seq % tq_b == 0 and seq % tk_b == 0 and tk_b >= k_sel

    x2 = x.reshape(t, d)
    pos = positions.reshape(t // tm, 1, tm)
    inv_freq = (ROPE_THETA ** (-jnp.arange(0, ROT_DIM, 2, dtype=F32) / ROT_DIM)).reshape(ROT_HALF, 1)
    mem2 = mem.reshape(batch * mem.shape[1], d)
    depth = g_mix.shape[0]
    w_up_b, w_down_b = w_up.astype(BF16), w_down.astype(BF16)

    for layer in range(depth):
        j = layer // 2
        kx, vx = _memkv_call(mem2, _row(g_mem[layer]), wkv_x[layer].astype(BF16), _row(gk_x[layer]))
        kx = kx.reshape(batch, mem.shape[1], -1)
        vx = vx.reshape(batch, mem.shape[1], -1)
        tail = (_row(g_xattn[layer]), wq_x[layer].astype(BF16), _row(gq_x[layer]) * (X_HEAD_DIM ** -0.5),
                kx, vx, wo_x[layer].astype(BF16))
        if layer % 2 == 0:
            lam_init = 0.8 - 0.6 * float(np.exp(-0.3 * layer))
            lam = (jnp.exp(jnp.sum(lam_q1[j].astype(F32) * lam_k1[j].astype(F32)))
                   - jnp.exp(jnp.sum(lam_q2[j].astype(F32) * lam_k2[j].astype(F32))) + lam_init)
            w_in = w_in_e[j]
            n_main = 2304
            n_idx = w_in.shape[1] - n_main
            widx = jnp.pad(w_in[:, n_main:], ((0, 0), (0, 3 * LANES - n_idx)))
            if IDX_PRECISION is None:
                widx = widx.astype(BF16)
            heads8 = 512 // HEAD_DIM
            bd = jnp.kron(jnp.eye(heads8, dtype=F32), jnp.ones((HEAD_DIM, HEAD_DIM), F32)).astype(BF16)
            gkidx = jnp.pad(g_kidx[j].astype(F32), (0, LANES - IDX_DIM)).reshape(1, LANES)
            qa, ka, vat, qb, kb, vbt, qi, kik, wi = _in0_call(
                x2, _row(g_mix[layer]), w_in[:, :n_main].astype(BF16), widx, w_kv_up_b[j].astype(BF16),
                pos, inv_freq, bd, _row(gq_a[j], heads8), _row(gk_a[j], heads8), _row(gq_b[j], heads8),
                _row(gk_b[j], heads8), _row(g_kv_b[j]), gkidx, tm=tm, tk=tk_b, idx_precision=IDX_PRECISION)
            out_a = _diff_attn_call(lam.reshape(1).astype(F32), qa, ka, vat, _row(g_sub_a[j]),
                                    batch=batch, seq=seq, tq=tq_b, tk=tk_b, group=qa.shape[1] // LANES, out_scale=1.0 - lam_init)
            tri = jnp.tril(jnp.ones((LANES, LANES), F32)).astype(BF16)
            out_b = _dsa_call(qi, wi, qb, kik, kb, vbt, tri, batch=batch, seq=seq, tq=tq_b, tk=tk_b,
                              k_sel=k_sel, idx_precision=IDX_PRECISION)
            w_out = w_out_e[j].astype(BF16)
            x2 = _out0_call(x2, out_a, out_b, w_out[:512], w_out[512:], *tail, seq=seq, tm=tm)
        else:
            taps = jnp.pad(conv_w[j].astype(F32), ((0, SUBLANES - CONV_W), (0, 0)))
            x2 = _mix1_call(x2, _row(g_mix[layer]), w_in_o[j].astype(BF16), taps, w_pool[j].astype(BF16),
                            _row(pool_scale[j]), w_out_o[j].astype(BF16), *tail, seq=seq, tm=tm)
        x2 = _mlp_call(x2, _row(g_mlp[layer]), w_up_b, w_down_b, layer, tm=min(1024, t), tf=2048)
    return x2.reshape(batch, seq, d)
```

```python
import functools

import numpy as np
import jax
import jax.numpy as jnp
from jax import lax
from jax.experimental import pallas as pl
from jax.experimental.pallas import tpu as pltpu

F32 = jnp.float32
BF16 = jnp.bfloat16

HEAD_DIM = 64
ROT_DIM = HEAD_DIM // 4
ROT_HALF = ROT_DIM // 2
ROPE_THETA = 500000.0
IDX_HEADS = 4
IDX_DIM = 64
TOPK_MAX = 256
CONV_W = 3
POOL_WINDOWS = (2, 4, 8, 16)
POOL_HALO = 16
X_HEADS = 4
X_HEAD_DIM = 128
EPS = 1e-6
LOG2_E = 1.4426950408889634

LANES = 128
SUBLANES = 8
VMEM_LIMIT_BYTES = 56 * 1024 * 1024

KEY_NEG_INF = np.int32(-2139095041)
INT32_MIN = np.int32(-(2 ** 31))

NT_DIMS = (((1,), (1,)), ((), ()))

IDX_PRECISION = None


def _cparams(*sem):
    return pltpu.CompilerParams(dimension_semantics=sem, vmem_limit_bytes=VMEM_LIMIT_BYTES)


def _full_spec(arr):
    nd = arr.ndim
    return pl.BlockSpec(arr.shape, lambda *_: (0,) * nd)


def _rms(x, g):
    ms = jnp.mean(x * x, axis=-1, keepdims=True)
    return x * lax.rsqrt(ms + EPS) * g


def _tile_lanes(t, width):
    reps = width // t.shape[-1]
    return t if reps == 1 else jnp.concatenate([t] * reps, axis=1)


def _rope_tables(pos_row, inv_col):
    ang = inv_col * pos_row
    cos, sin = jnp.cos(ang), jnp.sin(ang)
    rest = (HEAD_DIM - ROT_DIM, ang.shape[1])
    head_c = jnp.concatenate([cos, cos, jnp.ones(rest, F32)], axis=0)
    head_s = jnp.concatenate([-sin, sin, jnp.zeros(rest, F32)], axis=0)
    return (jnp.concatenate([head_c, head_c], axis=0).T, jnp.concatenate([head_s, head_s], axis=0).T)


def _rope(y, c, s):
    w = y.shape[-1]
    c, s = _tile_lanes(c, w), _tile_lanes(s, w)
    lane = lax.broadcasted_iota(jnp.int32, (1, w), 1)
    upper = pltpu.roll(y, w - ROT_HALF, axis=1)
    lower = pltpu.roll(y, ROT_HALF, axis=1)
    partner = jnp.where((lane & (HEAD_DIM - 1)) < ROT_HALF, upper, lower)
    return y * c + partner * s


def _headnorm64(y, blockdiag, g):
    ss = jnp.dot((y * y).astype(BF16), blockdiag, preferred_element_type=F32)
    return y * lax.rsqrt(ss * (1.0 / HEAD_DIM) + EPS) * g


def _in0_body(x_ref, gmix_ref, wmain_ref, widx_ref, wkvup_ref, pos_ref, inv_ref,
              bd_ref, gqa_ref, gka_ref, gqb_ref, gkb_ref, gkv_ref, gkidx_ref,
              qa_ref, ka_ref, vat_ref, qb_ref, kb_ref, vbt_ref, qi_ref, kik_ref, wi_ref,
              *, idx_precision):
    h = _rms(x_ref[...], gmix_ref[...])
    hb = h.astype(BF16)
    proj = lambda lo, hi: jnp.dot(hb, wmain_ref[:, lo:hi], preferred_element_type=F32)
    rot = _rope_tables(pos_ref[0].astype(F32), inv_ref[...])
    bd = bd_ref[...]
    scale = HEAD_DIM ** -0.5 * LOG2_E

    qa = _rope(_headnorm64(proj(0, 512), bd, gqa_ref[...]), *rot) * scale
    qa_ref[...] = qa.astype(BF16)
    ka = _rope(_headnorm64(proj(512, 1024), bd, gka_ref[...]), *rot)
    ka_ref[...] = ka.astype(BF16)
    tk = vat_ref.shape[2]
    va = proj(1024, 1536)
    for r in range(vat_ref.shape[0]):
        vat_ref[r] = va[r * tk:(r + 1) * tk, :].T.astype(BF16)
    qb = _rope(_headnorm64(proj(1536, 2048), bd, gqb_ref[...]), *rot) * scale
    qb_ref[...] = qb.astype(BF16)

    ckv = _rms(proj(2048, 2304), gkv_ref[...]).astype(BF16)
    kv = jnp.dot(ckv, wkvup_ref[...], preferred_element_type=F32)
    kb = _rope(_headnorm64(kv[:, 0:512], bd, gkb_ref[...]), *rot)
    kb_ref[...] = kb.astype(BF16)
    for r in range(vbt_ref.shape[0]):
        vbt_ref[r] = kv[r * tk:(r + 1) * tk, 512:1024].T.astype(BF16)

    if idx_precision is None:
        yi = jnp.dot(hb, widx_ref[...], preferred_element_type=F32)
    else:
        yi = jnp.dot(h, widx_ref[...], preferred_element_type=F32, precision=idx_precision)
    qi_ref[...] = _rope(yi[:, 0:256], *rot).astype(qi_ref.dtype)
    blk = yi[:, 256:384]
    lane = lax.broadcasted_iota(jnp.int32, blk.shape, 1)
    ms = jnp.sum(jnp.where(lane < IDX_DIM, blk * blk, 0.0), axis=-1, keepdims=True) * (1.0 / IDX_DIM)
    kin = blk * lax.rsqrt(ms + EPS) * gkidx_ref[...]
    kir = _rope(kin, *rot)
    kik = kir + pltpu.roll(kir, IDX_DIM, axis=1)
    kik_ref[...] = kik.astype(kik_ref.dtype)
    wi_ref[...] = blk * (IDX_HEADS ** -0.5 * IDX_DIM ** -0.5)


def _in0_call(x2, gmix, wmain, widx, wkvup, pos, inv, bd, gqa, gka, gqb, gkb, gkv, gkidx,
              *, tm, tk, idx_precision):
    t, d = x2.shape
    row = lambda w: pl.BlockSpec((tm, w), lambda i: (i, 0))
    fulls = [gmix, wmain, widx, wkvup]
    gains = [inv, bd, gqa, gka, gqb, gkb, gkv, gkidx]
    in_specs = ([row(d)] + [_full_spec(a) for a in fulls] + [pl.BlockSpec((1, 1, tm), lambda i: (i, 0, 0))]
                + [_full_spec(a) for a in gains])
    idx_dt = BF16 if idx_precision is None else F32
    sds = jax.ShapeDtypeStruct
    vt_shape, vt_spec = sds((t // tk, 512, tk), BF16), pl.BlockSpec((tm // tk, 512, tk), lambda i: (i, 0, 0))
    out_shape = ([sds((t, 512), BF16)] * 2 + [vt_shape] + [sds((t, 512), BF16)] * 2
                 + [vt_shape, sds((t, 256), idx_dt), sds((t, LANES), idx_dt), sds((t, LANES), F32)])
    out_specs = [row(512)] * 2 + [vt_spec] + [row(512)] * 2 + [vt_spec, row(256), row(LANES), row(LANES)]
    return pl.pallas_call(
        functools.partial(_in0_body, idx_precision=idx_precision),
        grid=(t // tm,),
        in_specs=in_specs,
        out_specs=out_specs,
        out_shape=out_shape,
        compiler_params=_cparams("parallel"),
        name="in0",
    )(x2, gmix, wmain, widx, wkvup, pos, inv, bd, gqa, gka, gqb, gkb, gkv, gkidx)


def _diff_attn_body(lam_ref, q_ref, k_ref, vt_ref, gsub_ref, o_ref, s_ref, pe_ref, *, tq, tk, out_scale):
    i = pl.program_id(2)
    q0 = i * tq
    n_full = (q0 + 1) // tk
    nkc = (q0 + tq + tk - 1) // tk
    lane = lax.broadcasted_iota(jnp.int32, (1, LANES), 1)
    lane_lo = lane < HEAD_DIM
    qpos = q0 + lax.broadcasted_iota(jnp.int32, (1, tq), 1)
    qpos2 = jnp.concatenate([qpos, qpos], axis=1)
    neg_inf = jnp.float32(-jnp.inf)
    heads = range(q_ref.shape[1] // LANES)
    q_cat = []
    for g in heads:
        q_h = q_ref[:, g * LANES:(g + 1) * LANES]
        q_cat.append(jnp.concatenate([q_h * lane_lo.astype(BF16), q_h * (~lane_lo).astype(BF16)], axis=0))

    def logit_step(g, c, m, masked):
        s = lax.dot_general(k_ref[pl.ds(pl.multiple_of(c * tk, tk), tk), g * LANES:(g + 1) * LANES],
                            q_cat[g], NT_DIMS, preferred_element_type=F32)
        if masked:
            kpos = c * tk + lax.broadcasted_iota(jnp.int32, (tk, 1), 0)
            s = jnp.where(kpos <= qpos2, s, neg_inf)
        s_ref[g % s_ref.shape[0], c] = s
        return jnp.maximum(m, jnp.max(s.reshape(tk // SUBLANES, SUBLANES, 2 * tq), axis=0))

    def pv_matmul(g, c):
        return jnp.dot(vt_ref[c, g * LANES:(g + 1) * LANES, :], pe_ref[g % pe_ref.shape[0]],
                       preferred_element_type=F32)

    def exp_step(g, c, shift, l8):
        pe = jnp.exp2(s_ref[g % s_ref.shape[0], c] - shift)
        pe_ref[g % pe_ref.shape[0]] = pe.astype(BF16)
        return l8 + jnp.sum(pe.reshape(tk // SUBLANES, SUBLANES, 2 * tq), axis=0)

    def pv_step(g, c, shift, l8, acc):
        pe = jnp.exp2(s_ref[g % s_ref.shape[0], c] - shift)
        acc = acc + jnp.dot(vt_ref[c, g * LANES:(g + 1) * LANES, :], pe.astype(BF16),
                            preferred_element_type=F32)
        return l8 + jnp.sum(pe.reshape(tk // SUBLANES, SUBLANES, 2 * tq), axis=0), acc

    def run(body, init, by_twos):
        loop = _loop_by_twos if by_twos else lax.fori_loop
        carry = loop(0, n_full, functools.partial(body, masked=False), init)
        return lax.fori_loop(n_full, nkc, functools.partial(body, masked=True), carry)

    lam = lam_ref[0]
    m_init = jnp.full((SUBLANES, 2 * tq), neg_inf, F32)
    pv_init = (jnp.zeros((SUBLANES, 2 * tq), F32), jnp.zeros((LANES, 2 * tq), F32))

    def finish(g, l8, acc):
        o_all = acc / jnp.sum(l8, axis=0, keepdims=True)
        o = o_all[:, 0:tq] - lam * o_all[:, tq:2 * tq]
        o = o * lax.rsqrt(jnp.mean(o * o, axis=0, keepdims=True) + EPS)
        o_ref[:, g * LANES:(g + 1) * LANES] = (o.T * gsub_ref[...] * out_scale).astype(o_ref.dtype)

    width = s_ref.shape[0] // 2
    stages = [heads[lo:lo + width] for lo in range(0, len(heads), width)]

    def logit_stage(stage, c, ms, masked):
        return tuple(logit_step(g, c, m, masked) for g, m in zip(stage, ms))

    def pv_stage(stage, c, shifts, carry):
        return tuple(pv_step(g, c, sh, *lc) for g, sh, lc in zip(stage, shifts, carry))

    ms = run(lambda c, ms, masked: logit_stage(stages[0], c, ms, masked), (m_init,) * width, True)
    for prev, cur in zip(stages[:-1], stages[1:]):
        shifts = [jnp.max(m, axis=0, keepdims=True) for m in ms]

        def fused(c, carry, masked, prev=prev, cur=cur, shifts=shifts):
            pv, ms = carry
            return pv_stage(prev, c, shifts, pv), logit_stage(cur, c, ms, masked)

        pv, ms = run(fused, ((pv_init,) * width, (m_init,) * width), True)
        for g, lc in zip(prev, pv):
            finish(g, *lc)
    shifts = [jnp.max(m, axis=0, keepdims=True) for m in ms]

    last = stages[-1]

    def last_chunk(c, carry):
        l8s, accs = carry
        accs = tuple(acc + pv_matmul(g, c - 1) for g, acc in zip(last, accs))
        return tuple(exp_step(g, c, sh, l8) for g, sh, l8 in zip(last, shifts, l8s)), accs

    l8s = tuple(exp_step(g, 0, sh, pv_init[0]) for g, sh in zip(last, shifts))
    l8s, accs = lax.fori_loop(1, nkc, last_chunk, (l8s, (pv_init[1],) * width))
    for g, l8, acc in zip(last, l8s, accs):
        finish(g, l8, acc + pv_matmul(g, nkc - 1))


def _diff_attn_call(lam, qa, ka, vat, gsub, *, batch, seq, tq, tk, group, out_scale):
    t = qa.shape[0]
    w = group * LANES
    nq, nk = seq // tq, seq // tk
    q_spec = pl.BlockSpec((tq, w), lambda b, h, i: (b * nq + i, h))
    return pl.pallas_call(
        functools.partial(_diff_attn_body, tq=tq, tk=tk, out_scale=out_scale),
        grid=(batch, qa.shape[1] // w, nq),
        in_specs=[pl.BlockSpec(memory_space=pltpu.SMEM), q_spec,
                  pl.BlockSpec((seq, w), lambda b, h, i: (b, h)),
                  pl.BlockSpec((nk, w, tk), lambda b, h, i: (b, h, 0)), _full_spec(gsub)],
        out_specs=q_spec,
        out_shape=jax.ShapeDtypeStruct((t, qa.shape[1]), BF16),
        scratch_shapes=[pltpu.VMEM((group, nk, tk, 2 * tq), F32), pltpu.VMEM((group // 2, tk, 2 * tq), BF16)],
        compiler_params=_cparams("parallel", "parallel", "arbitrary"),
        name="diff_attn",
    )(lam, qa, ka, vat, gsub)


def _loop_by_twos(lo, hi, step, carry):
    pairs = (hi - lo) // 2
    carry = lax.fori_loop(0, pairs, lambda i, cr: step(lo + 2 * i + 1, step(lo + 2 * i, cr)), carry)
    return lax.fori_loop(lo + 2 * pairs, hi, step, carry)


def _key_to_float(key):
    bits = key ^ ((key >> 31) & jnp.int32(0x7FFFFFFF))
    return lax.bitcast_convert_type(bits, F32)


def _dsa_body(qi_ref, wi_ref, qb_ref, kik_ref, kb_ref, vbt_ref, tri_ref, o_ref, sc_ref, bias_ref, s_ref, pe_ref,
              *, tq, tk, k_sel, idx_precision):
    i = pl.program_id(1)
    q0 = i * tq
    nkc = (q0 + tq + tk - 1) // tk
    qpos = q0 + lax.broadcasted_iota(jnp.int32, (1, tq), 1)
    lane = lax.broadcasted_iota(jnp.int32, (1, LANES), 1)
    lane_lo = lane < HEAD_DIM
    tiles = tk // LANES
    neg_inf = jnp.float32(-jnp.inf)

    wi_t = wi_ref[...].T
    qi = qi_ref[...]
    q_heads = []
    for h in range(IDX_HEADS):
        pair = qi[:, (h // 2) * LANES:(h // 2 + 1) * LANES]
        q_heads.append(pair * (lane_lo if h % 2 == 0 else ~lane_lo).astype(pair.dtype))
    qi_cat = jnp.concatenate(q_heads, axis=0)
    w_heads = [wi_t[IDX_DIM + h:IDX_DIM + h + 1, :] for h in range(IDX_HEADS)]

    def score_chunk(c, carry):
        kk = kik_ref[pl.ds(pl.multiple_of(c * tk, tk), tk), :]
        s = lax.dot_general(kk, qi_cat, NT_DIMS, preferred_element_type=F32, precision=idx_precision)
        acc = jnp.zeros((tk, tq), F32)
        for h in range(IDX_HEADS):
            acc = acc + jnp.maximum(s[:, h * tq:(h + 1) * tq], 0.0) * w_heads[h]
        kpos = c * tk + lax.broadcasted_iota(jnp.int32, (tk, 1), 0)
        sc_ref[c] = jnp.where(kpos <= qpos, acc, neg_inf)
        return carry

    _loop_by_twos(0, nkc, score_chunk, 0)

    acc_rows = 4 * SUBLANES

    def count_ge(cand):
        def body(c, acc):
            for r in range(tk // acc_rows):
                acc = acc + jnp.where(sc_ref[c, r * acc_rows:(r + 1) * acc_rows, :] >= cand, 1.0, 0.0)
            return acc
        acc = lax.fori_loop(0, nkc, body, jnp.zeros((acc_rows, tq), F32))
        return jnp.sum(acc, axis=0, keepdims=True)

    ksel = float(k_sel)

    def count_at(cand_key):
        return jnp.where(cand_key <= KEY_NEG_INF, ksel, count_ge(_key_to_float(cand_key)))

    zero_key = jnp.zeros((1, tq), jnp.int32)
    cnt = count_at(zero_key)
    thr = jnp.where(cnt >= ksel, zero_key, INT32_MIN)
    n_above = jnp.where(cnt >= ksel, 0.0, cnt)

    def bit_step(b, carry):
        thr, n_above = carry
        cand = thr + jnp.left_shift(jnp.int32(1), 30 - b)
        cnt = count_at(cand)
        return jnp.where(cnt >= ksel, cand, thr), jnp.where(cnt >= ksel, n_above, cnt)

    thr, n_above = lax.fori_loop(0, 31, bit_step, (thr, n_above))
    thr_f = _key_to_float(thr)
    above_f = _key_to_float(thr + 1)
    need = ksel - n_above
    tri = tri_ref[...]

    def select_chunk(c, taken):
        for t in range(tiles):
            st = sc_ref[c, t * LANES:(t + 1) * LANES, :]
            is_above = st >= above_f
            tied = jnp.where(is_above, 0.0, jnp.where(st >= thr_f, 1.0, 0.0))
            prefix = jnp.dot(tri, tied.astype(BF16), preferred_element_type=F32)
            tie_sel = jnp.where((taken + prefix) <= need, tied, 0.0)
            sel = jnp.where(is_above, 0.0, jnp.where(tie_sel > 0.0, 0.0, neg_inf))
            kpos = c * tk + t * LANES + lax.broadcasted_iota(jnp.int32, (LANES, 1), 0)
            bias_ref[c, t * LANES:(t + 1) * LANES, :] = jnp.where(kpos <= qpos, sel, neg_inf)
            taken = taken + prefix[LANES - 1:LANES, :]
        return taken

    _loop_by_twos(0, nkc, select_chunk, jnp.zeros((1, tq), F32))

    group = s_ref.shape[0]
    for p0 in range(0, qb_ref.shape[1] // LANES, group):
        pairs = range(p0, p0 + group)
        q_cat = []
        for p in pairs:
            q_pair = qb_ref[:, p * LANES:(p + 1) * LANES]
            q_cat.append(jnp.concatenate([q_pair * lane_lo.astype(BF16), q_pair * (~lane_lo).astype(BF16)],
                                         axis=0))

        def logit_chunk(c, carry, pairs=pairs, q_cat=q_cat):
            bias = bias_ref[c]
            bias2 = jnp.concatenate([bias, bias], axis=1)
            out = []
            for g, (p, m) in enumerate(zip(pairs, carry)):
                kc = kb_ref[pl.ds(pl.multiple_of(c * tk, tk), tk), p * LANES:(p + 1) * LANES]
                s = lax.dot_general(kc, q_cat[g], NT_DIMS, preferred_element_type=F32) + bias2
                s_ref[g, c] = s
                out.append(jnp.maximum(m, jnp.max(s.reshape(tk // SUBLANES, SUBLANES, 2 * tq), axis=0)))
            return tuple(out)

        m8 = _loop_by_twos(0, nkc, logit_chunk,
                           tuple(jnp.full((SUBLANES, 2 * tq), neg_inf, F32) for _ in pairs))
        shift = [jnp.max(m, axis=0, keepdims=True) for m in m8]

        def exp_stage(c, l8s, shift=shift):
            out = []
            for g, l8 in enumerate(l8s):
                pe = jnp.exp2(s_ref[g, c] - shift[g])
                pe_ref[g] = pe.astype(BF16)
                out.append(l8 + jnp.sum(pe.reshape(tk // SUBLANES, SUBLANES, 2 * tq), axis=0))
            return tuple(out)

        def pv_stage(c, accs, pairs=pairs):
            return tuple(acc + jnp.dot(vbt_ref[c, p * LANES:(p + 1) * LANES, :], pe_ref[g],
                                       preferred_element_type=F32)
                         for g, (p, acc) in enumerate(zip(pairs, accs)))

        def pv_chunk(c, carry):
            l8s, accs = carry
            accs = pv_stage(c - 1, accs)
            return exp_stage(c, l8s), accs

        l8s = exp_stage(0, tuple(jnp.zeros((SUBLANES, 2 * tq), F32) for _ in pairs))
        l8s, accs = lax.fori_loop(1, nkc, pv_chunk,
                                  (l8s, tuple(jnp.zeros((LANES, 2 * tq), F32) for _ in pairs)))
        accs = pv_stage(nkc - 1, accs)
        for p, l8, acc in zip(pairs, l8s, accs):
            o_all = acc / jnp.sum(l8, axis=0, keepdims=True)
            o_t = jnp.concatenate([o_all[0:HEAD_DIM, 0:tq], o_all[HEAD_DIM:LANES, tq:2 * tq]], axis=0)
            o_ref[:, p * LANES:(p + 1) * LANES] = o_t.T.astype(o_ref.dtype)


def _dsa_call(qi, wi, qb, kik, kb, vbt, tri, *, batch, seq, tq, tk, k_sel, idx_precision):
    t = qb.shape[0]
    nq, nk = seq // tq, seq // tk
    row = lambda w: pl.BlockSpec((tq, w), lambda b, i: (b * nq + i, 0))
    seqb = lambda w: pl.BlockSpec((seq, w), lambda b, i: (b, 0))
    vbt_spec = pl.BlockSpec((nk,) + vbt.shape[1:], lambda b, i: (b, 0, 0))
    return pl.pallas_call(
        functools.partial(_dsa_body, tq=tq, tk=tk, k_sel=k_sel, idx_precision=idx_precision),
        grid=(batch, nq),
        in_specs=[row(qi.shape[1]), row(LANES), row(qb.shape[1]), seqb(LANES), seqb(kb.shape[1]),
                  vbt_spec, _full_spec(tri)],
        out_specs=row(qb.shape[1]),
        out_shape=jax.ShapeDtypeStruct((t, qb.shape[1]), BF16),
        scratch_shapes=[pltpu.VMEM((nk, tk, tq), F32), pltpu.VMEM((nk, tk, tq), F32),
                        pltpu.VMEM((2, nk, tk, 2 * tq), F32), pltpu.VMEM((2, tk, 2 * tq), BF16)],
        compiler_params=_cparams("parallel", "arbitrary"),
        name="dsa",
    )(qi, wi, qb, kik, kb, vbt, tri)


def _memkv_body(mem_ref, g_ref, w_ref, gk_ref, k_ref, v_ref):
    hm = _rms(mem_ref[...], g_ref[...]).astype(BF16)
    kv = jnp.dot(hm, w_ref[...], preferred_element_type=F32)
    half = kv.shape[1] // 2
    for h in range(X_HEADS):
        kh = kv[:, h * X_HEAD_DIM:(h + 1) * X_HEAD_DIM]
        k_ref[:, h * X_HEAD_DIM:(h + 1) * X_HEAD_DIM] = _rms(kh, gk_ref[...]).astype(BF16)
    v_ref[...] = kv[:, half:].astype(BF16)


def _memkv_call(mem2, g, w, gk):
    rows, d = mem2.shape
    half = w.shape[1] // 2
    tm = min(rows, 512)
    row = lambda wd: pl.BlockSpec((tm, wd), lambda i: (i, 0))
    return pl.pallas_call(
        _memkv_body,
        grid=(rows // tm,),
        in_specs=[row(d), _full_spec(g), _full_spec(w), _full_spec(gk)],
        out_specs=[row(half), row(half)],
        out_shape=[jax.ShapeDtypeStruct((rows, half), BF16)] * 2,
        compiler_params=_cparams("parallel"),
        name="memkv",
    )(mem2, g, w, gk)


def _xattn_tail(x1, gx_ref, wq_ref, gq_ref, kx_ref, vx_ref, wo_ref):
    hx = _rms(x1, gx_ref[...]).astype(BF16)
    q = jnp.dot(hx, wq_ref[...], preferred_element_type=F32)
    outs = []
    for h in range(X_HEADS):
        cols = slice(h * X_HEAD_DIM, (h + 1) * X_HEAD_DIM)
        qh = _rms(q[:, cols], gq_ref[...]).astype(BF16)
        s = lax.dot_general(qh, kx_ref[0, :, cols], NT_DIMS, preferred_element_type=F32)
        p = jnp.exp(s - jnp.max(s, axis=-1, keepdims=True))
        o = (jnp.dot(p.astype(BF16), vx_ref[0, :, cols], preferred_element_type=F32)
             / jnp.sum(p, axis=-1, keepdims=True))
        outs.append(o.astype(BF16))
    o = jnp.concatenate(outs, axis=1)
    return x1 + jnp.dot(o, wo_ref[...], preferred_element_type=F32)


def _store_with_mlp_norm(x2, gm_ref, o_ref, hn_ref):
    o_ref[...] = x2
    hn_ref[...] = _rms(x2, gm_ref[...]).astype(BF16)


def _out0_body(x_ref, oa_ref, ob_ref, woa_ref, wob_ref, gx_ref, wq_ref, gq_ref, kx_ref, vx_ref, wo_ref,
               gm_ref, o_ref, hn_ref):
    mix = (jnp.dot(oa_ref[...], woa_ref[...], preferred_element_type=F32)
           + jnp.dot(ob_ref[...], wob_ref[...], preferred_element_type=F32))
    x2 = _xattn_tail(x_ref[...] + mix, gx_ref, wq_ref, gq_ref, kx_ref, vx_ref, wo_ref)
    _store_with_mlp_norm(x2, gm_ref, o_ref, hn_ref)


def _out0_call(x2, oa, ob, woa, wob, gx, wq, gq, kx, vx, wo, gm, *, seq, tm):
    t, d = x2.shape
    row = lambda w: pl.BlockSpec((tm, w), lambda i: (i, 0))
    mem_spec = pl.BlockSpec((1,) + kx.shape[1:], lambda i: ((i * tm) // seq, 0, 0))
    fulls = [woa, wob, gx, wq, gq]
    return pl.pallas_call(
        _out0_body,
        grid=(t // tm,),
        in_specs=[row(d), row(oa.shape[1]), row(ob.shape[1])] + [_full_spec(a) for a in fulls]
                 + [mem_spec, mem_spec, _full_spec(wo), _full_spec(gm)],
        out_specs=[row(d), row(d)],
        out_shape=[jax.ShapeDtypeStruct((t, d), F32), jax.ShapeDtypeStruct((t, d), BF16)],
        compiler_params=_cparams("parallel"),
        name="out0",
    )(x2, oa, ob, woa, wob, gx, wq, gq, kx, vx, wo, gm)


def _mlp_body(x_ref, hn_ref, wup_ref, wdn_ref, o_ref, acc_ref):
    j = pl.program_id(1)
    u = jnp.dot(hn_ref[...], wup_ref[...], preferred_element_type=F32)
    a = jnp.square(jnp.maximum(u, 0.0)).astype(BF16)
    prev = jnp.where(j == 0, 0.0, acc_ref[...])
    acc_ref[...] = prev + jnp.dot(a, wdn_ref[...], preferred_element_type=F32)

    @pl.when(j == pl.num_programs(1) - 1)
    def _():
        o_ref[...] = x_ref[...] + acc_ref[...]


def _mlp_call(x2, hn, wup, wdn, layer, *, tm, tf):
    t, d = x2.shape
    ff = wup.shape[2]
    row = pl.BlockSpec((tm, d), lambda i, j: (i, 0))
    return pl.pallas_call(
        _mlp_body,
        grid=(t // tm, ff // tf),
        in_specs=[row, row,
                  pl.BlockSpec((None, d, tf), lambda i, j: (layer, 0, j)),
                  pl.BlockSpec((None, tf, d), lambda i, j: (layer, j, 0))],
        out_specs=row,
        out_shape=jax.ShapeDtypeStruct((t, d), F32),
        scratch_shapes=[pltpu.VMEM((tm, d), F32)],
        compiler_params=_cparams("parallel", "arbitrary"),
        name="mlp",
    )(x2, hn, wup, wdn)


def _mix1_body(x_ref, xh_ref, gmix_ref, win_ref, cw_ref, wpool_ref, ps_ref, wout_ref,
               gx_ref, wq_ref, gq_ref, kx_ref, vx_ref, wo_ref, gm_ref, o_ref, hn_ref, *, tm, seq):
    i = pl.program_id(0)
    x = x_ref[...]
    xf = jnp.concatenate([xh_ref[...], x], axis=0)
    hf = _rms(xf, gmix_ref[...]).astype(BF16)
    cw = win_ref.shape[1] // 4
    proj = lambda n: jnp.dot(hf, win_ref[:, n * cw:(n + 1) * cw], preferred_element_type=F32)
    pos0 = (i * tm) % seq
    rid = lax.broadcasted_iota(jnp.int32, (tm + POOL_HALO, 1), 0)
    keep = jnp.where(rid >= POOL_HALO, 1.0, jnp.where(pos0 > 0, 1.0, 0.0))

    u = proj(1) * proj(2) * keep
    z = proj(3) * keep
    gb = proj(0)[POOL_HALO:]

    taps = cw_ref[...]
    conv = taps[CONV_W - 1:CONV_W, :] * u[POOL_HALO:]
    for back in range(1, CONV_W):
        conv = conv + taps[CONV_W - 1 - back:CONV_W - back, :] * pltpu.roll(u, back, axis=0)[POOL_HALO:]
    yc = gb * conv

    pos = pos0 + lax.broadcasted_iota(jnp.int32, (tm, 1), 0)
    gw = cw // len(POOL_WINDOWS)
    parts = []
    for g, w in enumerate(POOL_WINDOWS):
        zg = z[:, g * gw:(g + 1) * gw]
        s = zg
        sh = 1
        while sh < w:
            s = s + pltpu.roll(s, sh, axis=0)
            sh *= 2
        cnt = jnp.minimum(pos + 1, w).astype(F32)
        pooled = s[POOL_HALO:] / cnt - zg[POOL_HALO:]
        parts.append(jnp.dot(pooled.astype(BF16), wpool_ref[g], preferred_element_type=F32))
    yd = jnp.concatenate(parts, axis=1) * ps_ref[...]

    mix = (jnp.dot(yc.astype(BF16), wout_ref[0:cw, :], preferred_element_type=F32)
           + jnp.dot(yd.astype(BF16), wout_ref[cw:2 * cw, :], preferred_element_type=F32))
    x2 = _xattn_tail(x + mix, gx_ref, wq_ref, gq_ref, kx_ref, vx_ref, wo_ref)
    _store_with_mlp_norm(x2, gm_ref, o_ref, hn_ref)


def _mix1_call(x2, gmix, win, cw, wpool, ps, wout, gx, wq, gq, kx, vx, wo, gm, *, seq, tm):
    t, d = x2.shape
    row = pl.BlockSpec((tm, d), lambda i: (i, 0))
    halo = pl.BlockSpec((POOL_HALO, d), lambda i: (jnp.maximum(i * (tm // POOL_HALO) - 1, 0), 0))
    mem_spec = pl.BlockSpec((1,) + kx.shape[1:], lambda i: ((i * tm) // seq, 0, 0))
    fulls = [gmix, win, cw, wpool, ps, wout, gx, wq, gq]
    return pl.pallas_call(
        functools.partial(_mix1_body, tm=tm, seq=seq),
        grid=(t // tm,),
        in_specs=[row, halo] + [_full_spec(a) for a in fulls]
                 + [mem_spec, mem_spec, _full_spec(wo), _full_spec(gm)],
        out_specs=[row, row],
        out_shape=[jax.ShapeDtypeStruct((t, d), F32), jax.ShapeDtypeStruct((t, d), BF16)],
        compiler_params=_cparams("parallel"),
        name="mix1",
    )(x2, x2, gmix, win, cw, wpool, ps, wout, gx, wq, gq, kx, vx, wo, gm)


def _row(v, reps=1):
    return jnp.tile(v.astype(F32), reps).reshape(1, -1)


def kernel(x, mem, positions, g_mix, g_xattn, g_mem, g_mlp, wq_x, wkv_x, gq_x, gk_x, wo_x, w_up, w_down,
           w_in_e, gq_a, gk_a, lam_q1, lam_k1, lam_q2, lam_k2, g_sub_a, g_kv_b, w_kv_up_b, gq_b, gk_b,
           g_kidx, w_out_e, w_in_o, conv_w, w_pool, pool_scale, w_out_o):
    batch, seq, d = x.shape
    t = batch * seq
    k_sel = min(TOPK_MAX, seq // 4)
    tm = min(1024, seq)
    tq_b, tk_b = min(256, seq), min(512, seq)
    assert seq % tm == 0 and seq % tq_b == 0 and seq % tk_b == 0 and tk_b >= k_sel

    x2 = x.reshape(t, d)
    pos = positions.reshape(t // tm, 1, tm)
    inv_freq = (ROPE_THETA ** (-jnp.arange(0, ROT_DIM, 2, dtype=F32) / ROT_DIM)).reshape(ROT_HALF, 1)
    mem2 = mem.reshape(batch * mem.shape[1], d)
    depth = g_mix.shape[0]
    w_up_b, w_down_b = w_up.astype(BF16), w_down.astype(BF16)

    for layer in range(depth):
        j = layer // 2
        kx, vx = _memkv_call(mem2, _row(g_mem[layer]), wkv_x[layer].astype(BF16), _row(gk_x[layer]))
        kx = kx.reshape(batch, mem.shape[1], -1)
        vx = vx.reshape(batch, mem.shape[1], -1)
        tail = (_row(g_xattn[layer]), wq_x[layer].astype(BF16), _row(gq_x[layer]) * (X_HEAD_DIM ** -0.5),
                kx, vx, wo_x[layer].astype(BF16), _row(g_mlp[layer]))
        if layer % 2 == 0:
            lam_init = 0.8 - 0.6 * float(np.exp(-0.3 * layer))
            lam = (jnp.exp(jnp.sum(lam_q1[j].astype(F32) * lam_k1[j].astype(F32)))
                   - jnp.exp(jnp.sum(lam_q2[j].astype(F32) * lam_k2[j].astype(F32))) + lam_init)
            w_in = w_in_e[j]
            n_main = 2304
            n_idx = w_in.shape[1] - n_main
            widx = jnp.pad(w_in[:, n_main:], ((0, 0), (0, 3 * LANES - n_idx)))
            if IDX_PRECISION is None:
                widx = widx.astype(BF16)
            heads8 = 512 // HEAD_DIM
            bd = jnp.kron(jnp.eye(heads8, dtype=F32), jnp.ones((HEAD_DIM, HEAD_DIM), F32)).astype(BF16)
            gkidx = jnp.pad(g_kidx[j].astype(F32), (0, LANES - IDX_DIM)).reshape(1, LANES)
            qa, ka, vat, qb, kb, vbt, qi, kik, wi = _in0_call(
                x2, _row(g_mix[layer]), w_in[:, :n_main].astype(BF16), widx, w_kv_up_b[j].astype(BF16),
                pos, inv_freq, bd, _row(gq_a[j], heads8), _row(gk_a[j], heads8), _row(gq_b[j], heads8),
                _row(gk_b[j], heads8), _row(g_kv_b[j]), gkidx, tm=tm, tk=tk_b, idx_precision=IDX_PRECISION)
            out_a = _diff_attn_call(lam.reshape(1).astype(F32), qa, ka, vat, _row(g_sub_a[j]),
                                    batch=batch, seq=seq, tq=tq_b, tk=tk_b, group=qa.shape[1] // LANES, out_scale=1.0 - lam_init)
            tri = jnp.tril(jnp.ones((LANES, LANES), F32)).astype(BF16)
            out_b = _dsa_call(qi, wi, qb, kik, kb, vbt, tri, batch=batch, seq=seq, tq=tq_b, tk=tk_b,
                              k_sel=k_sel, idx_precision=IDX_PRECISION)
            w_out = w_out_e[j].astype(BF16)
            x2, hn = _out0_call(x2, out_a, out_b, w_out[:512], w_out[512:], *tail, seq=seq, tm=tm)
        else:
            taps = jnp.pad(conv_w[j].astype(F32), ((0, SUBLANES - CONV_W), (0, 0)))
            x2, hn = _mix1_call(x2, _row(g_mix[layer]), w_in_o[j].astype(BF16), taps, w_pool[j].astype(BF16),
                                _row(pool_scale[j]), w_out_o[j].astype(BF16), *tail, seq=seq, tm=tm)
        x2 = _mlp_call(x2, hn, w_up_b, w_down_b, layer, tm=min(1024, t), tf=2048)
    return x2.reshape(batch, seq, d)
```

```python
import functools

import numpy as np
import jax
import jax.numpy as jnp
from jax import lax
from jax.experimental import pallas as pl
from jax.experimental.pallas import tpu as pltpu

F32 = jnp.float32
BF16 = jnp.bfloat16

HEAD_DIM = 64
ROT_DIM = HEAD_DIM // 4
ROT_HALF = ROT_DIM // 2
ROPE_THETA = 500000.0
IDX_HEADS = 4
IDX_DIM = 64
TOPK_MAX = 256
CONV_W = 3
POOL_WINDOWS = (2, 4, 8, 16)
POOL_HALO = 16
X_HEADS = 4
X_HEAD_DIM = 128
EPS = 1e-6
LOG2_E = 1.4426950408889634

LANES = 128
SUBLANES = 8
VMEM_LIMIT_BYTES = 56 * 1024 * 1024

KEY_NEG_INF = np.int32(-2139095041)
INT32_MIN = np.int32(-(2 ** 31))

NT_DIMS = (((1,), (1,)), ((), ()))

IDX_PRECISION = None


def _cparams(*sem):
    return pltpu.CompilerParams(dimension_semantics=sem, vmem_limit_bytes=VMEM_LIMIT_BYTES)


def _full_spec(arr):
    nd = arr.ndim
    return pl.BlockSpec(arr.shape, lambda *_: (0,) * nd)


def _rms(x, g):
    ms = jnp.mean(x * x, axis=-1, keepdims=True)
    return x * lax.rsqrt(ms + EPS) * g


def _tile_lanes(t, width):
    reps = width // t.shape[-1]
    return t if reps == 1 else jnp.concatenate([t] * reps, axis=1)


def _rope_tables(pos_row, inv_col):
    ang = inv_col * pos_row
    cos, sin = jnp.cos(ang), jnp.sin(ang)
    rest = (HEAD_DIM - ROT_DIM, ang.shape[1])
    head_c = jnp.concatenate([cos, cos, jnp.ones(rest, F32)], axis=0)
    head_s = jnp.concatenate([-sin, sin, jnp.zeros(rest, F32)], axis=0)
    return (jnp.concatenate([head_c, head_c], axis=0).T, jnp.concatenate([head_s, head_s], axis=0).T)


def _rope(y, c, s):
    w = y.shape[-1]
    c, s = _tile_lanes(c, w), _tile_lanes(s, w)
    lane = lax.broadcasted_iota(jnp.int32, (1, w), 1)
    upper = pltpu.roll(y, w - ROT_HALF, axis=1)
    lower = pltpu.roll(y, ROT_HALF, axis=1)
    partner = jnp.where((lane & (HEAD_DIM - 1)) < ROT_HALF, upper, lower)
    return y * c + partner * s


def _headnorm64(y, blockdiag, g):
    ss = jnp.dot((y * y).astype(BF16), blockdiag, preferred_element_type=F32)
    return y * lax.rsqrt(ss * (1.0 / HEAD_DIM) + EPS) * g


def _in0_body(x_ref, gmix_ref, wmain_ref, widx_ref, wkvup_ref, pos_ref, inv_ref,
              bd_ref, gqa_ref, gka_ref, gqb_ref, gkb_ref, gkv_ref, gkidx_ref,
              qa_ref, ka_ref, vat_ref, qb_ref, kb_ref, vbt_ref, qi_ref, kik_ref, wi_ref,
              *, idx_precision):
    h = _rms(x_ref[...], gmix_ref[...])
    hb = h.astype(BF16)
    proj = lambda lo, hi: jnp.dot(hb, wmain_ref[:, lo:hi], preferred_element_type=F32)
    rot = _rope_tables(pos_ref[0].astype(F32), inv_ref[...])
    bd = bd_ref[...]
    scale = HEAD_DIM ** -0.5 * LOG2_E

    qa = _rope(_headnorm64(proj(0, 512), bd, gqa_ref[...]), *rot) * scale
    qa_ref[...] = qa.astype(BF16)
    ka = _rope(_headnorm64(proj(512, 1024), bd, gka_ref[...]), *rot)
    ka_ref[...] = ka.astype(BF16)
    tk = vat_ref.shape[2]
    va = proj(1024, 1536)
    for r in range(vat_ref.shape[0]):
        vat_ref[r] = va[r * tk:(r + 1) * tk, :].T.astype(BF16)
    qb = _rope(_headnorm64(proj(1536, 2048), bd, gqb_ref[...]), *rot) * scale
    qb_ref[...] = qb.astype(BF16)

    ckv = _rms(proj(2048, 2304), gkv_ref[...]).astype(BF16)
    kv = jnp.dot(ckv, wkvup_ref[...], preferred_element_type=F32)
    kb = _rope(_headnorm64(kv[:, 0:512], bd, gkb_ref[...]), *rot)
    kb_ref[...] = kb.astype(BF16)
    for r in range(vbt_ref.shape[0]):
        vbt_ref[r] = kv[r * tk:(r + 1) * tk, 512:1024].T.astype(BF16)

    if idx_precision is None:
        yi = jnp.dot(hb, widx_ref[...], preferred_element_type=F32)
    else:
        yi = jnp.dot(h, widx_ref[...], preferred_element_type=F32, precision=idx_precision)
    qi_ref[...] = _rope(yi[:, 0:256], *rot).astype(qi_ref.dtype)
    blk = yi[:, 256:384]
    lane = lax.broadcasted_iota(jnp.int32, blk.shape, 1)
    ms = jnp.sum(jnp.where(lane < IDX_DIM, blk * blk, 0.0), axis=-1, keepdims=True) * (1.0 / IDX_DIM)
    kin = blk * lax.rsqrt(ms + EPS) * gkidx_ref[...]
    kir = _rope(kin, *rot)
    kik = kir + pltpu.roll(kir, IDX_DIM, axis=1)
    kik_ref[...] = kik.astype(kik_ref.dtype)
    wi_ref[...] = blk * (IDX_HEADS ** -0.5 * IDX_DIM ** -0.5)


def _in0_call(x2, gmix, wmain, widx, wkvup, pos, inv, bd, gqa, gka, gqb, gkb, gkv, gkidx,
              *, tm, tk, idx_precision):
    t, d = x2.shape
    row = lambda w: pl.BlockSpec((tm, w), lambda i: (i, 0))
    fulls = [gmix, wmain, widx, wkvup]
    gains = [inv, bd, gqa, gka, gqb, gkb, gkv, gkidx]
    in_specs = ([row(d)] + [_full_spec(a) for a in fulls] + [pl.BlockSpec((1, 1, tm), lambda i: (i, 0, 0))]
                + [_full_spec(a) for a in gains])
    idx_dt = BF16 if idx_precision is None else F32
    sds = jax.ShapeDtypeStruct
    vt_shape, vt_spec = sds((t // tk, 512, tk), BF16), pl.BlockSpec((tm // tk, 512, tk), lambda i: (i, 0, 0))
    out_shape = ([sds((t, 512), BF16)] * 2 + [vt_shape] + [sds((t, 512), BF16)] * 2
                 + [vt_shape, sds((t, 256), idx_dt), sds((t, LANES), idx_dt), sds((t, LANES), F32)])
    out_specs = [row(512)] * 2 + [vt_spec] + [row(512)] * 2 + [vt_spec, row(256), row(LANES), row(LANES)]
    return pl.pallas_call(
        functools.partial(_in0_body, idx_precision=idx_precision),
        grid=(t // tm,),
        in_specs=in_specs,
        out_specs=out_specs,
        out_shape=out_shape,
        compiler_params=_cparams("parallel"),
        name="in0",
    )(x2, gmix, wmain, widx, wkvup, pos, inv, bd, gqa, gka, gqb, gkb, gkv, gkidx)


def _diff_attn_body(lam_ref, q_ref, k_ref, vt_ref, gsub_ref, o_ref, s_ref, pe_ref, *, tq, tk, out_scale):
    i = pl.program_id(2)
    q0 = i * tq
    n_full = (q0 + 1) // tk
    nkc = (q0 + tq + tk - 1) // tk
    lane = lax.broadcasted_iota(jnp.int32, (1, LANES), 1)
    lane_lo = lane < HEAD_DIM
    qpos = q0 + lax.broadcasted_iota(jnp.int32, (1, tq), 1)
    qpos2 = jnp.concatenate([qpos, qpos], axis=1)
    neg_inf = jnp.float32(-jnp.inf)
    heads = range(q_ref.shape[1] // LANES)
    q_cat = []
    for g in heads:
        q_h = q_ref[:, g * LANES:(g + 1) * LANES]
        q_cat.append(jnp.concatenate([q_h * lane_lo.astype(BF16), q_h * (~lane_lo).astype(BF16)], axis=0))

    def logit_step(g, c, m, masked):
        s = lax.dot_general(k_ref[pl.ds(pl.multiple_of(c * tk, tk), tk), g * LANES:(g + 1) * LANES],
                            q_cat[g], NT_DIMS, preferred_element_type=F32)
        if masked:
            kpos = c * tk + lax.broadcasted_iota(jnp.int32, (tk, 1), 0)
            s = jnp.where(kpos <= qpos2, s, neg_inf)
        s_ref[g % s_ref.shape[0], c] = s
        return jnp.maximum(m, jnp.max(s.reshape(tk // SUBLANES, SUBLANES, 2 * tq), axis=0))

    def pv_matmul(g, c):
        return jnp.dot(vt_ref[c, g * LANES:(g + 1) * LANES, :], pe_ref[g % pe_ref.shape[0]],
                       preferred_element_type=F32)

    def exp_step(g, c, shift, l8):
        pe = jnp.exp2(s_ref[g % s_ref.shape[0], c] - shift)
        pe_ref[g % pe_ref.shape[0]] = pe.astype(BF16)
        return l8 + jnp.sum(pe.reshape(tk // SUBLANES, SUBLANES, 2 * tq), axis=0)

    def pv_step(g, c, shift, l8, acc):
        pe = jnp.exp2(s_ref[g % s_ref.shape[0], c] - shift)
        acc = acc + jnp.dot(vt_ref[c, g * LANES:(g + 1) * LANES, :], pe.astype(BF16),
                            preferred_element_type=F32)
        return l8 + jnp.sum(pe.reshape(tk // SUBLANES, SUBLANES, 2 * tq), axis=0), acc

    def run(body, init):
        carry = _loop_by_twos(0, n_full, functools.partial(body, masked=False), init)
        return lax.fori_loop(n_full, nkc, functools.partial(body, masked=True), carry)

    lam = lam_ref[0]
    m_init = jnp.full((SUBLANES, 2 * tq), neg_inf, F32)
    pv_init = (jnp.zeros((SUBLANES, 2 * tq), F32), jnp.zeros((LANES, 2 * tq), F32))

    def finish(g, l8, acc):
        o_all = acc / jnp.sum(l8, axis=0, keepdims=True)
        o = o_all[:, 0:tq] - lam * o_all[:, tq:2 * tq]
        o = o * lax.rsqrt(jnp.mean(o * o, axis=0, keepdims=True) + EPS)
        o_ref[:, g * LANES:(g + 1) * LANES] = (o.T * gsub_ref[...] * out_scale).astype(o_ref.dtype)

    width = s_ref.shape[0] // 2
    stages = [heads[lo:lo + width] for lo in range(0, len(heads), width)]

    def logit_stage(stage, c, ms, masked):
        return tuple(logit_step(g, c, m, masked) for g, m in zip(stage, ms))

    def pv_stage(stage, c, shifts, carry):
        return tuple(pv_step(g, c, sh, *lc) for g, sh, lc in zip(stage, shifts, carry))

    ms = run(lambda c, ms, masked: logit_stage(stages[0], c, ms, masked), (m_init,) * width)
    for prev, cur in zip(stages[:-1], stages[1:]):
        shifts = [jnp.max(m, axis=0, keepdims=True) for m in ms]

        def fused(c, carry, masked, prev=prev, cur=cur, shifts=shifts):
            pv, ms = carry
            return pv_stage(prev, c, shifts, pv), logit_stage(cur, c, ms, masked)

        pv, ms = run(fused, ((pv_init,) * width, (m_init,) * width))
        for g, lc in zip(prev, pv):
            finish(g, *lc)
    shifts = [jnp.max(m, axis=0, keepdims=True) for m in ms]

    last = stages[-1]

    def last_chunk(c, carry):
        l8s, accs = carry
        accs = tuple(acc + pv_matmul(g, c - 1) for g, acc in zip(last, accs))
        return tuple(exp_step(g, c, sh, l8) for g, sh, l8 in zip(last, shifts, l8s)), accs

    l8s = tuple(exp_step(g, 0, sh, pv_init[0]) for g, sh in zip(last, shifts))
    l8s, accs = lax.fori_loop(1, nkc, last_chunk, (l8s, (pv_init[1],) * width))
    for g, l8, acc in zip(last, l8s, accs):
        finish(g, l8, acc + pv_matmul(g, nkc - 1))


def _diff_attn_call(lam, qa, ka, vat, gsub, *, batch, seq, tq, tk, group, out_scale):
    t = qa.shape[0]
    w = group * LANES
    nq, nk = seq // tq, seq // tk
    q_spec = pl.BlockSpec((tq, w), lambda b, h, i: (b * nq + i, h))
    return pl.pallas_call(
        functools.partial(_diff_attn_body, tq=tq, tk=tk, out_scale=out_scale),
        grid=(batch, qa.shape[1] // w, nq),
        in_specs=[pl.BlockSpec(memory_space=pltpu.SMEM), q_spec,
                  pl.BlockSpec((seq, w), lambda b, h, i: (b, h)),
                  pl.BlockSpec((nk, w, tk), lambda b, h, i: (b, h, 0)), _full_spec(gsub)],
        out_specs=q_spec,
        out_shape=jax.ShapeDtypeStruct((t, qa.shape[1]), BF16),
        scratch_shapes=[pltpu.VMEM((group, nk, tk, 2 * tq), F32), pltpu.VMEM((group // 2, tk, 2 * tq), BF16)],
        compiler_params=_cparams("parallel", "parallel", "arbitrary"),
        name="diff_attn",
    )(lam, qa, ka, vat, gsub)


def _loop_by_twos(lo, hi, step, carry):
    pairs = (hi - lo) // 2
    carry = lax.fori_loop(0, pairs, lambda i, cr: step(lo + 2 * i + 1, step(lo + 2 * i, cr)), carry)
    return lax.fori_loop(lo + 2 * pairs, hi, step, carry)


def _key_to_float(key):
    bits = key ^ ((key >> 31) & jnp.int32(0x7FFFFFFF))
    return lax.bitcast_convert_type(bits, F32)


def _dsa_body(qi_ref, wi_ref, qb_ref, kik_ref, kb_ref, vbt_ref, tri_ref, o_ref, sc_ref, bias_ref, s_ref, pe_ref,
              *, tq, tk, k_sel, idx_precision):
    i = pl.program_id(1)
    q0 = i * tq
    nkc = (q0 + tq + tk - 1) // tk
    qpos = q0 + lax.broadcasted_iota(jnp.int32, (1, tq), 1)
    lane = lax.broadcasted_iota(jnp.int32, (1, LANES), 1)
    lane_lo = lane < HEAD_DIM
    tiles = tk // LANES
    neg_inf = jnp.float32(-jnp.inf)

    wi_t = wi_ref[...].T
    qi = qi_ref[...]
    q_heads = []
    for h in range(IDX_HEADS):
        pair = qi[:, (h // 2) * LANES:(h // 2 + 1) * LANES]
        q_heads.append(pair * (lane_lo if h % 2 == 0 else ~lane_lo).astype(pair.dtype))
    qi_cat = jnp.concatenate(q_heads, axis=0)
    w_heads = [wi_t[IDX_DIM + h:IDX_DIM + h + 1, :] for h in range(IDX_HEADS)]

    def score_chunk(c, carry):
        kk = kik_ref[pl.ds(pl.multiple_of(c * tk, tk), tk), :]
        s = lax.dot_general(kk, qi_cat, NT_DIMS, preferred_element_type=F32, precision=idx_precision)
        acc = jnp.zeros((tk, tq), F32)
        for h in range(IDX_HEADS):
            acc = acc + jnp.maximum(s[:, h * tq:(h + 1) * tq], 0.0) * w_heads[h]
        kpos = c * tk + lax.broadcasted_iota(jnp.int32, (tk, 1), 0)
        sc_ref[c] = jnp.where(kpos <= qpos, acc, neg_inf)
        return carry

    _loop_by_twos(0, nkc, score_chunk, 0)

    acc_rows = 4 * SUBLANES

    def count_ge(cand):
        def body(c, acc):
            for r in range(tk // acc_rows):
                acc = acc + jnp.where(sc_ref[c, r * acc_rows:(r + 1) * acc_rows, :] >= cand, 1.0, 0.0)
            return acc
        acc = lax.fori_loop(0, nkc, body, jnp.zeros((acc_rows, tq), F32))
        return jnp.sum(acc, axis=0, keepdims=True)

    ksel = float(k_sel)

    def count_at(cand_key):
        return jnp.where(cand_key <= KEY_NEG_INF, ksel, count_ge(_key_to_float(cand_key)))

    zero_key = jnp.zeros((1, tq), jnp.int32)
    cnt = count_at(zero_key)
    thr = jnp.where(cnt >= ksel, zero_key, INT32_MIN)
    n_above = jnp.where(cnt >= ksel, 0.0, cnt)

    def bit_step(b, carry):
        thr, n_above = carry
        cand = thr + jnp.left_shift(jnp.int32(1), 30 - b)
        cnt = count_at(cand)
        return jnp.where(cnt >= ksel, cand, thr), jnp.where(cnt >= ksel, n_above, cnt)

    thr, n_above = lax.fori_loop(0, 31, bit_step, (thr, n_above))
    thr_f = _key_to_float(thr)
    above_f = _key_to_float(thr + 1)
    need = ksel - n_above
    tri = tri_ref[...]

    def select_chunk(c, taken):
        for t in range(tiles):
            st = sc_ref[c, t * LANES:(t + 1) * LANES, :]
            is_above = st >= above_f
            tied = jnp.where(is_above, 0.0, jnp.where(st >= thr_f, 1.0, 0.0))
            prefix = jnp.dot(tri, tied.astype(BF16), preferred_element_type=F32)
            tie_sel = jnp.where((taken + prefix) <= need, tied, 0.0)
            sel = jnp.where(is_above, 0.0, jnp.where(tie_sel > 0.0, 0.0, neg_inf))
            kpos = c * tk + t * LANES + lax.broadcasted_iota(jnp.int32, (LANES, 1), 0)
            bias_ref[c, t * LANES:(t + 1) * LANES, :] = jnp.where(kpos <= qpos, sel, neg_inf)
            taken = taken + prefix[LANES - 1:LANES, :]
        return taken

    _loop_by_twos(0, nkc, select_chunk, jnp.zeros((1, tq), F32))

    group = s_ref.shape[0]
    for p0 in range(0, qb_ref.shape[1] // LANES, group):
        pairs = range(p0, p0 + group)
        q_cat = []
        for p in pairs:
            q_pair = qb_ref[:, p * LANES:(p + 1) * LANES]
            q_cat.append(jnp.concatenate([q_pair * lane_lo.astype(BF16), q_pair * (~lane_lo).astype(BF16)],
                                         axis=0))

        def logit_chunk(c, carry, pairs=pairs, q_cat=q_cat):
            bias = bias_ref[c]
            bias2 = jnp.concatenate([bias, bias], axis=1)
            out = []
            for g, (p, m) in enumerate(zip(pairs, carry)):
                kc = kb_ref[pl.ds(pl.multiple_of(c * tk, tk), tk), p * LANES:(p + 1) * LANES]
                s = lax.dot_general(kc, q_cat[g], NT_DIMS, preferred_element_type=F32) + bias2
                s_ref[g, c] = s
                out.append(jnp.maximum(m, jnp.max(s.reshape(tk // SUBLANES, SUBLANES, 2 * tq), axis=0)))
            return tuple(out)

        m8 = _loop_by_twos(0, nkc, logit_chunk,
                           tuple(jnp.full((SUBLANES, 2 * tq), neg_inf, F32) for _ in pairs))
        shift = [jnp.max(m, axis=0, keepdims=True) for m in m8]

        def exp_stage(c, l8s, shift=shift):
            out = []
            for g, l8 in enumerate(l8s):
                pe = jnp.exp2(s_ref[g, c] - shift[g])
                pe_ref[g] = pe.astype(BF16)
                out.append(l8 + jnp.sum(pe.reshape(tk // SUBLANES, SUBLANES, 2 * tq), axis=0))
            return tuple(out)

        def pv_stage(c, accs, pairs=pairs):
            return tuple(acc + jnp.dot(vbt_ref[c, p * LANES:(p + 1) * LANES, :], pe_ref[g],
                                       preferred_element_type=F32)
                         for g, (p, acc) in enumerate(zip(pairs, accs)))

        def pv_chunk(c, carry):
            l8s, accs = carry
            accs = pv_stage(c - 1, accs)
            return exp_stage(c, l8s), accs

        l8s = exp_stage(0, tuple(jnp.zeros((SUBLANES, 2 * tq), F32) for _ in pairs))
        l8s, accs = lax.fori_loop(1, nkc, pv_chunk,
                                  (l8s, tuple(jnp.zeros((LANES, 2 * tq), F32) for _ in pairs)))
        accs = pv_stage(nkc - 1, accs)
        for p, l8, acc in zip(pairs, l8s, accs):
            o_all = acc / jnp.sum(l8, axis=0, keepdims=True)
            o_t = jnp.concatenate([o_all[0:HEAD_DIM, 0:tq], o_all[HEAD_DIM:LANES, tq:2 * tq]], axis=0)
            o_ref[:, p * LANES:(p + 1) * LANES] = o_t.T.astype(o_ref.dtype)


def _dsa_call(qi, wi, qb, kik, kb, vbt, tri, *, batch, seq, tq, tk, k_sel, idx_precision):
    t = qb.shape[0]
    nq, nk = seq // tq, seq // tk
    row = lambda w: pl.BlockSpec((tq, w), lambda b, i: (b * nq + i, 0))
    seqb = lambda w: pl.BlockSpec((seq, w), lambda b, i: (b, 0))
    vbt_spec = pl.BlockSpec((nk,) + vbt.shape[1:], lambda b, i: (b, 0, 0))
    return pl.pallas_call(
        functools.partial(_dsa_body, tq=tq, tk=tk, k_sel=k_sel, idx_precision=idx_precision),
        grid=(batch, nq),
        in_specs=[row(qi.shape[1]), row(LANES), row(qb.shape[1]), seqb(LANES), seqb(kb.shape[1]),
                  vbt_spec, _full_spec(tri)],
        out_specs=row(qb.shape[1]),
        out_shape=jax.ShapeDtypeStruct((t, qb.shape[1]), BF16),
        scratch_shapes=[pltpu.VMEM((nk, tk, tq), F32), pltpu.VMEM((nk, tk, tq), F32),
                        pltpu.VMEM((2, nk, tk, 2 * tq), F32), pltpu.VMEM((2, tk, 2 * tq), BF16)],
        compiler_params=_cparams("parallel", "arbitrary"),
        name="dsa",
    )(qi, wi, qb, kik, kb, vbt, tri)


def _memkv_body(mem_ref, g_ref, w_ref, gk_ref, k_ref, v_ref):
    hm = _rms(mem_ref[...], g_ref[...]).astype(BF16)
    kv = jnp.dot(hm, w_ref[...], preferred_element_type=F32)
    half = kv.shape[1] // 2
    for h in range(X_HEADS):
        kh = kv[:, h * X_HEAD_DIM:(h + 1) * X_HEAD_DIM]
        k_ref[:, h * X_HEAD_DIM:(h + 1) * X_HEAD_DIM] = _rms(kh, gk_ref[...]).astype(BF16)
    v_ref[...] = kv[:, half:].astype(BF16)


def _memkv_call(mem2, g, w, gk):
    rows, d = mem2.shape
    half = w.shape[1] // 2
    tm = min(rows, 512)
    row = lambda wd: pl.BlockSpec((tm, wd), lambda i: (i, 0))
    return pl.pallas_call(
        _memkv_body,
        grid=(rows // tm,),
        in_specs=[row(d), _full_spec(g), _full_spec(w), _full_spec(gk)],
        out_specs=[row(half), row(half)],
        out_shape=[jax.ShapeDtypeStruct((rows, half), BF16)] * 2,
        compiler_params=_cparams("parallel"),
        name="memkv",
    )(mem2, g, w, gk)


def _xattn_tail(x1, gx_ref, wq_ref, gq_ref, kx_ref, vx_ref, wo_ref):
    hx = _rms(x1, gx_ref[...]).astype(BF16)
    q = jnp.dot(hx, wq_ref[...], preferred_element_type=F32)
    outs = []
    for h in range(X_HEADS):
        cols = slice(h * X_HEAD_DIM, (h + 1) * X_HEAD_DIM)
        qh = _rms(q[:, cols], gq_ref[...]).astype(BF16)
        s = lax.dot_general(qh, kx_ref[0, :, cols], NT_DIMS, preferred_element_type=F32)
        p = jnp.exp(s - jnp.max(s, axis=-1, keepdims=True))
        o = (jnp.dot(p.astype(BF16), vx_ref[0, :, cols], preferred_element_type=F32)
             / jnp.sum(p, axis=-1, keepdims=True))
        outs.append(o.astype(BF16))
    o = jnp.concatenate(outs, axis=1)
    return x1 + jnp.dot(o, wo_ref[...], preferred_element_type=F32)


def _store_with_mlp_norm(x2, gm_ref, o_ref, hn_ref):
    o_ref[...] = x2
    hn_ref[...] = _rms(x2, gm_ref[...]).astype(BF16)


def _out0_body(x_ref, oa_ref, ob_ref, woa_ref, wob_ref, gx_ref, wq_ref, gq_ref, kx_ref, vx_ref, wo_ref,
               gm_ref, o_ref, hn_ref):
    mix = (jnp.dot(oa_ref[...], woa_ref[...], preferred_element_type=F32)
           + jnp.dot(ob_ref[...], wob_ref[...], preferred_element_type=F32))
    x2 = _xattn_tail(x_ref[...] + mix, gx_ref, wq_ref, gq_ref, kx_ref, vx_ref, wo_ref)
    _store_with_mlp_norm(x2, gm_ref, o_ref, hn_ref)


def _out0_call(x2, oa, ob, woa, wob, gx, wq, gq, kx, vx, wo, gm, *, seq, tm):
    t, d = x2.shape
    row = lambda w: pl.BlockSpec((tm, w), lambda i: (i, 0))
    mem_spec = pl.BlockSpec((1,) + kx.shape[1:], lambda i: ((i * tm) // seq, 0, 0))
    fulls = [woa, wob, gx, wq, gq]
    return pl.pallas_call(
        _out0_body,
        grid=(t // tm,),
        in_specs=[row(d), row(oa.shape[1]), row(ob.shape[1])] + [_full_spec(a) for a in fulls]
                 + [mem_spec, mem_spec, _full_spec(wo), _full_spec(gm)],
        out_specs=[row(d), row(d)],
        out_shape=[jax.ShapeDtypeStruct((t, d), F32), jax.ShapeDtypeStruct((t, d), BF16)],
        compiler_params=_cparams("parallel"),
        name="out0",
    )(x2, oa, ob, woa, wob, gx, wq, gq, kx, vx, wo, gm)


def _mlp_body(x_ref, hn_ref, wup_ref, wdn_ref, o_ref, acc_ref):
    j = pl.program_id(1)
    u = jnp.dot(hn_ref[...], wup_ref[...], preferred_element_type=F32)
    a = jnp.square(jnp.maximum(u, 0.0)).astype(BF16)
    prev = jnp.where(j == 0, 0.0, acc_ref[...])
    acc_ref[...] = prev + jnp.dot(a, wdn_ref[...], preferred_element_type=F32)

    @pl.when(j == pl.num_programs(1) - 1)
    def _():
        o_ref[...] = x_ref[...] + acc_ref[...]


def _mlp_call(x2, hn, wup, wdn, layer, *, tm, tf):
    t, d = x2.shape
    ff = wup.shape[2]
    row = pl.BlockSpec((tm, d), lambda i, j: (i, 0))
    return pl.pallas_call(
        _mlp_body,
        grid=(t // tm, ff // tf),
        in_specs=[row, row,
                  pl.BlockSpec((None, d, tf), lambda i, j: (layer, 0, j)),
                  pl.BlockSpec((None, tf, d), lambda i, j: (layer, j, 0))],
        out_specs=row,
        out_shape=jax.ShapeDtypeStruct((t, d), F32),
        scratch_shapes=[pltpu.VMEM((tm, d), F32)],
        compiler_params=_cparams("parallel", "arbitrary"),
        name="mlp",
    )(x2, hn, wup, wdn)


def _mix1_body(x_ref, xh_ref, gmix_ref, win_ref, cw_ref, wpool_ref, ps_ref, wout_ref,
               gx_ref, wq_ref, gq_ref, kx_ref, vx_ref, wo_ref, gm_ref, o_ref, hn_ref, *, tm, seq):
    i = pl.program_id(0)
    x = x_ref[...]
    xf = jnp.concatenate([xh_ref[...], x], axis=0)
    hf = _rms(xf, gmix_ref[...]).astype(BF16)
    cw = win_ref.shape[1] // 4
    proj = lambda n: jnp.dot(hf, win_ref[:, n * cw:(n + 1) * cw], preferred_element_type=F32)
    pos0 = (i * tm) % seq
    rid = lax.broadcasted_iota(jnp.int32, (tm + POOL_HALO, 1), 0)
    keep = jnp.where(rid >= POOL_HALO, 1.0, jnp.where(pos0 > 0, 1.0, 0.0))

    u = proj(1) * proj(2) * keep
    z = proj(3) * keep
    gb = proj(0)[POOL_HALO:]

    taps = cw_ref[...]
    conv = taps[CONV_W - 1:CONV_W, :] * u[POOL_HALO:]
    for back in range(1, CONV_W):
        conv = conv + taps[CONV_W - 1 - back:CONV_W - back, :] * pltpu.roll(u, back, axis=0)[POOL_HALO:]
    yc = gb * conv

    pos = pos0 + lax.broadcasted_iota(jnp.int32, (tm, 1), 0)
    gw = cw // len(POOL_WINDOWS)
    parts = []
    for g, w in enumerate(POOL_WINDOWS):
        zg = z[:, g * gw:(g + 1) * gw]
        s = zg
        sh = 1
        while sh < w:
            s = s + pltpu.roll(s, sh, axis=0)
            sh *= 2
        cnt = jnp.minimum(pos + 1, w).astype(F32)
        pooled = s[POOL_HALO:] / cnt - zg[POOL_HALO:]
        parts.append(jnp.dot(pooled.astype(BF16), wpool_ref[g], preferred_element_type=F32))
    yd = jnp.concatenate(parts, axis=1) * ps_ref[...]

    mix = (jnp.dot(yc.astype(BF16), wout_ref[0:cw, :], preferred_element_type=F32)
           + jnp.dot(yd.astype(BF16), wout_ref[cw:2 * cw, :], preferred_element_type=F32))
    x2 = _xattn_tail(x + mix, gx_ref, wq_ref, gq_ref, kx_ref, vx_ref, wo_ref)
    _store_with_mlp_norm(x2, gm_ref, o_ref, hn_ref)


def _mix1_call(x2, gmix, win, cw, wpool, ps, wout, gx, wq, gq, kx, vx, wo, gm, *, seq, tm):
    t, d = x2.shape
    row = pl.BlockSpec((tm, d), lambda i: (i, 0))
    halo = pl.BlockSpec((POOL_HALO, d), lambda i: (jnp.maximum(i * (tm // POOL_HALO) - 1, 0), 0))
    mem_spec = pl.BlockSpec((1,) + kx.shape[1:], lambda i: ((i * tm) // seq, 0, 0))
    fulls = [gmix, win, cw, wpool, ps, wout, gx, wq, gq]
    return pl.pallas_call(
        functools.partial(_mix1_body, tm=tm, seq=seq),
        grid=(t // tm,),
        in_specs=[row, halo] + [_full_spec(a) for a in fulls]
                 + [mem_spec, mem_spec, _full_spec(wo), _full_spec(gm)],
        out_specs=[row, row],
        out_shape=[jax.ShapeDtypeStruct((t, d), F32), jax.ShapeDtypeStruct((t, d), BF16)],
        compiler_params=_cparams("parallel"),
        name="mix1",
    )(x2, x2, gmix, win, cw, wpool, ps, wout, gx, wq, gq, kx, vx, wo, gm)


def _row(v, reps=1):
    return jnp.tile(v.astype(F32), reps).reshape(1, -1)


def kernel(x, mem, positions, g_mix, g_xattn, g_mem, g_mlp, wq_x, wkv_x, gq_x, gk_x, wo_x, w_up, w_down,
           w_in_e, gq_a, gk_a, lam_q1, lam_k1, lam_q2, lam_k2, g_sub_a, g_kv_b, w_kv_up_b, gq_b, gk_b,
           g_kidx, w_out_e, w_in_o, conv_w, w_pool, pool_scale, w_out_o):
    batch, seq, d = x.shape
    t = batch * seq
    k_sel = min(TOPK_MAX, seq // 4)
    tm = min(1024, seq)
    tq_b, tk_b = min(256, seq), min(512, seq)
    assert seq % tm == 0 and seq % tq_b == 0 and tm % tk_b == 0 and tk_b >= k_sel

    x2 = x.reshape(t, d)
    pos = positions.reshape(t // tm, 1, tm)
    inv_freq = (ROPE_THETA ** (-jnp.arange(0, ROT_DIM, 2, dtype=F32) / ROT_DIM)).reshape(ROT_HALF, 1)
    mem2 = mem.reshape(batch * mem.shape[1], d)
    depth = g_mix.shape[0]
    w_up_b, w_down_b = w_up.astype(BF16), w_down.astype(BF16)

    for layer in range(depth):
        j = layer // 2
        kx, vx = _memkv_call(mem2, _row(g_mem[layer]), wkv_x[layer].astype(BF16), _row(gk_x[layer]))
        kx = kx.reshape(batch, mem.shape[1], -1)
        vx = vx.reshape(batch, mem.shape[1], -1)
        tail = (_row(g_xattn[layer]), wq_x[layer].astype(BF16), _row(gq_x[layer]) * (X_HEAD_DIM ** -0.5),
                kx, vx, wo_x[layer].astype(BF16), _row(g_mlp[layer]))
        if layer % 2 == 0:
            lam_init = 0.8 - 0.6 * float(np.exp(-0.3 * layer))
            lam = (jnp.exp(jnp.sum(lam_q1[j].astype(F32) * lam_k1[j].astype(F32)))
                   - jnp.exp(jnp.sum(lam_q2[j].astype(F32) * lam_k2[j].astype(F32))) + lam_init)
            w_in = w_in_e[j]
            n_main = 2304
            n_idx = w_in.shape[1] - n_main
            widx = jnp.pad(w_in[:, n_main:], ((0, 0), (0, 3 * LANES - n_idx)))
            if IDX_PRECISION is None:
                widx = widx.astype(BF16)
            heads8 = 512 // HEAD_DIM
            bd = jnp.kron(jnp.eye(heads8, dtype=F32), jnp.ones((HEAD_DIM, HEAD_DIM), F32)).astype(BF16)
            gkidx = jnp.pad(g_kidx[j].astype(F32), (0, LANES - IDX_DIM)).reshape(1, LANES)
            qa, ka, vat, qb, kb, vbt, qi, kik, wi = _in0_call(
                x2, _row(g_mix[layer]), w_in[:, :n_main].astype(BF16), widx, w_kv_up_b[j].astype(BF16),
                pos, inv_freq, bd, _row(gq_a[j], heads8), _row(gk_a[j], heads8), _row(gq_b[j], heads8),
                _row(gk_b[j], heads8), _row(g_kv_b[j]), gkidx, tm=tm, tk=tk_b, idx_precision=IDX_PRECISION)
            out_a = _diff_attn_call(lam.reshape(1).astype(F32), qa, ka, vat, _row(g_sub_a[j]),
                                    batch=batch, seq=seq, tq=tq_b, tk=tk_b, group=qa.shape[1] // LANES, out_scale=1.0 - lam_init)
            tri = jnp.tril(jnp.ones((LANES, LANES), F32)).astype(BF16)
            out_b = _dsa_call(qi, wi, qb, kik, kb, vbt, tri, batch=batch, seq=seq, tq=tq_b, tk=tk_b,
                              k_sel=k_sel, idx_precision=IDX_PRECISION)
            w_out = w_out_e[j].astype(BF16)
            x2, hn = _out0_call(x2, out_a, out_b, w_out[:512], w_out[512:], *tail, seq=seq, tm=tm)
        else:
            taps = jnp.pad(conv_w[j].astype(F32), ((0, SUBLANES - CONV_W), (0, 0)))
            x2, hn = _mix1_call(x2, _row(g_mix[layer]), w_in_o[j].astype(BF16), taps, w_pool[j].astype(BF16),
                                _row(pool_scale[j]), w_out_o[j].astype(BF16), *tail, seq=seq, tm=tm)
        x2 = _mlp_call(x2, hn, w_up_b, w_down_b, layer, tm=min(1024, t), tf=2048)
    return x2.reshape(batch, seq, d)
```

```python
import functools

import numpy as np
import jax
import jax.numpy as jnp
from jax import lax
from jax.experimental import pallas as pl
from jax.experimental.pallas import tpu as pltpu

F32 = jnp.float32
BF16 = jnp.bfloat16

HEAD_DIM = 64
ROT_DIM = HEAD_DIM // 4
ROT_HALF = ROT_DIM // 2
ROPE_THETA = 500000.0
IDX_HEADS = 4
IDX_DIM = 64
TOPK_MAX = 256
CONV_W = 3
POOL_WINDOWS = (2, 4, 8, 16)
POOL_HALO = 16
X_HEADS = 4
X_HEAD_DIM = 128
EPS = 1e-6
LOG2_E = 1.4426950408889634

LANES = 128
SUBLANES = 8
VMEM_LIMIT_BYTES = 56 * 1024 * 1024

KEY_NEG_INF = np.int32(-2139095041)
INT32_MIN = np.int32(-(2 ** 31))

NT_DIMS = (((1,), (1,)), ((), ()))


def _cparams(*sem):
    return pltpu.CompilerParams(dimension_semantics=sem, vmem_limit_bytes=VMEM_LIMIT_BYTES)


def _full_spec(arr):
    nd = arr.ndim
    return pl.BlockSpec(arr.shape, lambda *_: (0,) * nd)


def _rms(x, g):
    ms = jnp.mean(x * x, axis=-1, keepdims=True)
    return x * lax.rsqrt(ms + EPS) * g


def _tile_lanes(t, width):
    reps = width // t.shape[-1]
    return t if reps == 1 else jnp.concatenate([t] * reps, axis=1)


def _rope_tables(pos_row, inv_col):
    ang = inv_col * pos_row
    cos, sin = jnp.cos(ang), jnp.sin(ang)
    rest = (HEAD_DIM - ROT_DIM, ang.shape[1])
    head_c = jnp.concatenate([cos, cos, jnp.ones(rest, F32)], axis=0)
    head_s = jnp.concatenate([-sin, sin, jnp.zeros(rest, F32)], axis=0)
    return (jnp.concatenate([head_c, head_c], axis=0).T, jnp.concatenate([head_s, head_s], axis=0).T)


def _rope(y, c, s):
    w = y.shape[-1]
    c, s = _tile_lanes(c, w), _tile_lanes(s, w)
    lane = lax.broadcasted_iota(jnp.int32, (1, w), 1)
    upper = pltpu.roll(y, w - ROT_HALF, axis=1)
    lower = pltpu.roll(y, ROT_HALF, axis=1)
    partner = jnp.where((lane & (HEAD_DIM - 1)) < ROT_HALF, upper, lower)
    return y * c + partner * s


def _headnorm64(y, blockdiag, g):
    ss = jnp.dot((y * y).astype(BF16), blockdiag, preferred_element_type=F32)
    return y * lax.rsqrt(ss * (1.0 / HEAD_DIM) + EPS) * g


def _in0_body(x_ref, gmix_ref, wmain_ref, widx_ref, wkvup_ref, pos_ref, inv_ref,
              bd_ref, gqa_ref, gka_ref, gqb_ref, gkb_ref, gkv_ref, gkidx_ref,
              qa_ref, ka_ref, vat_ref, qb_ref, kb_ref, vbt_ref, qi_ref, kik_ref, wi_ref):
    hb = _rms(x_ref[...], gmix_ref[...]).astype(BF16)
    proj = lambda lo, hi: jnp.dot(hb, wmain_ref[:, lo:hi], preferred_element_type=F32)
    rot = _rope_tables(pos_ref[0].astype(F32), inv_ref[...])
    bd = bd_ref[...]
    scale = HEAD_DIM ** -0.5 * LOG2_E

    qa = _rope(_headnorm64(proj(0, 512), bd, gqa_ref[...]), *rot) * scale
    qa_ref[...] = qa.astype(BF16)
    ka = _rope(_headnorm64(proj(512, 1024), bd, gka_ref[...]), *rot)
    ka_ref[...] = ka.astype(BF16)
    tk = vat_ref.shape[2]
    va = proj(1024, 1536)
    for r in range(vat_ref.shape[0]):
        vat_ref[r] = va[r * tk:(r + 1) * tk, :].T.astype(BF16)
    qb = _rope(_headnorm64(proj(1536, 2048), bd, gqb_ref[...]), *rot) * scale
    qb_ref[...] = qb.astype(BF16)

    ckv = _rms(proj(2048, 2304), gkv_ref[...]).astype(BF16)
    kv = jnp.dot(ckv, wkvup_ref[...], preferred_element_type=F32)
    kb = _rope(_headnorm64(kv[:, 0:512], bd, gkb_ref[...]), *rot)
    kb_ref[...] = kb.astype(BF16)
    for r in range(vbt_ref.shape[0]):
        vbt_ref[r] = kv[r * tk:(r + 1) * tk, 512:1024].T.astype(BF16)

    yi = jnp.dot(hb, widx_ref[...], preferred_element_type=F32)
    qi_ref[...] = _rope(yi[:, 0:256], *rot).astype(qi_ref.dtype)
    blk = yi[:, 256:384]
    lane = lax.broadcasted_iota(jnp.int32, blk.shape, 1)
    ms = jnp.sum(jnp.where(lane < IDX_DIM, blk * blk, 0.0), axis=-1, keepdims=True) * (1.0 / IDX_DIM)
    kin = blk * lax.rsqrt(ms + EPS) * gkidx_ref[...]
    kir = _rope(kin, *rot)
    kik = kir + pltpu.roll(kir, IDX_DIM, axis=1)
    kik_ref[...] = kik.astype(kik_ref.dtype)
    wi_ref[...] = blk * (IDX_HEADS ** -0.5 * IDX_DIM ** -0.5)


def _in0_call(x2, gmix, wmain, widx, wkvup, pos, inv, bd, gqa, gka, gqb, gkb, gkv, gkidx,
              *, tm, tk):
    t, d = x2.shape
    row = lambda w: pl.BlockSpec((tm, w), lambda i: (i, 0))
    fulls = [gmix, wmain, widx, wkvup]
    gains = [inv, bd, gqa, gka, gqb, gkb, gkv, gkidx]
    in_specs = ([row(d)] + [_full_spec(a) for a in fulls] + [pl.BlockSpec((1, 1, tm), lambda i: (i, 0, 0))]
                + [_full_spec(a) for a in gains])
    sds = jax.ShapeDtypeStruct
    vt_shape, vt_spec = sds((t // tk, 512, tk), BF16), pl.BlockSpec((tm // tk, 512, tk), lambda i: (i, 0, 0))
    out_shape = ([sds((t, 512), BF16)] * 2 + [vt_shape] + [sds((t, 512), BF16)] * 2
                 + [vt_shape, sds((t, 256), BF16), sds((t, LANES), BF16), sds((t, LANES), F32)])
    out_specs = [row(512)] * 2 + [vt_spec] + [row(512)] * 2 + [vt_spec, row(256), row(LANES), row(LANES)]
    return pl.pallas_call(
        _in0_body,
        grid=(t // tm,),
        in_specs=in_specs,
        out_specs=out_specs,
        out_shape=out_shape,
        compiler_params=_cparams("parallel"),
        name="in0",
    )(x2, gmix, wmain, widx, wkvup, pos, inv, bd, gqa, gka, gqb, gkb, gkv, gkidx)


def _diff_attn_body(lam_ref, q_ref, k_ref, vt_ref, gsub_ref, o_ref, s_ref, pe_ref, *, tq, tk, out_scale):
    i = pl.program_id(2)
    q0 = i * tq
    n_full = (q0 + 1) // tk
    nkc = (q0 + tq + tk - 1) // tk
    lane = lax.broadcasted_iota(jnp.int32, (1, LANES), 1)
    lane_lo = lane < HEAD_DIM
    qpos = q0 + lax.broadcasted_iota(jnp.int32, (1, tq), 1)
    qpos2 = jnp.concatenate([qpos, qpos], axis=1)
    neg_inf = jnp.float32(-jnp.inf)
    heads = range(q_ref.shape[1] // LANES)
    q_cat = []
    for g in heads:
        q_h = q_ref[:, g * LANES:(g + 1) * LANES]
        q_cat.append(jnp.concatenate([q_h * lane_lo.astype(BF16), q_h * (~lane_lo).astype(BF16)], axis=0))

    def logit_step(g, c, m, masked):
        s = lax.dot_general(k_ref[pl.ds(pl.multiple_of(c * tk, tk), tk), g * LANES:(g + 1) * LANES],
                            q_cat[g], NT_DIMS, preferred_element_type=F32)
        if masked:
            kpos = c * tk + lax.broadcasted_iota(jnp.int32, (tk, 1), 0)
            s = jnp.where(kpos <= qpos2, s, neg_inf)
        s_ref[g % s_ref.shape[0], c] = s
        return jnp.maximum(m, jnp.max(s.reshape(tk // SUBLANES, SUBLANES, 2 * tq), axis=0))

    def pv_matmul(g, c):
        return jnp.dot(vt_ref[c, g * LANES:(g + 1) * LANES, :], pe_ref[g % pe_ref.shape[0]],
                       preferred_element_type=F32)

    def exp_step(g, c, shift, l8):
        pe = jnp.exp2(s_ref[g % s_ref.shape[0], c] - shift)
        pe_ref[g % pe_ref.shape[0]] = pe.astype(BF16)
        return l8 + jnp.sum(pe.reshape(tk // SUBLANES, SUBLANES, 2 * tq), axis=0)

    def pv_step(g, c, shift, l8, acc):
        pe = jnp.exp2(s_ref[g % s_ref.shape[0], c] - shift)
        acc = acc + jnp.dot(vt_ref[c, g * LANES:(g + 1) * LANES, :], pe.astype(BF16),
                            preferred_element_type=F32)
        return l8 + jnp.sum(pe.reshape(tk // SUBLANES, SUBLANES, 2 * tq), axis=0), acc

    def run(body, init):
        carry = _loop_by_twos(0, n_full, functools.partial(body, masked=False), init)
        return lax.fori_loop(n_full, nkc, functools.partial(body, masked=True), carry)

    lam = lam_ref[0]
    m_init = jnp.full((SUBLANES, 2 * tq), neg_inf, F32)
    pv_init = (jnp.zeros((SUBLANES, 2 * tq), F32), jnp.zeros((LANES, 2 * tq), F32))

    def finish(g, l8, acc):
        o_all = acc / jnp.sum(l8, axis=0, keepdims=True)
        o = o_all[:, 0:tq] - lam * o_all[:, tq:2 * tq]
        o = o * lax.rsqrt(jnp.mean(o * o, axis=0, keepdims=True) + EPS)
        o_ref[:, g * LANES:(g + 1) * LANES] = (o.T * gsub_ref[...] * out_scale).astype(o_ref.dtype)

    width = s_ref.shape[0] // 2
    stages = [heads[lo:lo + width] for lo in range(0, len(heads), width)]

    def logit_stage(stage, c, ms, masked):
        return tuple(logit_step(g, c, m, masked) for g, m in zip(stage, ms))

    def pv_stage(stage, c, shifts, carry):
        return tuple(pv_step(g, c, sh, *lc) for g, sh, lc in zip(stage, shifts, carry))

    ms = run(lambda c, ms, masked: logit_stage(stages[0], c, ms, masked), (m_init,) * width)
    for prev, cur in zip(stages[:-1], stages[1:]):
        shifts = [jnp.max(m, axis=0, keepdims=True) for m in ms]

        def fused(c, carry, masked, prev=prev, cur=cur, shifts=shifts):
            pv, ms = carry
            return pv_stage(prev, c, shifts, pv), logit_stage(cur, c, ms, masked)

        pv, ms = run(fused, ((pv_init,) * width, (m_init,) * width))
        for g, lc in zip(prev, pv):
            finish(g, *lc)
    shifts = [jnp.max(m, axis=0, keepdims=True) for m in ms]

    last = stages[-1]

    def last_chunk(c, carry):
        l8s, accs = carry
        accs = tuple(acc + pv_matmul(g, c - 1) for g, acc in zip(last, accs))
        return tuple(exp_step(g, c, sh, l8) for g, sh, l8 in zip(last, shifts, l8s)), accs

    l8s = tuple(exp_step(g, 0, sh, pv_init[0]) for g, sh in zip(last, shifts))
    l8s, accs = lax.fori_loop(1, nkc, last_chunk, (l8s, (pv_init[1],) * width))
    for g, l8, acc in zip(last, l8s, accs):
        finish(g, l8, acc + pv_matmul(g, nkc - 1))


def _diff_attn_call(lam, qa, ka, vat, gsub, *, batch, seq, tq, tk, group, out_scale):
    t = qa.shape[0]
    w = group * LANES
    nq, nk = seq // tq, seq // tk
    q_spec = pl.BlockSpec((tq, w), lambda b, h, i: (b * nq + i, h))
    return pl.pallas_call(
        functools.partial(_diff_attn_body, tq=tq, tk=tk, out_scale=out_scale),
        grid=(batch, qa.shape[1] // w, nq),
        in_specs=[pl.BlockSpec(memory_space=pltpu.SMEM), q_spec,
                  pl.BlockSpec((seq, w), lambda b, h, i: (b, h)),
                  pl.BlockSpec((nk, w, tk), lambda b, h, i: (b, h, 0)), _full_spec(gsub)],
        out_specs=q_spec,
        out_shape=jax.ShapeDtypeStruct((t, qa.shape[1]), BF16),
        scratch_shapes=[pltpu.VMEM((group, nk, tk, 2 * tq), F32), pltpu.VMEM((group // 2, tk, 2 * tq), BF16)],
        compiler_params=_cparams("parallel", "parallel", "arbitrary"),
        name="diff_attn",
    )(lam, qa, ka, vat, gsub)


def _loop_by_twos(lo, hi, step, carry):
    pairs = (hi - lo) // 2
    carry = lax.fori_loop(0, pairs, lambda i, cr: step(lo + 2 * i + 1, step(lo + 2 * i, cr)), carry)
    return lax.fori_loop(lo + 2 * pairs, hi, step, carry)


def _key_to_float(key):
    bits = key ^ ((key >> 31) & jnp.int32(0x7FFFFFFF))
    return lax.bitcast_convert_type(bits, F32)


def _dsa_body(qi_ref, wi_ref, qb_ref, kik_ref, kb_ref, vbt_ref, tri_ref, o_ref, sc_ref, bias_ref, s_ref, pe_ref,
              *, tq, tk, k_sel):
    i = pl.program_id(1)
    q0 = i * tq
    nkc = (q0 + tq + tk - 1) // tk
    qpos = q0 + lax.broadcasted_iota(jnp.int32, (1, tq), 1)
    lane = lax.broadcasted_iota(jnp.int32, (1, LANES), 1)
    lane_lo = lane < HEAD_DIM
    tiles = tk // LANES
    neg_inf = jnp.float32(-jnp.inf)

    wi_t = wi_ref[...].T
    qi = qi_ref[...]
    q_heads = []
    for h in range(IDX_HEADS):
        pair = qi[:, (h // 2) * LANES:(h // 2 + 1) * LANES]
        q_heads.append(pair * (lane_lo if h % 2 == 0 else ~lane_lo).astype(pair.dtype))
    qi_cat = jnp.concatenate(q_heads, axis=0)
    w_heads = [wi_t[IDX_DIM + h:IDX_DIM + h + 1, :] for h in range(IDX_HEADS)]

    def score_chunk(c, carry):
        kk = kik_ref[pl.ds(pl.multiple_of(c * tk, tk), tk), :]
        s = lax.dot_general(kk, qi_cat, NT_DIMS, preferred_element_type=F32)
        acc = jnp.zeros((tk, tq), F32)
        for h in range(IDX_HEADS):
            acc = acc + jnp.maximum(s[:, h * tq:(h + 1) * tq], 0.0) * w_heads[h]
        kpos = c * tk + lax.broadcasted_iota(jnp.int32, (tk, 1), 0)
        sc_ref[c] = jnp.where(kpos <= qpos, acc, neg_inf)
        return carry

    _loop_by_twos(0, nkc, score_chunk, 0)

    acc_rows = 4 * SUBLANES

    def count_ge(cand):
        def body(c, acc):
            for r in range(tk // acc_rows):
                acc = acc + jnp.where(sc_ref[c, r * acc_rows:(r + 1) * acc_rows, :] >= cand, 1.0, 0.0)
            return acc
        acc = lax.fori_loop(0, nkc, body, jnp.zeros((acc_rows, tq), F32))
        return jnp.sum(acc, axis=0, keepdims=True)

    ksel = float(k_sel)

    def count_at(cand_key):
        return jnp.where(cand_key <= KEY_NEG_INF, ksel, count_ge(_key_to_float(cand_key)))

    zero_key = jnp.zeros((1, tq), jnp.int32)
    cnt = count_at(zero_key)
    thr = jnp.where(cnt >= ksel, zero_key, INT32_MIN)
    n_above = jnp.where(cnt >= ksel, 0.0, cnt)

    def bit_step(b, carry):
        thr, n_above = carry
        cand = thr + jnp.left_shift(jnp.int32(1), 30 - b)
        cnt = count_at(cand)
        return jnp.where(cnt >= ksel, cand, thr), jnp.where(cnt >= ksel, n_above, cnt)

    thr, n_above = lax.fori_loop(0, 31, bit_step, (thr, n_above))
    thr_f = _key_to_float(thr)
    above_f = _key_to_float(thr + 1)
    need = ksel - n_above
    tri = tri_ref[...]

    def select_chunk(c, taken):
        for t in range(tiles):
            st = sc_ref[c, t * LANES:(t + 1) * LANES, :]
            is_above = st >= above_f
            tied = jnp.where(is_above, 0.0, jnp.where(st >= thr_f, 1.0, 0.0))
            prefix = jnp.dot(tri, tied.astype(BF16), preferred_element_type=F32)
            tie_sel = jnp.where((taken + prefix) <= need, tied, 0.0)
            sel = jnp.where(is_above, 0.0, jnp.where(tie_sel > 0.0, 0.0, neg_inf))
            kpos = c * tk + t * LANES + lax.broadcasted_iota(jnp.int32, (LANES, 1), 0)
            bias_ref[c, t * LANES:(t + 1) * LANES, :] = jnp.where(kpos <= qpos, sel, neg_inf)
            taken = taken + prefix[LANES - 1:LANES, :]
        return taken

    _loop_by_twos(0, nkc, select_chunk, jnp.zeros((1, tq), F32))

    group = s_ref.shape[0]
    for p0 in range(0, qb_ref.shape[1] // LANES, group):
        pairs = range(p0, p0 + group)
        q_cat = []
        for p in pairs:
            q_pair = qb_ref[:, p * LANES:(p + 1) * LANES]
            q_cat.append(jnp.concatenate([q_pair * lane_lo.astype(BF16), q_pair * (~lane_lo).astype(BF16)],
                                         axis=0))

        def logit_chunk(c, carry, pairs=pairs, q_cat=q_cat):
            bias = bias_ref[c]
            bias2 = jnp.concatenate([bias, bias], axis=1)
            out = []
            for g, (p, m) in enumerate(zip(pairs, carry)):
                kc = kb_ref[pl.ds(pl.multiple_of(c * tk, tk), tk), p * LANES:(p + 1) * LANES]
                s = lax.dot_general(kc, q_cat[g], NT_DIMS, preferred_element_type=F32) + bias2
                s_ref[g, c] = s
                out.append(jnp.maximum(m, jnp.max(s.reshape(tk // SUBLANES, SUBLANES, 2 * tq), axis=0)))
            return tuple(out)

        m8 = _loop_by_twos(0, nkc, logit_chunk,
                           tuple(jnp.full((SUBLANES, 2 * tq), neg_inf, F32) for _ in pairs))
        shift = [jnp.max(m, axis=0, keepdims=True) for m in m8]

        def exp_stage(c, l8s, shift=shift):
            out = []
            for g, l8 in enumerate(l8s):
                pe = jnp.exp2(s_ref[g, c] - shift[g])
                pe_ref[g] = pe.astype(BF16)
                out.append(l8 + jnp.sum(pe.reshape(tk // SUBLANES, SUBLANES, 2 * tq), axis=0))
            return tuple(out)

        def pv_stage(c, accs, pairs=pairs):
            return tuple(acc + jnp.dot(vbt_ref[c, p * LANES:(p + 1) * LANES, :], pe_ref[g],
                                       preferred_element_type=F32)
                         for g, (p, acc) in enumerate(zip(pairs, accs)))

        def pv_chunk(c, carry):
            l8s, accs = carry
            accs = pv_stage(c - 1, accs)
            return exp_stage(c, l8s), accs

        l8s = exp_stage(0, tuple(jnp.zeros((SUBLANES, 2 * tq), F32) for _ in pairs))
        l8s, accs = lax.fori_loop(1, nkc, pv_chunk,
                                  (l8s, tuple(jnp.zeros((LANES, 2 * tq), F32) for _ in pairs)))
        accs = pv_stage(nkc - 1, accs)
        for p, l8, acc in zip(pairs, l8s, accs):
            o_all = acc / jnp.sum(l8, axis=0, keepdims=True)
            o_t = jnp.concatenate([o_all[0:HEAD_DIM, 0:tq], o_all[HEAD_DIM:LANES, tq:2 * tq]], axis=0)
            o_ref[:, p * LANES:(p + 1) * LANES] = o_t.T.astype(o_ref.dtype)


def _dsa_call(qi, wi, qb, kik, kb, vbt, tri, *, batch, seq, tq, tk, k_sel):
    t = qb.shape[0]
    nq, nk = seq // tq, seq // tk
    row = lambda w: pl.BlockSpec((tq, w), lambda b, i: (b * nq + i, 0))
    seqb = lambda w: pl.BlockSpec((seq, w), lambda b, i: (b, 0))
    vbt_spec = pl.BlockSpec((nk,) + vbt.shape[1:], lambda b, i: (b, 0, 0))
    return pl.pallas_call(
        functools.partial(_dsa_body, tq=tq, tk=tk, k_sel=k_sel),
        grid=(batch, nq),
        in_specs=[row(qi.shape[1]), row(LANES), row(qb.shape[1]), seqb(LANES), seqb(kb.shape[1]),
                  vbt_spec, _full_spec(tri)],
        out_specs=row(qb.shape[1]),
        out_shape=jax.ShapeDtypeStruct((t, qb.shape[1]), BF16),
        scratch_shapes=[pltpu.VMEM((nk, tk, tq), F32), pltpu.VMEM((nk, tk, tq), F32),
                        pltpu.VMEM((2, nk, tk, 2 * tq), F32), pltpu.VMEM((2, tk, 2 * tq), BF16)],
        compiler_params=_cparams("parallel", "arbitrary"),
        name="dsa",
    )(qi, wi, qb, kik, kb, vbt, tri)


def _memkv_body(mem_ref, g_ref, w_ref, gk_ref, k_ref, v_ref):
    hm = _rms(mem_ref[...], g_ref[...]).astype(BF16)
    kv = jnp.dot(hm, w_ref[...], preferred_element_type=F32)
    half = kv.shape[1] // 2
    for h in range(X_HEADS):
        kh = kv[:, h * X_HEAD_DIM:(h + 1) * X_HEAD_DIM]
        k_ref[:, h * X_HEAD_DIM:(h + 1) * X_HEAD_DIM] = _rms(kh, gk_ref[...]).astype(BF16)
    v_ref[...] = kv[:, half:].astype(BF16)


def _memkv_call(mem2, g, w, gk):
    rows, d = mem2.shape
    half = w.shape[1] // 2
    tm = min(rows, 512)
    row = lambda wd: pl.BlockSpec((tm, wd), lambda i: (i, 0))
    return pl.pallas_call(
        _memkv_body,
        grid=(rows // tm,),
        in_specs=[row(d), _full_spec(g), _full_spec(w), _full_spec(gk)],
        out_specs=[row(half), row(half)],
        out_shape=[jax.ShapeDtypeStruct((rows, half), BF16)] * 2,
        compiler_params=_cparams("parallel"),
        name="memkv",
    )(mem2, g, w, gk)


def _xattn_tail(x1, gx_ref, wq_ref, gq_ref, kx_ref, vx_ref, wo_ref):
    hx = _rms(x1, gx_ref[...]).astype(BF16)
    q = jnp.dot(hx, wq_ref[...], preferred_element_type=F32)
    outs = []
    for h in range(X_HEADS):
        cols = slice(h * X_HEAD_DIM, (h + 1) * X_HEAD_DIM)
        qh = _rms(q[:, cols], gq_ref[...]).astype(BF16)
        s = lax.dot_general(qh, kx_ref[0, :, cols], NT_DIMS, preferred_element_type=F32)
        p = jnp.exp(s - jnp.max(s, axis=-1, keepdims=True))
        o = (jnp.dot(p.astype(BF16), vx_ref[0, :, cols], preferred_element_type=F32)
             / jnp.sum(p, axis=-1, keepdims=True))
        outs.append(o.astype(BF16))
    o = jnp.concatenate(outs, axis=1)
    return x1 + jnp.dot(o, wo_ref[...], preferred_element_type=F32)


def _store_with_mlp_norm(x2, gm_ref, o_ref, hn_ref):
    o_ref[...] = x2
    hn_ref[...] = _rms(x2, gm_ref[...]).astype(BF16)


def _out0_body(x_ref, oa_ref, ob_ref, woa_ref, wob_ref, gx_ref, wq_ref, gq_ref, kx_ref, vx_ref, wo_ref,
               gm_ref, o_ref, hn_ref):
    mix = (jnp.dot(oa_ref[...], woa_ref[...], preferred_element_type=F32)
           + jnp.dot(ob_ref[...], wob_ref[...], preferred_element_type=F32))
    x2 = _xattn_tail(x_ref[...] + mix, gx_ref, wq_ref, gq_ref, kx_ref, vx_ref, wo_ref)
    _store_with_mlp_norm(x2, gm_ref, o_ref, hn_ref)


def _out0_call(x2, oa, ob, woa, wob, gx, wq, gq, kx, vx, wo, gm, *, seq, tm):
    t, d = x2.shape
    row = lambda w: pl.BlockSpec((tm, w), lambda i: (i, 0))
    mem_spec = pl.BlockSpec((1,) + kx.shape[1:], lambda i: ((i * tm) // seq, 0, 0))
    fulls = [woa, wob, gx, wq, gq]
    return pl.pallas_call(
        _out0_body,
        grid=(t // tm,),
        in_specs=[row(d), row(oa.shape[1]), row(ob.shape[1])] + [_full_spec(a) for a in fulls]
                 + [mem_spec, mem_spec, _full_spec(wo), _full_spec(gm)],
        out_specs=[row(d), row(d)],
        out_shape=[jax.ShapeDtypeStruct((t, d), F32), jax.ShapeDtypeStruct((t, d), BF16)],
        compiler_params=_cparams("parallel"),
        name="out0",
    )(x2, oa, ob, woa, wob, gx, wq, gq, kx, vx, wo, gm)


def _mlp_body(x_ref, hn_ref, wup_ref, wdn_ref, o_ref, acc_ref):
    j = pl.program_id(1)
    u = jnp.dot(hn_ref[...], wup_ref[...], preferred_element_type=F32)
    a = jnp.square(jnp.maximum(u, 0.0)).astype(BF16)
    prev = jnp.where(j == 0, 0.0, acc_ref[...])
    acc_ref[...] = prev + jnp.dot(a, wdn_ref[...], preferred_element_type=F32)

    @pl.when(j == pl.num_programs(1) - 1)
    def _():
        o_ref[...] = x_ref[...] + acc_ref[...]


def _mlp_call(x2, hn, wup, wdn, layer, *, tm, tf):
    t, d = x2.shape
    ff = wup.shape[2]
    row = pl.BlockSpec((tm, d), lambda i, j: (i, 0))
    return pl.pallas_call(
        _mlp_body,
        grid=(t // tm, ff // tf),
        in_specs=[row, row,
                  pl.BlockSpec((None, d, tf), lambda i, j: (layer, 0, j)),
                  pl.BlockSpec((None, tf, d), lambda i, j: (layer, j, 0))],
        out_specs=row,
        out_shape=jax.ShapeDtypeStruct((t, d), F32),
        scratch_shapes=[pltpu.VMEM((tm, d), F32)],
        compiler_params=_cparams("parallel", "arbitrary"),
        name="mlp",
    )(x2, hn, wup, wdn)


def _mix1_body(x_ref, xh_ref, gmix_ref, win_ref, cw_ref, wpool_ref, ps_ref, wout_ref,
               gx_ref, wq_ref, gq_ref, kx_ref, vx_ref, wo_ref, gm_ref, o_ref, hn_ref, *, tm, seq):
    i = pl.program_id(0)
    x = x_ref[...]
    xf = jnp.concatenate([xh_ref[...], x], axis=0)
    hf = _rms(xf, gmix_ref[...]).astype(BF16)
    cw = win_ref.shape[1] // 4
    proj = lambda n: jnp.dot(hf, win_ref[:, n * cw:(n + 1) * cw], preferred_element_type=F32)
    pos0 = (i * tm) % seq
    rid = lax.broadcasted_iota(jnp.int32, (tm + POOL_HALO, 1), 0)
    keep = jnp.where(rid >= POOL_HALO, 1.0, jnp.where(pos0 > 0, 1.0, 0.0))

    u = proj(1) * proj(2) * keep
    z = proj(3) * keep
    gb = proj(0)[POOL_HALO:]

    taps = cw_ref[...]
    conv = taps[CONV_W - 1:CONV_W, :] * u[POOL_HALO:]
    for back in range(1, CONV_W):
        conv = conv + taps[CONV_W - 1 - back:CONV_W - back, :] * pltpu.roll(u, back, axis=0)[POOL_HALO:]
    yc = gb * conv

    pos = pos0 + lax.broadcasted_iota(jnp.int32, (tm, 1), 0)
    gw = cw // len(POOL_WINDOWS)
    parts = []
    for g, w in enumerate(POOL_WINDOWS):
        zg = z[:, g * gw:(g + 1) * gw]
        s = zg
        sh = 1
        while sh < w:
            s = s + pltpu.roll(s, sh, axis=0)
            sh *= 2
        cnt = jnp.minimum(pos + 1, w).astype(F32)
        pooled = s[POOL_HALO:] / cnt - zg[POOL_HALO:]
        parts.append(jnp.dot(pooled.astype(BF16), wpool_ref[g], preferred_element_type=F32))
    yd = jnp.concatenate(parts, axis=1) * ps_ref[...]

    mix = (jnp.dot(yc.astype(BF16), wout_ref[0:cw, :], preferred_element_type=F32)
           + jnp.dot(yd.astype(BF16), wout_ref[cw:2 * cw, :], preferred_element_type=F32))
    x2 = _xattn_tail(x + mix, gx_ref, wq_ref, gq_ref, kx_ref, vx_ref, wo_ref)
    _store_with_mlp_norm(x2, gm_ref, o_ref, hn_ref)


def _mix1_call(x2, gmix, win, cw, wpool, ps, wout, gx, wq, gq, kx, vx, wo, gm, *, seq, tm):
    t, d = x2.shape
    row = pl.BlockSpec((tm, d), lambda i: (i, 0))
    halo = pl.BlockSpec((POOL_HALO, d), lambda i: (jnp.maximum(i * (tm // POOL_HALO) - 1, 0), 0))
    mem_spec = pl.BlockSpec((1,) + kx.shape[1:], lambda i: ((i * tm) // seq, 0, 0))
    fulls = [gmix, win, cw, wpool, ps, wout, gx, wq, gq]
    return pl.pallas_call(
        functools.partial(_mix1_body, tm=tm, seq=seq),
        grid=(t // tm,),
        in_specs=[row, halo] + [_full_spec(a) for a in fulls]
                 + [mem_spec, mem_spec, _full_spec(wo), _full_spec(gm)],
        out_specs=[row, row],
        out_shape=[jax.ShapeDtypeStruct((t, d), F32), jax.ShapeDtypeStruct((t, d), BF16)],
        compiler_params=_cparams("parallel"),
        name="mix1",
    )(x2, x2, gmix, win, cw, wpool, ps, wout, gx, wq, gq, kx, vx, wo, gm)


def _row(v, reps=1):
    return jnp.tile(v.astype(F32), reps).reshape(1, -1)


def kernel(x, mem, positions, g_mix, g_xattn, g_mem, g_mlp, wq_x, wkv_x, gq_x, gk_x, wo_x, w_up, w_down,
           w_in_e, gq_a, gk_a, lam_q1, lam_k1, lam_q2, lam_k2, g_sub_a, g_kv_b, w_kv_up_b, gq_b, gk_b,
           g_kidx, w_out_e, w_in_o, conv_w, w_pool, pool_scale, w_out_o):
    batch, seq, d = x.shape
    t = batch * seq
    k_sel = min(TOPK_MAX, seq // 4)
    tm = min(1024, seq)
    tq_b, tk_b = min(256, seq), min(512, seq)
    assert seq % tm == 0 and seq % tq_b == 0 and tm % tk_b == 0 and tk_b >= k_sel

    x2 = x.reshape(t, d)
    pos = positions.reshape(t // tm, 1, tm)
    inv_freq = (ROPE_THETA ** (-jnp.arange(0, ROT_DIM, 2, dtype=F32) / ROT_DIM)).reshape(ROT_HALF, 1)
    mem2 = mem.reshape(batch * mem.shape[1], d)
    depth = g_mix.shape[0]
    w_up_b, w_down_b = w_up.astype(BF16), w_down.astype(BF16)

    for layer in range(depth):
        j = layer // 2
        kx, vx = _memkv_call(mem2, _row(g_mem[layer]), wkv_x[layer].astype(BF16), _row(gk_x[layer]))
        kx = kx.reshape(batch, mem.shape[1], -1)
        vx = vx.reshape(batch, mem.shape[1], -1)
        tail = (_row(g_xattn[layer]), wq_x[layer].astype(BF16), _row(gq_x[layer]) * (X_HEAD_DIM ** -0.5),
                kx, vx, wo_x[layer].astype(BF16), _row(g_mlp[layer]))
        if layer % 2 == 0:
            lam_init = 0.8 - 0.6 * float(np.exp(-0.3 * layer))
            lam = (jnp.exp(jnp.sum(lam_q1[j].astype(F32) * lam_k1[j].astype(F32)))
                   - jnp.exp(jnp.sum(lam_q2[j].astype(F32) * lam_k2[j].astype(F32))) + lam_init)
            w_in = w_in_e[j]
            n_main = 2304
            n_idx = w_in.shape[1] - n_main
            widx = jnp.pad(w_in[:, n_main:], ((0, 0), (0, 3 * LANES - n_idx))).astype(BF16)
            heads8 = 512 // HEAD_DIM
            bd = jnp.kron(jnp.eye(heads8, dtype=F32), jnp.ones((HEAD_DIM, HEAD_DIM), F32)).astype(BF16)
            gkidx = jnp.pad(g_kidx[j].astype(F32), (0, LANES - IDX_DIM)).reshape(1, LANES)
            qa, ka, vat, qb, kb, vbt, qi, kik, wi = _in0_call(
                x2, _row(g_mix[layer]), w_in[:, :n_main].astype(BF16), widx, w_kv_up_b[j].astype(BF16),
                pos, inv_freq, bd, _row(gq_a[j], heads8), _row(gk_a[j], heads8), _row(gq_b[j], heads8),
                _row(gk_b[j], heads8), _row(g_kv_b[j]), gkidx, tm=tm, tk=tk_b)
            out_a = _diff_attn_call(lam.reshape(1).astype(F32), qa, ka, vat, _row(g_sub_a[j]),
                                    batch=batch, seq=seq, tq=tq_b, tk=tk_b, group=qa.shape[1] // LANES, out_scale=1.0 - lam_init)
            tri = jnp.tril(jnp.ones((LANES, LANES), F32)).astype(BF16)
            out_b = _dsa_call(qi, wi, qb, kik, kb, vbt, tri, batch=batch, seq=seq, tq=tq_b, tk=tk_b, k_sel=k_sel)
            w_out = w_out_e[j].astype(BF16)
            x2, hn = _out0_call(x2, out_a, out_b, w_out[:512], w_out[512:], *tail, seq=seq, tm=tm)
        else:
            taps = jnp.pad(conv_w[j].astype(F32), ((0, SUBLANES - CONV_W), (0, 0)))
            x2, hn = _mix1_call(x2, _row(g_mix[layer]), w_in_o[j].astype(BF16), taps, w_pool[j].astype(BF16),
                                _row(pool_scale[j]), w_out_o[j].astype(BF16), *tail, seq=seq, tm=tm)
        x2 = _mlp_call(x2, hn, w_up_b, w_down_b, layer, tm=min(1024, t), tf=2048)
    return x2.reshape(batch, seq, d)
```

```python
import functools

import numpy as np
import jax
import jax.numpy as jnp
from jax import lax
from jax.experimental import pallas as pl
from jax.experimental.pallas import tpu as pltpu

F32 = jnp.float32
BF16 = jnp.bfloat16

HEAD_DIM = 64
ROT_DIM = HEAD_DIM // 4
ROT_HALF = ROT_DIM // 2
ROPE_THETA = 500000.0
IDX_HEADS = 4
IDX_DIM = 64
TOPK_MAX = 256
CONV_W = 3
POOL_WINDOWS = (2, 4, 8, 16)
POOL_HALO = 16
X_HEADS = 4
X_HEAD_DIM = 128
EPS = 1e-6
LOG2_E = 1.4426950408889634

LANES = 128
SUBLANES = 8
VMEM_LIMIT_BYTES = 60 * 1024 * 1024

KEY_NEG_INF = np.int32(-2139095041)
INT32_MIN = np.int32(-(2 ** 31))

NT_DIMS = (((1,), (1,)), ((), ()))


def _cparams(*sem):
    return pltpu.CompilerParams(dimension_semantics=sem, vmem_limit_bytes=VMEM_LIMIT_BYTES)


def _full_spec(arr):
    nd = arr.ndim
    return pl.BlockSpec(arr.shape, lambda *_: (0,) * nd)


def _rms(x, g):
    ms = jnp.mean(x * x, axis=-1, keepdims=True)
    return x * lax.rsqrt(ms + EPS) * g


def _tile_lanes(t, width):
    reps = width // t.shape[-1]
    return t if reps == 1 else jnp.concatenate([t] * reps, axis=1)


def _rope_tables(pos_row, inv_col):
    ang = inv_col * pos_row
    cos, sin = jnp.cos(ang), jnp.sin(ang)
    rest = (HEAD_DIM - ROT_DIM, ang.shape[1])
    head_c = jnp.concatenate([cos, cos, jnp.ones(rest, F32)], axis=0)
    head_s = jnp.concatenate([-sin, sin, jnp.zeros(rest, F32)], axis=0)
    return (jnp.concatenate([head_c, head_c], axis=0).T, jnp.concatenate([head_s, head_s], axis=0).T)


def _rope(y, c, s):
    w = y.shape[-1]
    c, s = _tile_lanes(c, w), _tile_lanes(s, w)
    lane = lax.broadcasted_iota(jnp.int32, (1, w), 1)
    upper = pltpu.roll(y, w - ROT_HALF, axis=1)
    lower = pltpu.roll(y, ROT_HALF, axis=1)
    partner = jnp.where((lane & (HEAD_DIM - 1)) < ROT_HALF, upper, lower)
    return y * c + partner * s


def _headnorm64(y, blockdiag, g):
    ss = jnp.dot((y * y).astype(BF16), blockdiag, preferred_element_type=F32)
    return y * lax.rsqrt(ss * (1.0 / HEAD_DIM) + EPS) * g


def _in0_body(x_ref, gmix_ref, wmain_ref, widx_ref, wkvup_ref, pos_ref, inv_ref,
              bd_ref, gqa_ref, gka_ref, gqb_ref, gkb_ref, gkv_ref, gkidx_ref,
              qa_ref, ka_ref, vat_ref, qb_ref, kb_ref, vbt_ref, qi_ref, kik_ref, wi_ref):
    hb = _rms(x_ref[...], gmix_ref[...]).astype(BF16)
    proj = lambda lo, hi: jnp.dot(hb, wmain_ref[:, lo:hi], preferred_element_type=F32)
    rot = _rope_tables(pos_ref[0].astype(F32), inv_ref[...])
    bd = bd_ref[...]
    scale = HEAD_DIM ** -0.5 * LOG2_E

    qa = _rope(_headnorm64(proj(0, 512), bd, gqa_ref[...]), *rot) * scale
    qa_ref[...] = qa.astype(BF16)
    ka = _rope(_headnorm64(proj(512, 1024), bd, gka_ref[...]), *rot)
    ka_ref[...] = ka.astype(BF16)
    tk = vat_ref.shape[2]
    va = proj(1024, 1536)
    for r in range(vat_ref.shape[0]):
        vat_ref[r] = va[r * tk:(r + 1) * tk, :].T.astype(BF16)
    qb = _rope(_headnorm64(proj(1536, 2048), bd, gqb_ref[...]), *rot) * scale
    qb_ref[...] = qb.astype(BF16)

    ckv = _rms(proj(2048, 2304), gkv_ref[...]).astype(BF16)
    kv = jnp.dot(ckv, wkvup_ref[...], preferred_element_type=F32)
    kb = _rope(_headnorm64(kv[:, 0:512], bd, gkb_ref[...]), *rot)
    kb_ref[...] = kb.astype(BF16)
    for r in range(vbt_ref.shape[0]):
        vbt_ref[r] = kv[r * tk:(r + 1) * tk, 512:1024].T.astype(BF16)

    yi = jnp.dot(hb, widx_ref[...], preferred_element_type=F32)
    qi_ref[...] = _rope(yi[:, 0:256], *rot).astype(qi_ref.dtype)
    blk = yi[:, 256:384]
    lane = lax.broadcasted_iota(jnp.int32, blk.shape, 1)
    ms = jnp.sum(jnp.where(lane < IDX_DIM, blk * blk, 0.0), axis=-1, keepdims=True) * (1.0 / IDX_DIM)
    kin = blk * lax.rsqrt(ms + EPS) * gkidx_ref[...]
    kir = _rope(kin, *rot)
    kik = kir + pltpu.roll(kir, IDX_DIM, axis=1)
    kik_ref[...] = kik.astype(kik_ref.dtype)
    wi_ref[...] = blk * (IDX_HEADS ** -0.5 * IDX_DIM ** -0.5)


def _in0_call(x2, gmix, wmain, widx, wkvup, pos, inv, bd, gqa, gka, gqb, gkb, gkv, gkidx,
              *, tm, tk):
    t, d = x2.shape
    row = lambda w: pl.BlockSpec((tm, w), lambda i: (i, 0))
    fulls = [gmix, wmain, widx, wkvup]
    gains = [inv, bd, gqa, gka, gqb, gkb, gkv, gkidx]
    in_specs = ([row(d)] + [_full_spec(a) for a in fulls] + [pl.BlockSpec((1, 1, tm), lambda i: (i, 0, 0))]
                + [_full_spec(a) for a in gains])
    sds = jax.ShapeDtypeStruct
    vt_shape, vt_spec = sds((t // tk, 512, tk), BF16), pl.BlockSpec((tm // tk, 512, tk), lambda i: (i, 0, 0))
    out_shape = ([sds((t, 512), BF16)] * 2 + [vt_shape] + [sds((t, 512), BF16)] * 2
                 + [vt_shape, sds((t, 256), BF16), sds((t, LANES), BF16), sds((t, LANES), F32)])
    out_specs = [row(512)] * 2 + [vt_spec] + [row(512)] * 2 + [vt_spec, row(256), row(LANES), row(LANES)]
    return pl.pallas_call(
        _in0_body,
        grid=(t // tm,),
        in_specs=in_specs,
        out_specs=out_specs,
        out_shape=out_shape,
        compiler_params=_cparams("parallel"),
        name="in0",
    )(x2, gmix, wmain, widx, wkvup, pos, inv, bd, gqa, gka, gqb, gkb, gkv, gkidx)


def _diff_attn_body(lam_ref, q_ref, k_ref, vt_ref, gsub_ref, o_ref, s_ref, pe_ref, *, tq, tk, out_scale):
    i = pl.program_id(2)
    q0 = i * tq
    n_full = (q0 + 1) // tk
    nkc = (q0 + tq + tk - 1) // tk
    lane = lax.broadcasted_iota(jnp.int32, (1, LANES), 1)
    lane_lo = lane < HEAD_DIM
    qpos = q0 + lax.broadcasted_iota(jnp.int32, (1, tq), 1)
    qpos2 = jnp.concatenate([qpos, qpos], axis=1)
    neg_inf = jnp.float32(-jnp.inf)
    heads = range(q_ref.shape[1] // LANES)
    q_cat = []
    for g in heads:
        q_h = q_ref[:, g * LANES:(g + 1) * LANES]
        q_cat.append(jnp.concatenate([q_h * lane_lo.astype(BF16), q_h * (~lane_lo).astype(BF16)], axis=0))

    def logit_step(g, c, m, masked):
        s = lax.dot_general(k_ref[pl.ds(pl.multiple_of(c * tk, tk), tk), g * LANES:(g + 1) * LANES],
                            q_cat[g], NT_DIMS, preferred_element_type=F32)
        if masked:
            kpos = c * tk + lax.broadcasted_iota(jnp.int32, (tk, 1), 0)
            s = jnp.where(kpos <= qpos2, s, neg_inf)
        s_ref[g % s_ref.shape[0], c] = s
        return jnp.maximum(m, jnp.max(s.reshape(tk // SUBLANES, SUBLANES, 2 * tq), axis=0))

    def pv_matmul(g, c):
        return jnp.dot(vt_ref[c, g * LANES:(g + 1) * LANES, :], pe_ref[g % pe_ref.shape[0]],
                       preferred_element_type=F32)

    def exp_step(g, c, shift, l8):
        pe = jnp.exp2(s_ref[g % s_ref.shape[0], c] - shift)
        pe_ref[g % pe_ref.shape[0]] = pe.astype(BF16)
        return l8 + jnp.sum(pe.reshape(tk // SUBLANES, SUBLANES, 2 * tq), axis=0)

    def pv_step(g, c, shift, l8, acc):
        pe = jnp.exp2(s_ref[g % s_ref.shape[0], c] - shift)
        acc = acc + jnp.dot(vt_ref[c, g * LANES:(g + 1) * LANES, :], pe.astype(BF16),
                            preferred_element_type=F32)
        return l8 + jnp.sum(pe.reshape(tk // SUBLANES, SUBLANES, 2 * tq), axis=0), acc

    def run(body, init):
        carry = _loop_by_twos(0, n_full, functools.partial(body, masked=False), init)
        return lax.fori_loop(n_full, nkc, functools.partial(body, masked=True), carry)

    lam = lam_ref[0]
    m_init = jnp.full((SUBLANES, 2 * tq), neg_inf, F32)
    pv_init = (jnp.zeros((SUBLANES, 2 * tq), F32), jnp.zeros((LANES, 2 * tq), F32))

    def finish(g, l8, acc):
        o_all = acc / jnp.sum(l8, axis=0, keepdims=True)
        o = o_all[:, 0:tq] - lam * o_all[:, tq:2 * tq]
        o = o * lax.rsqrt(jnp.mean(o * o, axis=0, keepdims=True) + EPS)
        o_ref[:, g * LANES:(g + 1) * LANES] = (o.T * gsub_ref[...] * out_scale).astype(o_ref.dtype)

    width = s_ref.shape[0] // 2
    stages = [heads[lo:lo + width] for lo in range(0, len(heads), width)]

    def logit_stage(stage, c, ms, masked):
        return tuple(logit_step(g, c, m, masked) for g, m in zip(stage, ms))

    def pv_stage(stage, c, shifts, carry):
        return tuple(pv_step(g, c, sh, *lc) for g, sh, lc in zip(stage, shifts, carry))

    ms = run(lambda c, ms, masked: logit_stage(stages[0], c, ms, masked), (m_init,) * width)
    for prev, cur in zip(stages[:-1], stages[1:]):
        shifts = [jnp.max(m, axis=0, keepdims=True) for m in ms]

        def fused(c, carry, masked, prev=prev, cur=cur, shifts=shifts):
            pv, ms = carry
            return pv_stage(prev, c, shifts, pv), logit_stage(cur, c, ms, masked)

        pv, ms = run(fused, ((pv_init,) * width, (m_init,) * width))
        for g, lc in zip(prev, pv):
            finish(g, *lc)
    shifts = [jnp.max(m, axis=0, keepdims=True) for m in ms]

    last = stages[-1]

    def last_chunk(c, carry):
        l8s, accs = carry
        accs = tuple(acc + pv_matmul(g, c - 1) for g, acc in zip(last, accs))
        return tuple(exp_step(g, c, sh, l8) for g, sh, l8 in zip(last, shifts, l8s)), accs

    l8s = tuple(exp_step(g, 0, sh, pv_init[0]) for g, sh in zip(last, shifts))
    l8s, accs = lax.fori_loop(1, nkc, last_chunk, (l8s, (pv_init[1],) * width))
    for g, l8, acc in zip(last, l8s, accs):
        finish(g, l8, acc + pv_matmul(g, nkc - 1))


def _diff_attn_call(lam, qa, ka, vat, gsub, *, batch, seq, tq, tk, group, out_scale):
    t = qa.shape[0]
    w = group * LANES
    nq, nk = seq // tq, seq // tk
    q_spec = pl.BlockSpec((tq, w), lambda b, h, i: (b * nq + i, h))
    return pl.pallas_call(
        functools.partial(_diff_attn_body, tq=tq, tk=tk, out_scale=out_scale),
        grid=(batch, qa.shape[1] // w, nq),
        in_specs=[pl.BlockSpec(memory_space=pltpu.SMEM), q_spec,
                  pl.BlockSpec((seq, w), lambda b, h, i: (b, h)),
                  pl.BlockSpec((nk, w, tk), lambda b, h, i: (b, h, 0)), _full_spec(gsub)],
        out_specs=q_spec,
        out_shape=jax.ShapeDtypeStruct((t, qa.shape[1]), BF16),
        scratch_shapes=[pltpu.VMEM((group, nk, tk, 2 * tq), F32), pltpu.VMEM((group // 2, tk, 2 * tq), BF16)],
        compiler_params=_cparams("parallel", "parallel", "arbitrary"),
        name="diff_attn",
    )(lam, qa, ka, vat, gsub)


def _loop_by_twos(lo, hi, step, carry):
    pairs = (hi - lo) // 2
    carry = lax.fori_loop(0, pairs, lambda i, cr: step(lo + 2 * i + 1, step(lo + 2 * i, cr)), carry)
    return lax.fori_loop(lo + 2 * pairs, hi, step, carry)


def _key_to_float(key):
    bits = key ^ ((key >> 31) & jnp.int32(0x7FFFFFFF))
    return lax.bitcast_convert_type(bits, F32)


def _dsa_body(qi_ref, wi_ref, qb_ref, kik_ref, kb_ref, vbt_ref, tri_ref, o_ref, sc_ref, bias_ref, s_ref, pe_ref,
              *, tq, tk, k_sel):
    i = pl.program_id(1)
    q0 = i * tq
    nkc = (q0 + tq + tk - 1) // tk
    qpos = q0 + lax.broadcasted_iota(jnp.int32, (1, tq), 1)
    lane = lax.broadcasted_iota(jnp.int32, (1, LANES), 1)
    lane_lo = lane < HEAD_DIM
    tiles = tk // LANES
    neg_inf = jnp.float32(-jnp.inf)

    wi_t = wi_ref[...].T
    qi = qi_ref[...]
    q_heads = []
    for h in range(IDX_HEADS):
        pair = qi[:, (h // 2) * LANES:(h // 2 + 1) * LANES]
        q_heads.append(pair * (lane_lo if h % 2 == 0 else ~lane_lo).astype(pair.dtype))
    qi_cat = jnp.concatenate(q_heads, axis=0)
    w_heads = [wi_t[IDX_DIM + h:IDX_DIM + h + 1, :] for h in range(IDX_HEADS)]

    def score_chunk(c, carry):
        kk = kik_ref[pl.ds(pl.multiple_of(c * tk, tk), tk), :]
        s = lax.dot_general(kk, qi_cat, NT_DIMS, preferred_element_type=F32)
        acc = jnp.zeros((tk, tq), F32)
        for h in range(IDX_HEADS):
            acc = acc + jnp.maximum(s[:, h * tq:(h + 1) * tq], 0.0) * w_heads[h]
        kpos = c * tk + lax.broadcasted_iota(jnp.int32, (tk, 1), 0)
        sc_ref[c] = jnp.where(kpos <= qpos, acc, neg_inf)
        return carry

    _loop_by_twos(0, nkc, score_chunk, 0)

    acc_rows = 4 * SUBLANES

    def count_ge(cand):
        def body(c, acc):
            for r in range(tk // acc_rows):
                acc = acc + jnp.where(sc_ref[c, r * acc_rows:(r + 1) * acc_rows, :] >= cand, 1.0, 0.0)
            return acc
        acc = lax.fori_loop(0, nkc, body, jnp.zeros((acc_rows, tq), F32))
        return jnp.sum(acc, axis=0, keepdims=True)

    ksel = float(k_sel)

    def count_at(cand_key):
        return jnp.where(cand_key <= KEY_NEG_INF, ksel, count_ge(_key_to_float(cand_key)))

    zero_key = jnp.zeros((1, tq), jnp.int32)
    cnt = count_at(zero_key)
    thr = jnp.where(cnt >= ksel, zero_key, INT32_MIN)
    n_above = jnp.where(cnt >= ksel, 0.0, cnt)

    def bit_step(b, carry):
        thr, n_above = carry
        cand = thr + jnp.left_shift(jnp.int32(1), 30 - b)
        cnt = count_at(cand)
        return jnp.where(cnt >= ksel, cand, thr), jnp.where(cnt >= ksel, n_above, cnt)

    thr, n_above = lax.fori_loop(0, 31, bit_step, (thr, n_above))
    thr_f = _key_to_float(thr)
    above_f = _key_to_float(thr + 1)
    need = ksel - n_above
    tri = tri_ref[...]

    def select_chunk(c, taken):
        for t in range(tiles):
            st = sc_ref[c, t * LANES:(t + 1) * LANES, :]
            is_above = st >= above_f
            tied = jnp.where(is_above, 0.0, jnp.where(st >= thr_f, 1.0, 0.0))
            prefix = jnp.dot(tri, tied.astype(BF16), preferred_element_type=F32)
            tie_sel = jnp.where((taken + prefix) <= need, tied, 0.0)
            sel = jnp.where(is_above, 0.0, jnp.where(tie_sel > 0.0, 0.0, neg_inf))
            kpos = c * tk + t * LANES + lax.broadcasted_iota(jnp.int32, (LANES, 1), 0)
            bias_ref[c, t * LANES:(t + 1) * LANES, :] = jnp.where(kpos <= qpos, sel, neg_inf)
            taken = taken + prefix[LANES - 1:LANES, :]
        return taken

    _loop_by_twos(0, nkc, select_chunk, jnp.zeros((1, tq), F32))

    group = s_ref.shape[0]
    for p0 in range(0, qb_ref.shape[1] // LANES, group):
        pairs = range(p0, p0 + group)
        q_cat = []
        for p in pairs:
            q_pair = qb_ref[:, p * LANES:(p + 1) * LANES]
            q_cat.append(jnp.concatenate([q_pair * lane_lo.astype(BF16), q_pair * (~lane_lo).astype(BF16)],
                                         axis=0))

        def logit_chunk(c, carry, pairs=pairs, q_cat=q_cat):
            bias = bias_ref[c]
            bias2 = jnp.concatenate([bias, bias], axis=1)
            out = []
            for g, (p, m) in enumerate(zip(pairs, carry)):
                kc = kb_ref[pl.ds(pl.multiple_of(c * tk, tk), tk), p * LANES:(p + 1) * LANES]
                s = lax.dot_general(kc, q_cat[g], NT_DIMS, preferred_element_type=F32) + bias2
                s_ref[g, c] = s
                out.append(jnp.maximum(m, jnp.max(s.reshape(tk // SUBLANES, SUBLANES, 2 * tq), axis=0)))
            return tuple(out)

        m8 = _loop_by_twos(0, nkc, logit_chunk,
                           tuple(jnp.full((SUBLANES, 2 * tq), neg_inf, F32) for _ in pairs))
        shift = [jnp.max(m, axis=0, keepdims=True) for m in m8]

        def exp_stage(c, l8s, shift=shift):
            out = []
            for g, l8 in enumerate(l8s):
                pe = jnp.exp2(s_ref[g, c] - shift[g])
                pe_ref[g] = pe.astype(BF16)
                out.append(l8 + jnp.sum(pe.reshape(tk // SUBLANES, SUBLANES, 2 * tq), axis=0))
            return tuple(out)

        def pv_stage(c, accs, pairs=pairs):
            return tuple(acc + jnp.dot(vbt_ref[c, p * LANES:(p + 1) * LANES, :], pe_ref[g],
                                       preferred_element_type=F32)
                         for g, (p, acc) in enumerate(zip(pairs, accs)))

        def pv_chunk(c, carry):
            l8s, accs = carry
            accs = pv_stage(c - 1, accs)
            return exp_stage(c, l8s), accs

        l8s = exp_stage(0, tuple(jnp.zeros((SUBLANES, 2 * tq), F32) for _ in pairs))
        l8s, accs = lax.fori_loop(1, nkc, pv_chunk,
                                  (l8s, tuple(jnp.zeros((LANES, 2 * tq), F32) for _ in pairs)))
        accs = pv_stage(nkc - 1, accs)
        for p, l8, acc in zip(pairs, l8s, accs):
            o_all = acc / jnp.sum(l8, axis=0, keepdims=True)
            o_t = jnp.concatenate([o_all[0:HEAD_DIM, 0:tq], o_all[HEAD_DIM:LANES, tq:2 * tq]], axis=0)
            o_ref[:, p * LANES:(p + 1) * LANES] = o_t.T.astype(o_ref.dtype)


def _dsa_call(qi, wi, qb, kik, kb, vbt, tri, *, batch, seq, tq, tk, k_sel):
    t = qb.shape[0]
    nq, nk = seq // tq, seq // tk
    row = lambda w: pl.BlockSpec((tq, w), lambda b, i: (b * nq + i, 0))
    once = pl.Buffered(1)
    seqb = lambda w: pl.BlockSpec((seq, w), lambda b, i: (b, 0), pipeline_mode=once)
    vbt_spec = pl.BlockSpec((nk,) + vbt.shape[1:], lambda b, i: (b, 0, 0), pipeline_mode=once)
    return pl.pallas_call(
        functools.partial(_dsa_body, tq=tq, tk=tk, k_sel=k_sel),
        grid=(batch, nq),
        in_specs=[row(qi.shape[1]), row(LANES), row(qb.shape[1]), seqb(LANES), seqb(kb.shape[1]),
                  vbt_spec, _full_spec(tri)],
        out_specs=row(qb.shape[1]),
        out_shape=jax.ShapeDtypeStruct((t, qb.shape[1]), BF16),
        scratch_shapes=[pltpu.VMEM((nk, tk, tq), F32), pltpu.VMEM((nk, tk, tq), F32),
                        pltpu.VMEM((4, nk, tk, 2 * tq), F32), pltpu.VMEM((4, tk, 2 * tq), BF16)],
        compiler_params=_cparams("parallel", "arbitrary"),
        name="dsa",
    )(qi, wi, qb, kik, kb, vbt, tri)


def _memkv_body(mem_ref, g_ref, w_ref, gk_ref, k_ref, v_ref):
    hm = _rms(mem_ref[...], g_ref[...]).astype(BF16)
    kv = jnp.dot(hm, w_ref[...], preferred_element_type=F32)
    half = kv.shape[1] // 2
    for h in range(X_HEADS):
        kh = kv[:, h * X_HEAD_DIM:(h + 1) * X_HEAD_DIM]
        k_ref[:, h * X_HEAD_DIM:(h + 1) * X_HEAD_DIM] = _rms(kh, gk_ref[...]).astype(BF16)
    v_ref[...] = kv[:, half:].astype(BF16)


def _memkv_call(mem2, g, w, gk):
    rows, d = mem2.shape
    half = w.shape[1] // 2
    tm = min(rows, 512)
    row = lambda wd: pl.BlockSpec((tm, wd), lambda i: (i, 0))
    return pl.pallas_call(
        _memkv_body,
        grid=(rows // tm,),
        in_specs=[row(d), _full_spec(g), _full_spec(w), _full_spec(gk)],
        out_specs=[row(half), row(half)],
        out_shape=[jax.ShapeDtypeStruct((rows, half), BF16)] * 2,
        compiler_params=_cparams("parallel"),
        name="memkv",
    )(mem2, g, w, gk)


def _xattn_tail(x1, gx_ref, wq_ref, gq_ref, kx_ref, vx_ref, wo_ref):
    hx = _rms(x1, gx_ref[...]).astype(BF16)
    q = jnp.dot(hx, wq_ref[...], preferred_element_type=F32)
    outs = []
    for h in range(X_HEADS):
        cols = slice(h * X_HEAD_DIM, (h + 1) * X_HEAD_DIM)
        qh = _rms(q[:, cols], gq_ref[...]).astype(BF16)
        s = lax.dot_general(qh, kx_ref[0, :, cols], NT_DIMS, preferred_element_type=F32)
        p = jnp.exp(s - jnp.max(s, axis=-1, keepdims=True))
        o = (jnp.dot(p.astype(BF16), vx_ref[0, :, cols], preferred_element_type=F32)
             / jnp.sum(p, axis=-1, keepdims=True))
        outs.append(o.astype(BF16))
    o = jnp.concatenate(outs, axis=1)
    return x1 + jnp.dot(o, wo_ref[...], preferred_element_type=F32)


def _store_with_mlp_norm(x2, gm_ref, o_ref, hn_ref):
    o_ref[...] = x2
    hn_ref[...] = _rms(x2, gm_ref[...]).astype(BF16)


def _out0_body(x_ref, oa_ref, ob_ref, woa_ref, wob_ref, gx_ref, wq_ref, gq_ref, kx_ref, vx_ref, wo_ref,
               gm_ref, o_ref, hn_ref):
    mix = (jnp.dot(oa_ref[...], woa_ref[...], preferred_element_type=F32)
           + jnp.dot(ob_ref[...], wob_ref[...], preferred_element_type=F32))
    x2 = _xattn_tail(x_ref[...] + mix, gx_ref, wq_ref, gq_ref, kx_ref, vx_ref, wo_ref)
    _store_with_mlp_norm(x2, gm_ref, o_ref, hn_ref)


def _out0_call(x2, oa, ob, woa, wob, gx, wq, gq, kx, vx, wo, gm, *, seq, tm):
    t, d = x2.shape
    row = lambda w: pl.BlockSpec((tm, w), lambda i: (i, 0))
    mem_spec = pl.BlockSpec((1,) + kx.shape[1:], lambda i: ((i * tm) // seq, 0, 0))
    fulls = [woa, wob, gx, wq, gq]
    return pl.pallas_call(
        _out0_body,
        grid=(t // tm,),
        in_specs=[row(d), row(oa.shape[1]), row(ob.shape[1])] + [_full_spec(a) for a in fulls]
                 + [mem_spec, mem_spec, _full_spec(wo), _full_spec(gm)],
        out_specs=[row(d), row(d)],
        out_shape=[jax.ShapeDtypeStruct((t, d), F32), jax.ShapeDtypeStruct((t, d), BF16)],
        compiler_params=_cparams("parallel"),
        name="out0",
    )(x2, oa, ob, woa, wob, gx, wq, gq, kx, vx, wo, gm)


def _mlp_body(x_ref, hn_ref, wup_ref, wdn_ref, o_ref, acc_ref):
    j = pl.program_id(1)
    u = jnp.dot(hn_ref[...], wup_ref[...], preferred_element_type=F32)
    a = jnp.square(jnp.maximum(u, 0.0)).astype(BF16)
    prev = jnp.where(j == 0, 0.0, acc_ref[...])
    acc_ref[...] = prev + jnp.dot(a, wdn_ref[...], preferred_element_type=F32)

    @pl.when(j == pl.num_programs(1) - 1)
    def _():
        o_ref[...] = x_ref[...] + acc_ref[...]


def _mlp_call(x2, hn, wup, wdn, layer, *, tm, tf):
    t, d = x2.shape
    ff = wup.shape[2]
    row = pl.BlockSpec((tm, d), lambda i, j: (i, 0))
    return pl.pallas_call(
        _mlp_body,
        grid=(t // tm, ff // tf),
        in_specs=[row, row,
                  pl.BlockSpec((None, d, tf), lambda i, j: (layer, 0, j)),
                  pl.BlockSpec((None, tf, d), lambda i, j: (layer, j, 0))],
        out_specs=row,
        out_shape=jax.ShapeDtypeStruct((t, d), F32),
        scratch_shapes=[pltpu.VMEM((tm, d), F32)],
        compiler_params=_cparams("parallel", "arbitrary"),
        name="mlp",
    )(x2, hn, wup, wdn)


def _mix1_body(x_ref, xh_ref, gmix_ref, win_ref, cw_ref, wpool_ref, ps_ref, wout_ref,
               gx_ref, wq_ref, gq_ref, kx_ref, vx_ref, wo_ref, gm_ref, o_ref, hn_ref, *, tm, seq):
    i = pl.program_id(0)
    x = x_ref[...]
    xf = jnp.concatenate([xh_ref[...], x], axis=0)
    hf = _rms(xf, gmix_ref[...]).astype(BF16)
    cw = win_ref.shape[1] // 4
    proj = lambda n: jnp.dot(hf, win_ref[:, n * cw:(n + 1) * cw], preferred_element_type=F32)
    pos0 = (i * tm) % seq
    rid = lax.broadcasted_iota(jnp.int32, (tm + POOL_HALO, 1), 0)
    keep = jnp.where(rid >= POOL_HALO, 1.0, jnp.where(pos0 > 0, 1.0, 0.0))

    u = proj(1) * proj(2) * keep
    z = proj(3) * keep
    gb = proj(0)[POOL_HALO:]

    taps = cw_ref[...]
    conv = taps[CONV_W - 1:CONV_W, :] * u[POOL_HALO:]
    for back in range(1, CONV_W):
        conv = conv + taps[CONV_W - 1 - back:CONV_W - back, :] * pltpu.roll(u, back, axis=0)[POOL_HALO:]
    yc = gb * conv

    pos = pos0 + lax.broadcasted_iota(jnp.int32, (tm, 1), 0)
    gw = cw // len(POOL_WINDOWS)
    parts = []
    for g, w in enumerate(POOL_WINDOWS):
        zg = z[:, g * gw:(g + 1) * gw]
        s = zg
        sh = 1
        while sh < w:
            s = s + pltpu.roll(s, sh, axis=0)
            sh *= 2
        cnt = jnp.minimum(pos + 1, w).astype(F32)
        pooled = s[POOL_HALO:] / cnt - zg[POOL_HALO:]
        parts.append(jnp.dot(pooled.astype(BF16), wpool_ref[g], preferred_element_type=F32))
    yd = jnp.concatenate(parts, axis=1) * ps_ref[...]

    mix = (jnp.dot(yc.astype(BF16), wout_ref[0:cw, :], preferred_element_type=F32)
           + jnp.dot(yd.astype(BF16), wout_ref[cw:2 * cw, :], preferred_element_type=F32))
    x2 = _xattn_tail(x + mix, gx_ref, wq_ref, gq_ref, kx_ref, vx_ref, wo_ref)
    _store_with_mlp_norm(x2, gm_ref, o_ref, hn_ref)


def _mix1_call(x2, gmix, win, cw, wpool, ps, wout, gx, wq, gq, kx, vx, wo, gm, *, seq, tm):
    t, d = x2.shape
    row = pl.BlockSpec((tm, d), lambda i: (i, 0))
    halo = pl.BlockSpec((POOL_HALO, d), lambda i: (jnp.maximum(i * (tm // POOL_HALO) - 1, 0), 0))
    mem_spec = pl.BlockSpec((1,) + kx.shape[1:], lambda i: ((i * tm) // seq, 0, 0))
    fulls = [gmix, win, cw, wpool, ps, wout, gx, wq, gq]
    return pl.pallas_call(
        functools.partial(_mix1_body, tm=tm, seq=seq),
        grid=(t // tm,),
        in_specs=[row, halo] + [_full_spec(a) for a in fulls]
                 + [mem_spec, mem_spec, _full_spec(wo), _full_spec(gm)],
        out_specs=[row, row],
        out_shape=[jax.ShapeDtypeStruct((t, d), F32), jax.ShapeDtypeStruct((t, d), BF16)],
        compiler_params=_cparams("parallel"),
        name="mix1",
    )(x2, x2, gmix, win, cw, wpool, ps, wout, gx, wq, gq, kx, vx, wo, gm)


def _row(v, reps=1):
    return jnp.tile(v.astype(F32), reps).reshape(1, -1)


def kernel(x, mem, positions, g_mix, g_xattn, g_mem, g_mlp, wq_x, wkv_x, gq_x, gk_x, wo_x, w_up, w_down,
           w_in_e, gq_a, gk_a, lam_q1, lam_k1, lam_q2, lam_k2, g_sub_a, g_kv_b, w_kv_up_b, gq_b, gk_b,
           g_kidx, w_out_e, w_in_o, conv_w, w_pool, pool_scale, w_out_o):
    batch, seq, d = x.shape
    t = batch * seq
    k_sel = min(TOPK_MAX, seq // 4)
    tm = min(1024, seq)
    tq_b, tk_b = min(256, seq), min(512, seq)
    assert seq % tm == 0 and seq % tq_b == 0 and tm % tk_b == 0 and tk_b >= k_sel

    x2 = x.reshape(t, d)
    pos = positions.reshape(t // tm, 1, tm)
    inv_freq = (ROPE_THETA ** (-jnp.arange(0, ROT_DIM, 2, dtype=F32) / ROT_DIM)).reshape(ROT_HALF, 1)
    mem2 = mem.reshape(batch * mem.shape[1], d)
    depth = g_mix.shape[0]
    w_up_b, w_down_b = w_up.astype(BF16), w_down.astype(BF16)

    for layer in range(depth):
        j = layer // 2
        kx, vx = _memkv_call(mem2, _row(g_mem[layer]), wkv_x[layer].astype(BF16), _row(gk_x[layer]))
        kx = kx.reshape(batch, mem.shape[1], -1)
        vx = vx.reshape(batch, mem.shape[1], -1)
        tail = (_row(g_xattn[layer]), wq_x[layer].astype(BF16), _row(gq_x[layer]) * (X_HEAD_DIM ** -0.5),
                kx, vx, wo_x[layer].astype(BF16), _row(g_mlp[layer]))
        if layer % 2 == 0:
            lam_init = 0.8 - 0.6 * float(np.exp(-0.3 * layer))
            lam = (jnp.exp(jnp.sum(lam_q1[j].astype(F32) * lam_k1[j].astype(F32)))
                   - jnp.exp(jnp.sum(lam_q2[j].astype(F32) * lam_k2[j].astype(F32))) + lam_init)
            w_in = w_in_e[j]
            n_main = 2304
            n_idx = w_in.shape[1] - n_main
            widx = jnp.pad(w_in[:, n_main:], ((0, 0), (0, 3 * LANES - n_idx))).astype(BF16)
            heads8 = 512 // HEAD_DIM
            bd = jnp.kron(jnp.eye(heads8, dtype=F32), jnp.ones((HEAD_DIM, HEAD_DIM), F32)).astype(BF16)
            gkidx = jnp.pad(g_kidx[j].astype(F32), (0, LANES - IDX_DIM)).reshape(1, LANES)
            qa, ka, vat, qb, kb, vbt, qi, kik, wi = _in0_call(
                x2, _row(g_mix[layer]), w_in[:, :n_main].astype(BF16), widx, w_kv_up_b[j].astype(BF16),
                pos, inv_freq, bd, _row(gq_a[j], heads8), _row(gk_a[j], heads8), _row(gq_b[j], heads8),
                _row(gk_b[j], heads8), _row(g_kv_b[j]), gkidx, tm=tm, tk=tk_b)
            out_a = _diff_attn_call(lam.reshape(1).astype(F32), qa, ka, vat, _row(g_sub_a[j]),
                                    batch=batch, seq=seq, tq=tq_b, tk=tk_b, group=qa.shape[1] // LANES, out_scale=1.0 - lam_init)
            tri = jnp.tril(jnp.ones((LANES, LANES), F32)).astype(BF16)
            out_b = _dsa_call(qi, wi, qb, kik, kb, vbt, tri, batch=batch, seq=seq, tq=tq_b, tk=tk_b, k_sel=k_sel)
            w_out = w_out_e[j].astype(BF16)
            x2, hn = _out0_call(x2, out_a, out_b, w_out[:512], w_out[512:], *tail, seq=seq, tm=tm)
        else:
            taps = jnp.pad(conv_w[j].astype(F32), ((0, SUBLANES - CONV_W), (0, 0)))
            x2, hn = _mix1_call(x2, _row(g_mix[layer]), w_in_o[j].astype(BF16), taps, w_pool[j].astype(BF16),
                                _row(pool_scale[j]), w_out_o[j].astype(BF16), *tail, seq=seq, tm=tm)
        x2 = _mlp_call(x2, hn, w_up_b, w_down_b, layer, tm=min(1024, t), tf=2048)
    return x2.reshape(batch, seq, d)
```
